```python
import jax, jax.numpy as jnp
from jax import lax
import numpy as np

D_MODEL = 1024
BATCH = 32
SEQ = 256
DEPTH = 4
DEC_BATCH = 2
DEC_SEQ = 2048
PAST_LEN = 256

GRID_W = 64
N_HEADS = 8
N_KV_HEADS = 2
HEAD_DIM = 64
ATTN_W = N_HEADS * HEAD_DIM
KV_W = N_KV_HEADS * HEAD_DIM
POOL_WINDOWS = (2, 4, 8, 16)
POOL_GROUPS = len(POOL_WINDOWS)
POOL_GROUP_W = 128
POOL_W = POOL_GROUPS * POOL_GROUP_W
A_IN_W = ATTN_W + 2 * KV_W + POOL_W
MIX_W = ATTN_W + POOL_W
CONV_W = D_MODEL
CONV_K = 3
PEER_HEADS = 8
PEER_NKEYS = 128
PEER_EXPERTS = PEER_NKEYS * PEER_NKEYS
PEER_TOPK = 16
PEER_DK = 128
PEER_BLOCK = 128
Q_BLOCK = 128
ROPE_THETA = 10000.0
N_ATTN_LAYERS = (DEPTH + 1) // 2
N_CONV_LAYERS = DEPTH // 2
EPS = 1e-6

kernel_name = "hybrid_diffusion_attn_pool_conv_peer_step"


def rms_norm(x, g):
    xf = x.astype(jnp.float32)
    y = xf * lax.rsqrt(jnp.mean(xf * xf, axis=-1, keepdims=True) + EPS)
    return (y * g.astype(jnp.float32)).astype(x.dtype)


def modulate(x, shift, scale):
    return x * (1 + scale) + shift


def axial_rope(x):
    T = x.shape[1]
    rows_n = T // GRID_W
    row = jnp.repeat(jnp.arange(rows_n), GRID_W)
    col = jnp.tile(jnp.arange(GRID_W), rows_n)
    half = HEAD_DIM // 2
    nf = half // 2
    inv = ROPE_THETA ** (-jnp.arange(nf, dtype=jnp.float32) / nf)

    def rot(seg, pos):
        ang = pos.astype(jnp.float32)[:, None] * inv[None, :]
        cos = jnp.cos(ang)[None, :, None, :].astype(seg.dtype)
        sin = jnp.sin(ang)[None, :, None, :].astype(seg.dtype)
        s1, s2 = seg[..., :nf], seg[..., nf:]
        return jnp.concatenate([s1 * cos - s2 * sin, s2 * cos + s1 * sin], axis=-1)

    return jnp.concatenate([rot(x[..., :half], row), rot(x[..., half:], col)], axis=-1)


def block_attention(q, k, v):
    B, T, H, Dh = q.shape
    G = H // N_KV_HEADS
    nb = T // Q_BLOCK
    qb = q.reshape(B, nb, Q_BLOCK, N_KV_HEADS, G, Dh).transpose(1, 0, 2, 3, 4, 5)
    scale = HEAD_DIM ** -0.5

    def one(qblk):
        s = jnp.einsum("bqkgd,bskd->bkgqs", qblk, k).astype(jnp.float32) * scale
        p = jax.nn.softmax(s, axis=-1).astype(v.dtype)
        return jnp.einsum("bkgqs,bskd->bqkgd", p, v)

    o = lax.map(one, qb)
    return o.transpose(1, 0, 2, 3, 4, 5).reshape(B, T, H * Dh)


def pool_mixer(p, w, scale):
    B, T, _ = p.shape
    pg = p.reshape(B, T, POOL_GROUPS, POOL_GROUP_W)
    cs = jnp.pad(jnp.cumsum(pg.astype(jnp.float32), axis=1), ((0, 0), (1, 0), (0, 0), (0, 0)))
    t = jnp.arange(T)
    means = []
    for g, win in enumerate(POOL_WINDOWS):
        lo = jnp.clip(t - win // 2, 0, T)
        hi = jnp.clip(t + win - win // 2, 0, T)
        tot = cs[:, hi, g] - cs[:, lo, g]
        cnt = (hi - lo).astype(jnp.float32)[None, :, None]
        means.append(tot / cnt)
    mean = jnp.stack(means, axis=2).astype(p.dtype)
    y = jnp.einsum("btgc,gcd->btgd", mean - pg, w)
    return y.reshape(B, T, POOL_W) * scale


def attn_pool_mixer(h, w_in, w_out, q_g, k_g, pool_w, pool_s, ctx_k=None, ctx_v=None):
    B, T, _ = h.shape
    z = h @ w_in
    q = z[..., :ATTN_W].reshape(B, T, N_HEADS, HEAD_DIM)
    k = z[..., ATTN_W:ATTN_W + KV_W].reshape(B, T, N_KV_HEADS, HEAD_DIM)
    v = z[..., ATTN_W + KV_W:ATTN_W + 2 * KV_W].reshape(B, T, N_KV_HEADS, HEAD_DIM)
    p = z[..., ATTN_W + 2 * KV_W:]
    q = rms_norm(q, q_g)
    k = rms_norm(k, k_g)
    if ctx_k is None:
        a = block_attention(q, k, v)
    else:
        keys = jnp.concatenate([axial_rope(k), ctx_k.astype(k.dtype)], axis=1)
        vals = jnp.concatenate([v, ctx_v.astype(v.dtype)], axis=1)
        a = block_attention(axial_rope(q), keys, vals)
    out = jnp.concatenate([a, pool_mixer(p, pool_w, pool_s)], axis=-1) @ w_out
    return out, k, v


def conv_mixer(h, w_in, conv_w, w_out):
    gb, gc, hh = jnp.split(h @ w_in, 3, axis=-1)
    u = gc * hh
    T = u.shape[1]
    pad = CONV_K // 2
    up = jnp.pad(u, ((0, 0), (pad, CONV_K - 1 - pad), (0, 0)))
    y = up[:, 0:T] * conv_w[0]
    for j in range(1, CONV_K):
        y = y + up[:, j:j + T] * conv_w[j]
    return (gb * y) @ w_out


def peer(h, wq, subkeys, U, V):
    B, T, D = h.shape
    hb = h.reshape(-1, PEER_BLOCK, D)

    def one(xb):
        q = (xb @ wq).reshape(PEER_BLOCK, PEER_HEADS, 2, PEER_DK)
        s = jnp.einsum("nhpk,pek->nhpe", q, subkeys)
        sv, si = lax.top_k(s, PEER_TOPK)
        cand = (sv[:, :, 0, :, None] + sv[:, :, 1, None, :]).reshape(PEER_BLOCK, PEER_HEADS, PEER_TOPK * PEER_TOPK)
        fv, fi = lax.top_k(cand, PEER_TOPK)
        i1 = jnp.take_along_axis(si[:, :, 0], fi // PEER_TOPK, axis=-1)
        i2 = jnp.take_along_axis(si[:, :, 1], fi % PEER_TOPK, axis=-1)
        idx = i1 * PEER_NKEYS + i2
        g = jax.nn.softmax(fv.astype(jnp.float32), axis=-1).astype(xb.dtype)
        act = jax.nn.gelu(jnp.einsum("nhed,nd->nhe", U[idx], xb), approximate=False)
        return jnp.einsum("nhe,nhed->nd", g * act, V[idx])

    return lax.map(one, hb).reshape(B, T, D)


def setup_inputs(seed: int = 0) -> dict:
    key = jax.random.key(seed)
    ks = jax.random.split(key, 24)

    def nrm(k, shape, s):
        return jax.random.normal(k, shape, jnp.float32) * s

    D = D_MODEL
    return {
        "x_prompt": nrm(ks[0], (BATCH, SEQ, D), 1.0),
        "x_sample": nrm(ks[1], (DEC_BATCH, DEC_SEQ, D), 1.0),
        "cache_k": nrm(ks[2], (DEC_BATCH, N_ATTN_LAYERS, PAST_LEN, N_KV_HEADS, HEAD_DIM), 1.0),
        "cache_v": nrm(ks[3], (DEC_BATCH, N_ATTN_LAYERS, PAST_LEN, N_KV_HEADS, HEAD_DIM), 1.0),
        "c": nrm(ks[4], (DEC_BATCH, D), 1.0),
        "c_ctx": nrm(ks[5], (D,), 1.0),
        "mod_w": nrm(ks[6], (DEPTH, D, 6 * D), 0.5 * D ** -0.5),
        "mod_b": nrm(ks[7], (DEPTH, 6 * D), 0.02),
        "norm1_g": 1.0 + nrm(ks[8], (DEPTH, D), 0.02),
        "norm2_g": 1.0 + nrm(ks[9], (DEPTH, D), 0.02),
        "a_in_w": nrm(ks[10], (N_ATTN_LAYERS, D, A_IN_W), D ** -0.5),
        "a_out_w": nrm(ks[11], (N_ATTN_LAYERS, MIX_W, D), MIX_W ** -0.5),
        "q_norm_g": 1.0 + nrm(ks[12], (N_ATTN_LAYERS, HEAD_DIM), 0.02),
        "k_norm_g": 1.0 + nrm(ks[13], (N_ATTN_LAYERS, HEAD_DIM), 0.02),
        "pool_w": nrm(ks[14], (N_ATTN_LAYERS, POOL_GROUPS, POOL_GROUP_W, POOL_GROUP_W), POOL_GROUP_W ** -0.5),
        "pool_scale": 0.5 + nrm(ks[15], (N_ATTN_LAYERS, POOL_W), 0.1),
        "c_in_w": nrm(ks[16], (N_CONV_LAYERS, D, 3 * CONV_W), D ** -0.5),
        "c_conv_w": nrm(ks[17], (N_CONV_LAYERS, CONV_K, CONV_W), CONV_K ** -0.5),
        "c_out_w": nrm(ks[18], (N_CONV_LAYERS, CONV_W, D), CONV_W ** -0.5),
        "peer_wq": nrm(ks[19], (DEPTH, D, PEER_HEADS * 2 * PEER_DK), D ** -0.5),
        "peer_subkeys": nrm(ks[20], (DEPTH, 2, PEER_NKEYS, PEER_DK), PEER_DK ** -0.5),
        "peer_u": nrm(ks[21], (DEPTH, PEER_EXPERTS, D), D ** -0.5),
        "peer_v": nrm(ks[22], (DEPTH, PEER_EXPERTS, D), 0.5),
        "final_g": 1.0 + nrm(ks[23], (D,), 0.02),
    }


def reference(x_prompt, x_sample, cache_k, cache_v, c, c_ctx, mod_w, mod_b, norm1_g, norm2_g,
              a_in_w, a_out_w, q_norm_g, k_norm_g, pool_w, pool_scale, c_in_w, c_conv_w, c_out_w,
              peer_wq, peer_subkeys, peer_u, peer_v, final_g):
    xp, xs = x_prompt, x_sample
    new_k, new_v = [], []
    for l in range(DEPTH):
        mp = jnp.split(jax.nn.silu(c_ctx) @ mod_w[l] + mod_b[l], 6, axis=-1)
        ms = [m[:, None, :] for m in jnp.split(jax.nn.silu(c) @ mod_w[l] + mod_b[l], 6, axis=-1)]
        hp = modulate(rms_norm(xp, norm1_g[l]), mp[0], mp[1])
        hs = modulate(rms_norm(xs, norm1_g[l]), ms[0], ms[1])
        if l % 2 == 0:
            i = l // 2
            yp, kp, vp = attn_pool_mixer(hp, a_in_w[i], a_out_w[i], q_norm_g[i], k_norm_g[i], pool_w[i], pool_scale[i])
            ys, _, _ = attn_pool_mixer(hs, a_in_w[i], a_out_w[i], q_norm_g[i], k_norm_g[i], pool_w[i], pool_scale[i],
                                       ctx_k=cache_k[:, i], ctx_v=cache_v[:, i])
            new_k.append(kp)
            new_v.append(vp)
        else:
            j = l // 2
            yp = conv_mixer(hp, c_in_w[j], c_conv_w[j], c_out_w[j])
            ys = conv_mixer(hs, c_in_w[j], c_conv_w[j], c_out_w[j])
        xp = xp + mp[2] * yp
        xs = xs + ms[2] * ys
        hp = modulate(rms_norm(xp, norm2_g[l]), mp[3], mp[4])
        hs = modulate(rms_norm(xs, norm2_g[l]), ms[3], ms[4])
        xp = xp + mp[5] * peer(hp, peer_wq[l], peer_subkeys[l], peer_u[l], peer_v[l])
        xs = xs + ms[5] * peer(hs, peer_wq[l], peer_subkeys[l], peer_u[l], peer_v[l])
    y_prompt = rms_norm(xp, final_g)
    y_sample = rms_norm(xs, final_g)
    new_cache_k = jnp.stack(new_k, axis=1)
    new_cache_v = jnp.stack(new_v, axis=1)
    return (y_prompt, y_sample, new_cache_k, new_cache_v)
```

```python
import functools

import numpy as np
import jax
import jax.numpy as jnp
from jax import lax
from jax.experimental import pallas as pl
from jax.experimental.pallas import tpu as pltpu

F32 = jnp.float32
BF16 = jnp.bfloat16

D = 1024
N_PROMPT_SEQ, SEQ = 32, 256
N_SAMPLE_SEQ, DEC_SEQ = 2, 2048
PAST = 256
NP_TOK = N_PROMPT_SEQ * SEQ
NS_TOK = N_SAMPLE_SEQ * DEC_SEQ
N_TOK = NP_TOK + NS_TOK
DEPTH = 4
GRID_W = 64
N_HEADS, N_KV, HEAD_DIM = 8, 2, 64
ATTN_W, KV_W, POOL_W = 512, 128, 512
POOL_WINDOWS = (2, 4, 8, 16)
A_IN_W = ATTN_W + 2 * KV_W + POOL_W
PEER_HEADS, PEER_NKEYS, PEER_TOPK, PEER_DK = 8, 128, 16, 128
PEER_EXPERTS = PEER_NKEYS * PEER_NKEYS
ROPE_THETA = 10000.0
EPS = 1e-6

TM = 256
HALO = 8
TR = 128
G_PITCH = 136
TE = 1024
EC = 512
VMEM_LIMIT = 56 * 1024 * 1024


def _cparams(sem):
    return pltpu.CompilerParams(dimension_semantics=sem, vmem_limit_bytes=VMEM_LIMIT)


def _mod_row(tile, tile_tokens):
    start = tile * tile_tokens
    return (start >= NP_TOK).astype(jnp.int32) + (start >= NP_TOK + DEC_SEQ).astype(jnp.int32)


def _modnorm(x, g, shift, scale):
    ms = jnp.mean(x * x, axis=-1, keepdims=True)
    y = x * lax.rsqrt(ms + EPS) * g
    return y * (1.0 + scale) + shift


def _split_bf16(a):
    hi = a.astype(BF16)
    lo = (a - hi.astype(F32)).astype(BF16)
    return hi, lo


def _dot(a, b):
    return jnp.dot(a, b, preferred_element_type=F32)


def _dot_nt(a, b):
    return lax.dot_general(a, b, (((1,), (1,)), ((), ())), preferred_element_type=F32)


def _mod_kernel(ct_ref, w_ref, b_ref, o_ref):
    ct = ct_ref[...]
    s = ct * jax.nn.sigmoid(ct)
    w = w_ref[0]
    rows = [jnp.sum(s[:, r:r + 1] * w, axis=0, keepdims=True) + b_ref[0] for r in range(3)]
    rows.append(jnp.zeros((5, w.shape[1]), F32))
    o_ref[0] = jnp.concatenate(rows, axis=0)


def _modulation(ct, mod_w, mod_b):
    tn = 1536
    return pl.pallas_call(
        _mod_kernel,
        grid=(DEPTH, 6 * D // tn),
        in_specs=[
            pl.BlockSpec((D, 8), lambda l, j: (0, 0)),
            pl.BlockSpec((1, D, tn), lambda l, j: (l, 0, j)),
            pl.BlockSpec((1, 1, tn), lambda l, j: (l, 0, j)),
        ],
        out_specs=pl.BlockSpec((1, 8, tn), lambda l, j: (l, 0, j)),
        out_shape=jax.ShapeDtypeStruct((DEPTH, 8, 6 * D), F32),
        compiler_params=_cparams(("arbitrary", "arbitrary")),
        name="adaln_modulation",
    )(ct, mod_w, mod_b.reshape(DEPTH, 1, 6 * D))


def _head_norm(x, g, bd):
    outs = []
    for c in range(x.shape[1] // 128):
        xc = x[:, c * 128:(c + 1) * 128]
        hi, lo = _split_bf16(xc * xc)
        ms = (_dot(hi, bd) + _dot(lo, bd)) * (1.0 / HEAD_DIM)
        outs.append(xc * lax.rsqrt(ms + EPS))
    return jnp.concatenate(outs, axis=1) * g


def _rope(x, cos, sin, low16):
    outs = []
    for c in range(x.shape[1] // 128):
        sl = slice(c * 128, (c + 1) * 128)
        xc = x[:, sl]
        partner = jnp.where(low16, pltpu.roll(xc, 112, 1), pltpu.roll(xc, 16, 1))
        outs.append(xc * cos[:, sl] + partner * sin[:, sl])
    return jnp.concatenate(outs, axis=1)


def _a1_kernel(x_ref, m_ref, g1_ref, w_ref, qg_ref, kg_ref, bd_ref, rc_ref, rs_ref,
               q_ref, kk_ref, vv_ref, knew_ref, vnew_ref, p_ref):
    h = _modnorm(x_ref[...], g1_ref[...], m_ref[0, 0:1, :], m_ref[0, 1:2, :])
    z = _dot(h.astype(BF16), w_ref[...])
    bd = bd_ref[...]
    q = _head_norm(z[:, :ATTN_W], qg_ref[...], bd)
    k = _head_norm(z[:, ATTN_W:ATTN_W + KV_W], kg_ref[...], bd)
    v = z[:, ATTN_W + KV_W:ATTN_W + 2 * KV_W]
    knew_ref[...] = k
    vnew_ref[...] = v
    p_ref[...] = z[:, ATTN_W + 2 * KV_W:]
    cos, sin = rc_ref[...], rs_ref[...]
    lane = lax.broadcasted_iota(jnp.int32, (TM, 128), 1)
    low16 = (lane & 31) < 16
    q = _rope(q, cos, sin, low16) * (HEAD_DIM ** -0.5)
    k = _rope(k, cos, sin, low16)
    q_ref[...] = q.astype(BF16)
    kk_ref[...] = jnp.concatenate([k, pltpu.roll(k, 64, 1)], axis=1).astype(BF16)
    vv_ref[...] = jnp.concatenate([v, pltpu.roll(v, 64, 1)], axis=1).astype(BF16)


def _rope_block(t):
    return jnp.where(t < N_PROMPT_SEQ, 0, 1 + (t - N_PROMPT_SEQ) % (DEC_SEQ // TM))


def _attn_in(x, mods, g1, w_in, qg, kg, bd, rope_c, rope_s):
    n_tiles = N_TOK // TM
    tile = lambda w: pl.BlockSpec((TM, w), lambda t: (t, 0))
    const = lambda shape: pl.BlockSpec(shape, lambda t: (0,) * len(shape))
    return pl.pallas_call(
        _a1_kernel,
        grid=(n_tiles,),
        in_specs=[
            tile(D),
            pl.BlockSpec((1, 6, D), lambda t: (_mod_row(t, TM), 0, 0)),
            const((1, D)),
            const((D, A_IN_W)),
            const((1, ATTN_W)),
            const((1, KV_W)),
            const((128, 128)),
            pl.BlockSpec((TM, ATTN_W), lambda t: (_rope_block(t), 0)),
            pl.BlockSpec((TM, ATTN_W), lambda t: (_rope_block(t), 0)),
        ],
        out_specs=[tile(ATTN_W), tile(2 * KV_W), tile(2 * KV_W), tile(KV_W), tile(KV_W), tile(POOL_W)],
        out_shape=[
            jax.ShapeDtypeStruct((N_TOK, ATTN_W), BF16),
            jax.ShapeDtypeStruct((N_TOK, 2 * KV_W), BF16),
            jax.ShapeDtypeStruct((N_TOK, 2 * KV_W), BF16),
            jax.ShapeDtypeStruct((N_TOK, KV_W), F32),
            jax.ShapeDtypeStruct((N_TOK, KV_W), F32),
            jax.ShapeDtypeStruct((N_TOK, POOL_W), F32),
        ],
        compiler_params=_cparams(("arbitrary",)),
        name="attn_in_proj",
    )(x, mods, g1, w_in, qg, kg, bd, rope_c, rope_s)


def _a2_kernel(x_ref, m_ref, q_ref, kk_ref, vv_ref, p_ref, pp_ref, pn_ref, band_ref, pw_ref, ps_ref,
               wo_ref, o_ref, *, tiles_per_seq):
    t = pl.program_id(0)
    q = q_ref[...]
    kk = kk_ref[0]
    vv = vv_ref[0]
    lane = lax.broadcasted_iota(jnp.int32, (TM, 128), 1)
    half_mask = (lane < HEAD_DIM, lane >= HEAD_DIM)
    zero_q = jnp.zeros((TM, 128), BF16)
    chunks = []
    for c in range(ATTN_W // 128):
        qc = q[:, c * 128:(c + 1) * 128]
        acc = None
        for half in range(2):
            kv = (2 * c + half) // (N_HEADS // N_KV)
            slab = slice(0, 128) if kv == half else slice(128, 256)
            s = _dot_nt(jnp.where(half_mask[half], qc, zero_q), kk[:, slab])
            e = jnp.exp(s - jnp.max(s, axis=-1, keepdims=True))
            l = jnp.sum(e, axis=-1, keepdims=True)
            o = _dot(e.astype(BF16), vv[:, slab]) / l
            o = jnp.where(half_mask[half], o, 0.0)
            acc = o if acc is None else acc + o
        chunks.append(acc)

    pos = t % tiles_per_seq
    has_prev = jnp.where(pos == 0, 0.0, 1.0)
    has_next = jnp.where(pos == tiles_per_seq - 1, 0.0, 1.0)
    p = p_ref[...]
    pext = jnp.concatenate([pp_ref[...] * has_prev, p, pn_ref[...] * has_next], axis=0)
    row = lax.broadcasted_iota(jnp.int32, (TM + 2 * HALO, 128), 0)
    valid = jnp.where(row < HALO, has_prev, jnp.where(row >= TM + HALO, has_next, 1.0)).astype(BF16)
    hi, lo = _split_bf16(pext)
    for g in range(len(POOL_WINDOWS)):
        sl = slice(g * 128, (g + 1) * 128)
        band = band_ref[g]
        tot = _dot(band, hi[:, sl]) + _dot(band, lo[:, sl])
        cnt = _dot(band, valid)
        dev = tot / cnt - p[:, sl]
        chunks.append(_dot(dev.astype(BF16), pw_ref[g]) * ps_ref[:, sl])
    mix = jnp.concatenate(chunks, axis=1).astype(BF16)
    o_ref[...] = x_ref[...] + m_ref[0, 2:3, :] * _dot(mix, wo_ref[...])


def _attn_out(x, mods, q, kk_seq, vv_seq, p, band, pool_w, pool_s, w_out, *, first_tile, n_tiles, tiles_per_seq):
    s_len = kk_seq.shape[1]
    off = first_tile
    n_halo_blocks = N_TOK // HALO
    per_tile = TM // HALO
    tile = lambda w: pl.BlockSpec((TM, w), lambda t: (t + off, 0))
    const = lambda shape: pl.BlockSpec(shape, lambda t: (0,) * len(shape))
    seq = pl.BlockSpec((1, s_len, 2 * KV_W), lambda t: (t // tiles_per_seq, 0, 0))
    return pl.pallas_call(
        functools.partial(_a2_kernel, tiles_per_seq=tiles_per_seq),
        grid=(n_tiles,),
        in_specs=[
            tile(D),
            pl.BlockSpec((1, 6, D), lambda t: (_mod_row(t + off, TM), 0, 0)),
            tile(ATTN_W),
            seq,
            seq,
            tile(POOL_W),
            pl.BlockSpec((HALO, POOL_W), lambda t: (jnp.maximum((t + off) * per_tile - 1, 0), 0)),
            pl.BlockSpec((HALO, POOL_W), lambda t: (jnp.minimum((t + off + 1) * per_tile, n_halo_blocks - 1), 0)),
            const((len(POOL_WINDOWS), TM, TM + 2 * HALO)),
            const((len(POOL_WINDOWS), 128, 128)),
            const((1, POOL_W)),
            const((D, D)),
        ],
        out_specs=tile(D),
        out_shape=jax.ShapeDtypeStruct((N_TOK, D), F32),
        input_output_aliases={0: 0},
        compiler_params=_cparams(("arbitrary",)),
        name="attn_pool_out_%d" % s_len,
    )(x, mods, q, kk_seq, vv_seq, p, p, p, band, pool_w, pool_s, w_out)


def _conv_kernel(x_ref, xp_ref, xn_ref, m_ref, g1_ref, wi_ref, cw_ref, wo_ref, o_ref):
    t = pl.program_id(0)
    is_prompt = t < N_PROMPT_SEQ
    pos = t & (DEC_SEQ // TM - 1)
    has_prev = jnp.where(is_prompt | (pos == 0), 0.0, 1.0)
    has_next = jnp.where(is_prompt | (pos == DEC_SEQ // TM - 1), 0.0, 1.0)
    x = x_ref[...]
    xext = jnp.concatenate([xp_ref[...], x, xn_ref[...]], axis=0)
    h = _modnorm(xext, g1_ref[...], m_ref[0, 0:1, :], m_ref[0, 1:2, :])
    z = _dot(h.astype(BF16), wi_ref[...])
    gb = z[HALO:HALO + TM, :D]
    u = z[:, D:2 * D] * z[:, 2 * D:]
    row = lax.broadcasted_iota(jnp.int32, (TM + 2 * HALO, 1), 0)
    u = u * jnp.where(row < HALO, has_prev, jnp.where(row >= TM + HALO, has_next, 1.0))
    rows = TM + 2 * HALO
    cw = cw_ref[...]
    y = (pltpu.roll(u, 1, 0)[HALO:HALO + TM] * cw[0:1, :]
         + u[HALO:HALO + TM] * cw[1:2, :]
         + pltpu.roll(u, rows - 1, 0)[HALO:HALO + TM] * cw[2:3, :])
    out = _dot((gb * y).astype(BF16), wo_ref[...])
    o_ref[...] = x + m_ref[0, 2:3, :] * out


def _conv_layer(x, mods, g1, w_in, conv_w, w_out):
    n_tiles = N_TOK // TM
    n_halo_blocks = N_TOK // HALO
    per_tile = TM // HALO
    const = lambda shape: pl.BlockSpec(shape, lambda t: (0,) * len(shape))
    return pl.pallas_call(
        _conv_kernel,
        grid=(n_tiles,),
        in_specs=[
            pl.BlockSpec((TM, D), lambda t: (t, 0)),
            pl.BlockSpec((HALO, D), lambda t: (jnp.maximum(t * per_tile - 1, 0), 0)),
            pl.BlockSpec((HALO, D), lambda t: (jnp.minimum((t + 1) * per_tile, n_halo_blocks - 1), 0)),
            pl.BlockSpec((1, 6, D), lambda t: (_mod_row(t, TM), 0, 0)),
            const((1, D)),
            const((D, 3 * D)),
            const((8, D)),
            const((D, D)),
        ],
        out_specs=pl.BlockSpec((TM, D), lambda t: (t, 0)),
        out_shape=jax.ShapeDtypeStruct((N_TOK, D), F32),
        compiler_params=_cparams(("arbitrary",)),
        name="conv_mixer",
    )(x, x, x, mods, g1, w_in, conv_w, w_out)


def _top16_rows(s, n_rows):
    iota = lax.broadcasted_iota(jnp.int32, s.shape, 0)
    vals, idxs = [], []
    for _ in range(PEER_TOPK):
        m = jnp.max(s, axis=0, keepdims=True)
        ix = jnp.min(jnp.where(s == m, iota, n_rows), axis=0, keepdims=True)
        s = jnp.where(iota == ix, -jnp.inf, s)
        vals.append(m)
        idxs.append(ix)
    return jnp.concatenate(vals, axis=0), jnp.concatenate(idxs, axis=0)


def _route_kernel(x_ref, m_ref, g2_ref, wqt_ref, sk_ref, hb_ref, gate_ref,
                  qt_scr, sv_scr, si_scr, i1_scr, i2_scr, gt_scr, gscr):
    h = _modnorm(x_ref[...], g2_ref[...], m_ref[0, 3:4, :], m_ref[0, 4:5, :])
    hb = h.astype(BF16)
    hb_ref[...] = hb
    qt_scr[...] = _dot_nt(wqt_ref[...], hb)

    def stage1(hp, carry):
        qb = qt_scr[pl.ds(pl.multiple_of(hp * PEER_DK, PEER_DK), PEER_DK), :]
        sk = sk_ref[hp % 2]
        q_hi, q_lo = _split_bf16(qb)
        k_hi, k_lo = _split_bf16(sk)
        s = _dot(k_hi, q_hi) + (_dot(k_hi, q_lo) + _dot(k_lo, q_hi))
        sv, si = _top16_rows(s, PEER_NKEYS)
        sv_scr[hp] = sv
        si_scr[hp] = si.astype(F32)
        return carry

    lax.fori_loop(0, 2 * PEER_HEADS, stage1, 0)

    def stage2(hd, carry):
        sv1, sv2 = sv_scr[2 * hd], sv_scr[2 * hd + 1]
        si1, si2 = si_scr[2 * hd], si_scr[2 * hd + 1]
        cand = jnp.concatenate([sv1[a:a + 1, :] + sv2 for a in range(PEER_TOPK)], axis=0)
        fv, fi = _top16_rows(cand, PEER_TOPK * PEER_TOPK)
        fa, fb = fi >> 4, fi & (PEER_TOPK - 1)
        i1 = jnp.zeros_like(fv)
        i2 = jnp.zeros_like(fv)
        for a in range(PEER_TOPK):
            i1 = jnp.where(fa == a, si1[a:a + 1, :], i1)
            i2 = jnp.where(fb == a, si2[a:a + 1, :], i2)
        e = jnp.exp(fv - fv[0:1, :])
        g = e / jnp.sum(e, axis=0, keepdims=True)
        rows = pl.ds(pl.multiple_of(hd * PEER_TOPK, PEER_TOPK), PEER_TOPK)
        i1_scr[rows, :] = i1
        i2_scr[rows, :] = i2
        gt_scr[rows, :] = g
        return carry

    lax.fori_loop(0, PEER_HEADS, stage2, 0)

    i1_scr[...] = i1_scr[...].T
    i2_scr[...] = i2_scr[...].T
    gt_scr[...] = gt_scr[...].T
    key = lax.broadcasted_iota(jnp.int32, (PEER_NKEYS, 128), 0).astype(F32)

    def per_token(n, carry):
        row = pl.ds(n, 1)
        left = jnp.where(key == i1_scr[row, :], gt_scr[row, :], 0.0).astype(BF16)
        right = jnp.where(key == i2_scr[row, :], 1.0, 0.0).astype(BF16)
        gscr[pl.ds(n, PEER_NKEYS, stride=G_PITCH), :] = _dot_nt(left, right)
        return carry

    lax.fori_loop(0, TR, per_token, 0, unroll=4)

    def copy_out(i, carry):
        gate_ref[i] = gscr[pl.ds(pl.multiple_of(i * G_PITCH, 8), TR), :].astype(BF16)
        return carry

    lax.fori_loop(0, PEER_NKEYS, copy_out, 0, unroll=4)


def _peer_route(x, mods, g2, wqt, subkeys):
    n_tiles = N_TOK // TR
    const = lambda shape: pl.BlockSpec(shape, lambda t: (0,) * len(shape))
    return pl.pallas_call(
        _route_kernel,
        grid=(n_tiles,),
        in_specs=[
            pl.BlockSpec((TR, D), lambda t: (t, 0)),
            pl.BlockSpec((1, 6, D), lambda t: (_mod_row(t, TR), 0, 0)),
            const((1, D)),
            const((PEER_HEADS * 2 * PEER_DK, D)),
            const((2, PEER_NKEYS, PEER_DK)),
        ],
        out_specs=[
            pl.BlockSpec((TR, D), lambda t: (t, 0)),
            pl.BlockSpec((PEER_NKEYS, TR, PEER_NKEYS), lambda t: (0, t, 0)),
        ],
        out_shape=[
            jax.ShapeDtypeStruct((N_TOK, D), BF16),
            jax.ShapeDtypeStruct((PEER_NKEYS, N_TOK, PEER_NKEYS), BF16),
        ],
        scratch_shapes=[
            pltpu.VMEM((PEER_HEADS * 2 * PEER_DK, TR), F32),
            pltpu.VMEM((2 * PEER_HEADS, PEER_TOPK, TR), F32),
            pltpu.VMEM((2 * PEER_HEADS, PEER_TOPK, TR), F32),
            pltpu.VMEM((PEER_HEADS * PEER_TOPK, TR), F32),
            pltpu.VMEM((PEER_HEADS * PEER_TOPK, TR), F32),
            pltpu.VMEM((PEER_HEADS * PEER_TOPK, TR), F32),
            pltpu.VMEM((PEER_NKEYS * G_PITCH, PEER_NKEYS), F32),
        ],
        compiler_params=_cparams(("arbitrary",)),
        name="peer_route",
    )(x, mods, g2, wqt, subkeys)


def _expert_kernel(x_ref, m_ref, hb_ref, gate_ref, u_ref, v_ref, o_ref, acc_ref):
    c = pl.program_id(1)

    @pl.when(c == 0)
    def _():
        acc_ref[...] = jnp.zeros_like(acc_ref)

    pre = _dot_nt(hb_ref[...], u_ref[...].astype(BF16))
    act = 0.5 * pre * (1.0 + lax.erf(pre * (2.0 ** -0.5)))
    w = jnp.concatenate(
        [act[:, a * 128:(a + 1) * 128] * gate_ref[a].astype(F32) for a in range(EC // PEER_NKEYS)], axis=1)
    acc_ref[...] += _dot(w.astype(BF16), v_ref[...].astype(BF16))

    @pl.when(c == pl.num_programs(1) - 1)
    def _():
        o_ref[...] = x_ref[...] + m_ref[0, 5:6, :] * acc_ref[...]


def _peer_experts(x, mods, hb, gate, u, v):
    n_tiles = N_TOK // TE
    n_chunks = PEER_EXPERTS // EC
    return pl.pallas_call(
        _expert_kernel,
        grid=(n_tiles, n_chunks),
        in_specs=[
            pl.BlockSpec((TE, D), lambda t, c: (t, 0)),
            pl.BlockSpec((1, 6, D), lambda t, c: (_mod_row(t, TE), 0, 0)),
            pl.BlockSpec((TE, D), lambda t, c: (t, 0)),
            pl.BlockSpec((EC // PEER_NKEYS, TE, PEER_NKEYS), lambda t, c: (c, t, 0)),
            pl.BlockSpec((EC, D), lambda t, c: (c, 0)),
            pl.BlockSpec((EC, D), lambda t, c: (c, 0)),
        ],
        out_specs=pl.BlockSpec((TE, D), lambda t, c: (t, 0)),
        out_shape=jax.ShapeDtypeStruct((N_TOK, D), F32),
        scratch_shapes=[pltpu.VMEM((TE, D), F32)],
        compiler_params=_cparams(("arbitrary", "arbitrary")),
        name="peer_experts",
    )(x, mods, hb, gate, u, v)


def _final_kernel(x_ref, g_ref, o_ref):
    x = x_ref[...]
    ms = jnp.mean(x * x, axis=-1, keepdims=True)
    o_ref[...] = x * lax.rsqrt(ms + EPS) * g_ref[...]


def _final_norm(x, g):
    return pl.pallas_call(
        _final_kernel,
        grid=(N_TOK // TM,),
        in_specs=[pl.BlockSpec((TM, D), lambda t: (t, 0)), pl.BlockSpec((1, D), lambda t: (0, 0))],
        out_specs=pl.BlockSpec((TM, D), lambda t: (t, 0)),
        out_shape=jax.ShapeDtypeStruct((N_TOK, D), F32),
        compiler_params=_cparams(("arbitrary",)),
        name="final_norm",
    )(x, g)


def _rope_tables():
    t = jnp.arange(DEC_SEQ)
    row, col = t // GRID_W, t % GRID_W
    nf = HEAD_DIM // 4
    inv = ROPE_THETA ** (-jnp.arange(nf, dtype=F32) / nf)
    ang_r = row.astype(F32)[:, None] * inv[None, :]
    ang_c = col.astype(F32)[:, None] * inv[None, :]
    cos = jnp.concatenate([jnp.cos(ang_r)] * 2 + [jnp.cos(ang_c)] * 2, axis=1)
    sin = jnp.concatenate([-jnp.sin(ang_r), jnp.sin(ang_r), -jnp.sin(ang_c), jnp.sin(ang_c)], axis=1)
    cos = jnp.concatenate([jnp.ones((TM, HEAD_DIM), F32), cos], axis=0)
    sin = jnp.concatenate([jnp.zeros((TM, HEAD_DIM), F32), sin], axis=0)
    return jnp.tile(cos, (1, N_HEADS)), jnp.tile(sin, (1, N_HEADS))


def _band_matrices():
    t = np.arange(TM)[:, None]
    j = np.arange(TM + 2 * HALO)[None, :] - HALO
    bands = [((j >= t - w // 2) & (j < t + w - w // 2)).astype(np.float32) for w in POOL_WINDOWS]
    return jnp.asarray(np.stack(bands), dtype=BF16)


def _head_block_diag():
    i = np.arange(128)
    return jnp.asarray((i[:, None] // HEAD_DIM == i[None, :] // HEAD_DIM).astype(np.float32), dtype=BF16)


def kernel(x_prompt, x_sample, cache_k, cache_v, c, c_ctx, mod_w, mod_b, norm1_g, norm2_g, a_in_w, a_out_w,
           q_norm_g, k_norm_g, pool_w, pool_scale, c_in_w, c_conv_w, c_out_w, peer_wq, peer_subkeys, peer_u,
           peer_v, final_g):
    x = jnp.concatenate([x_prompt.reshape(NP_TOK, D), x_sample.reshape(NS_TOK, D)], axis=0)
    ct = jnp.zeros((D, 8), F32).at[:, 0].set(c_ctx).at[:, 1:3].set(c.T)
    mods = _modulation(ct, mod_w, mod_b).reshape(DEPTH, 8, 6, D)
    rope_c, rope_s = _rope_tables()
    band = _band_matrices()
    bd = _head_block_diag()
    wqt = jnp.swapaxes(peer_wq, 1, 2).astype(BF16)
    new_k, new_v = [], []
    for l in range(DEPTH):
        g1 = norm1_g[l].reshape(1, D)
        if l % 2 == 0:
            i = l // 2
            q, kk, vv, knew, vnew, p = _attn_in(
                x, mods[l], g1, a_in_w[i].astype(BF16), jnp.tile(q_norm_g[i], N_HEADS).reshape(1, ATTN_W),
                jnp.tile(k_norm_g[i], N_KV).reshape(1, KV_W), bd, rope_c, rope_s)
            new_k.append(knew[:NP_TOK].reshape(N_PROMPT_SEQ, SEQ, N_KV, HEAD_DIM))
            new_v.append(vnew[:NP_TOK].reshape(N_PROMPT_SEQ, SEQ, N_KV, HEAD_DIM))
            ck = cache_k[:, i]
            cv = cache_v[:, i]
            ctx_kk = jnp.concatenate([ck, ck[:, :, ::-1]], axis=2).reshape(N_SAMPLE_SEQ, PAST, 2 * KV_W).astype(BF16)
            ctx_vv = jnp.concatenate([cv, cv[:, :, ::-1]], axis=2).reshape(N_SAMPLE_SEQ, PAST, 2 * KV_W).astype(BF16)
            kk_p = kk[:NP_TOK].reshape(N_PROMPT_SEQ, SEQ, 2 * KV_W)
            vv_p = vv[:NP_TOK].reshape(N_PROMPT_SEQ, SEQ, 2 * KV_W)
            kk_s = jnp.concatenate([kk[NP_TOK:].reshape(N_SAMPLE_SEQ, DEC_SEQ, 2 * KV_W), ctx_kk], axis=1)
            vv_s = jnp.concatenate([vv[NP_TOK:].reshape(N_SAMPLE_SEQ, DEC_SEQ, 2 * KV_W), ctx_vv], axis=1)
            common = (band, pool_w[i].astype(BF16), pool_scale[i].reshape(1, POOL_W), a_out_w[i].astype(BF16))
            x = _attn_out(x, mods[l], q, kk_p, vv_p, p, *common,
                          first_tile=0, n_tiles=N_PROMPT_SEQ, tiles_per_seq=SEQ // TM)
            x = _attn_out(x, mods[l], q, kk_s, vv_s, p, *common,
                          first_tile=N_PROMPT_SEQ, n_tiles=NS_TOK // TM, tiles_per_seq=DEC_SEQ // TM)
        else:
            j = l // 2
            cw = jnp.zeros((8, D), F32).at[:3].set(c_conv_w[j])
            x = _conv_layer(x, mods[l], g1, c_in_w[j].astype(BF16), cw, c_out_w[j].astype(BF16))
        hb, gate = _peer_route(x, mods[l], norm2_g[l].reshape(1, D), wqt[l], peer_subkeys[l])
        x = _peer_experts(x, mods[l], hb, gate, peer_u[l], peer_v[l])
    y = _final_norm(x, final_g.reshape(1, D))
    y_prompt = y[:NP_TOK].reshape(N_PROMPT_SEQ, SEQ, D)
    y_sample = y[NP_TOK:].reshape(N_SAMPLE_SEQ, DEC_SEQ, D)
    return (y_prompt, y_sample, jnp.stack(new_k, axis=1), jnp.stack(new_v, axis=1))
```

```python
import functools

import numpy as np
import jax
import jax.numpy as jnp
from jax import lax
from jax.experimental import pallas as pl
from jax.experimental.pallas import tpu as pltpu

F32 = jnp.float32
BF16 = jnp.bfloat16

D = 1024
N_PROMPT_SEQ, SEQ = 32, 256
N_SAMPLE_SEQ, DEC_SEQ = 2, 2048
PAST = 256
NP_TOK = N_PROMPT_SEQ * SEQ
NS_TOK = N_SAMPLE_SEQ * DEC_SEQ
N_TOK = NP_TOK + NS_TOK
DEPTH = 4
GRID_W = 64
N_HEADS, N_KV, HEAD_DIM = 8, 2, 64
ATTN_W, KV_W, POOL_W = 512, 128, 512
POOL_WINDOWS = (2, 4, 8, 16)
A_IN_W = ATTN_W + 2 * KV_W + POOL_W
PEER_HEADS, PEER_NKEYS, PEER_TOPK, PEER_DK = 8, 128, 16, 128
PEER_EXPERTS = PEER_NKEYS * PEER_NKEYS
ROPE_THETA = 10000.0
EPS = 1e-6

TM = 256
HALO = 8
TR = 256
G_PITCH = TR + 8
TE = 1024
EC = 1024
EC_SUB = 512
VMEM_LIMIT = 56 * 1024 * 1024


def _cparams(sem):
    return pltpu.CompilerParams(dimension_semantics=sem, vmem_limit_bytes=VMEM_LIMIT)


def _mod_row(tile, tile_tokens):
    start = tile * tile_tokens
    return (start >= NP_TOK).astype(jnp.int32) + (start >= NP_TOK + DEC_SEQ).astype(jnp.int32)


def _modnorm(x, g, shift, scale):
    ms = jnp.mean(x * x, axis=-1, keepdims=True)
    y = x * lax.rsqrt(ms + EPS) * g
    return y * (1.0 + scale) + shift


def _split_bf16(a):
    hi = a.astype(BF16)
    lo = (a - hi.astype(F32)).astype(BF16)
    return hi, lo


def _dot(a, b):
    return jnp.dot(a, b, preferred_element_type=F32)


def _dot_nt(a, b):
    return lax.dot_general(a, b, (((1,), (1,)), ((), ())), preferred_element_type=F32)


def _mod_kernel(ct_ref, w_ref, b_ref, o_ref):
    ct = ct_ref[...]
    s = ct * jax.nn.sigmoid(ct)
    w = w_ref[0]
    rows = [jnp.sum(s[:, r:r + 1] * w, axis=0, keepdims=True) + b_ref[0] for r in range(3)]
    rows.append(jnp.zeros((5, w.shape[1]), F32))
    o_ref[0] = jnp.concatenate(rows, axis=0)


def _modulation(ct, mod_w, mod_b):
    tn = 1536
    return pl.pallas_call(
        _mod_kernel,
        grid=(DEPTH, 6 * D // tn),
        in_specs=[
            pl.BlockSpec((D, 8), lambda l, j: (0, 0)),
            pl.BlockSpec((1, D, tn), lambda l, j: (l, 0, j)),
            pl.BlockSpec((1, 1, tn), lambda l, j: (l, 0, j)),
        ],
        out_specs=pl.BlockSpec((1, 8, tn), lambda l, j: (l, 0, j)),
        out_shape=jax.ShapeDtypeStruct((DEPTH, 8, 6 * D), F32),
        compiler_params=_cparams(("arbitrary", "arbitrary")),
        name="adaln_modulation",
    )(ct, mod_w, mod_b.reshape(DEPTH, 1, 6 * D))


def _head_norm(x, g, bd):
    outs = []
    for c in range(x.shape[1] // 128):
        xc = x[:, c * 128:(c + 1) * 128]
        hi, lo = _split_bf16(xc * xc)
        ms = (_dot(hi, bd) + _dot(lo, bd)) * (1.0 / HEAD_DIM)
        outs.append(xc * lax.rsqrt(ms + EPS))
    return jnp.concatenate(outs, axis=1) * g


def _rope(x, cos, sin, low16):
    outs = []
    for c in range(x.shape[1] // 128):
        sl = slice(c * 128, (c + 1) * 128)
        xc = x[:, sl]
        partner = jnp.where(low16, pltpu.roll(xc, 112, 1), pltpu.roll(xc, 16, 1))
        outs.append(xc * cos[:, sl] + partner * sin[:, sl])
    return jnp.concatenate(outs, axis=1)


def _a1_kernel(x_ref, m_ref, g1_ref, w_ref, qg_ref, kg_ref, bd_ref, rc_ref, rs_ref,
               q_ref, kk_ref, vv_ref, knew_ref, vnew_ref, p_ref):
    h = _modnorm(x_ref[...], g1_ref[...], m_ref[0, 0:1, :], m_ref[0, 1:2, :])
    z = _dot(h.astype(BF16), w_ref[...])
    bd = bd_ref[...]
    q = _head_norm(z[:, :ATTN_W], qg_ref[...], bd)
    k = _head_norm(z[:, ATTN_W:ATTN_W + KV_W], kg_ref[...], bd)
    v = z[:, ATTN_W + KV_W:ATTN_W + 2 * KV_W]
    knew_ref[...] = k
    vnew_ref[...] = v
    p_ref[...] = z[:, ATTN_W + 2 * KV_W:]
    cos, sin = rc_ref[...], rs_ref[...]
    lane = lax.broadcasted_iota(jnp.int32, (TM, 128), 1)
    low16 = (lane & 31) < 16
    q = _rope(q, cos, sin, low16) * (HEAD_DIM ** -0.5)
    k = _rope(k, cos, sin, low16)
    q_ref[...] = q.astype(BF16)
    kk_ref[...] = jnp.concatenate([k, pltpu.roll(k, 64, 1)], axis=1).astype(BF16)
    vv_ref[...] = jnp.concatenate([v, pltpu.roll(v, 64, 1)], axis=1).astype(BF16)


def _rope_block(t):
    return jnp.where(t < N_PROMPT_SEQ, 0, 1 + (t - N_PROMPT_SEQ) % (DEC_SEQ // TM))


def _attn_in(x, mods, g1, w_in, qg, kg, bd, rope_c, rope_s):
    n_tiles = N_TOK // TM
    tile = lambda w: pl.BlockSpec((TM, w), lambda t: (t, 0))
    const = lambda shape: pl.BlockSpec(shape, lambda t: (0,) * len(shape))
    return pl.pallas_call(
        _a1_kernel,
        grid=(n_tiles,),
        in_specs=[
            tile(D),
            pl.BlockSpec((1, 6, D), lambda t: (_mod_row(t, TM), 0, 0)),
            const((1, D)),
            const((D, A_IN_W)),
            const((1, ATTN_W)),
            const((1, KV_W)),
            const((128, 128)),
            pl.BlockSpec((TM, ATTN_W), lambda t: (_rope_block(t), 0)),
            pl.BlockSpec((TM, ATTN_W), lambda t: (_rope_block(t), 0)),
        ],
        out_specs=[tile(ATTN_W), tile(2 * KV_W), tile(2 * KV_W), tile(KV_W), tile(KV_W), tile(POOL_W)],
        out_shape=[
            jax.ShapeDtypeStruct((N_TOK, ATTN_W), BF16),
            jax.ShapeDtypeStruct((N_TOK, 2 * KV_W), BF16),
            jax.ShapeDtypeStruct((N_TOK, 2 * KV_W), BF16),
            jax.ShapeDtypeStruct((N_TOK, KV_W), F32),
            jax.ShapeDtypeStruct((N_TOK, KV_W), F32),
            jax.ShapeDtypeStruct((N_TOK, POOL_W), F32),
        ],
        compiler_params=_cparams(("arbitrary",)),
        name="attn_in_proj",
    )(x, mods, g1, w_in, qg, kg, bd, rope_c, rope_s)


def _a2_kernel(x_ref, m_ref, q_ref, kk_ref, vv_ref, p_ref, pp_ref, pn_ref, band_ref, pw_ref, ps_ref,
               wo_ref, o_ref, *, tiles_per_seq):
    t = pl.program_id(0)
    q = q_ref[...]
    kk = kk_ref[0]
    vv = vv_ref[0]
    lane = lax.broadcasted_iota(jnp.int32, (TM, 128), 1)
    half_mask = (lane < HEAD_DIM, lane >= HEAD_DIM)
    zero_q = jnp.zeros((TM, 128), BF16)
    chunks = []
    for c in range(ATTN_W // 128):
        qc = q[:, c * 128:(c + 1) * 128]
        acc = None
        for half in range(2):
            kv = (2 * c + half) // (N_HEADS // N_KV)
            slab = slice(0, 128) if kv == half else slice(128, 256)
            s = _dot_nt(jnp.where(half_mask[half], qc, zero_q), kk[:, slab])
            e = jnp.exp(s - jnp.max(s, axis=-1, keepdims=True))
            l = jnp.sum(e, axis=-1, keepdims=True)
            o = _dot(e.astype(BF16), vv[:, slab]) / l
            o = jnp.where(half_mask[half], o, 0.0)
            acc = o if acc is None else acc + o
        chunks.append(acc)

    pos = t % tiles_per_seq
    has_prev = jnp.where(pos == 0, 0.0, 1.0)
    has_next = jnp.where(pos == tiles_per_seq - 1, 0.0, 1.0)
    p = p_ref[...]
    pext = jnp.concatenate([pp_ref[...] * has_prev, p, pn_ref[...] * has_next], axis=0)
    row = lax.broadcasted_iota(jnp.int32, (TM + 2 * HALO, 128), 0)
    valid = jnp.where(row < HALO, has_prev, jnp.where(row >= TM + HALO, has_next, 1.0)).astype(BF16)
    hi, lo = _split_bf16(pext)
    for g in range(len(POOL_WINDOWS)):
        sl = slice(g * 128, (g + 1) * 128)
        band = band_ref[g]
        tot = _dot(band, hi[:, sl]) + _dot(band, lo[:, sl])
        cnt = _dot(band, valid)
        dev = tot / cnt - p[:, sl]
        chunks.append(_dot(dev.astype(BF16), pw_ref[g]) * ps_ref[:, sl])
    mix = jnp.concatenate(chunks, axis=1).astype(BF16)
    o_ref[...] = x_ref[...] + m_ref[0, 2:3, :] * _dot(mix, wo_ref[...])


def _attn_out(x, mods, q, kk_seq, vv_seq, p, band, pool_w, pool_s, w_out, *, first_tile, n_tiles, tiles_per_seq):
    s_len = kk_seq.shape[1]
    off = first_tile
    n_halo_blocks = N_TOK // HALO
    per_tile = TM // HALO
    tile = lambda w: pl.BlockSpec((TM, w), lambda t: (t + off, 0))
    const = lambda shape: pl.BlockSpec(shape, lambda t: (0,) * len(shape))
    seq = pl.BlockSpec((1, s_len, 2 * KV_W), lambda t: (t // tiles_per_seq, 0, 0))
    return pl.pallas_call(
        functools.partial(_a2_kernel, tiles_per_seq=tiles_per_seq),
        grid=(n_tiles,),
        in_specs=[
            tile(D),
            pl.BlockSpec((1, 6, D), lambda t: (_mod_row(t + off, TM), 0, 0)),
            tile(ATTN_W),
            seq,
            seq,
            tile(POOL_W),
            pl.BlockSpec((HALO, POOL_W), lambda t: (jnp.maximum((t + off) * per_tile - 1, 0), 0)),
            pl.BlockSpec((HALO, POOL_W), lambda t: (jnp.minimum((t + off + 1) * per_tile, n_halo_blocks - 1), 0)),
            const((len(POOL_WINDOWS), TM, TM + 2 * HALO)),
            const((len(POOL_WINDOWS), 128, 128)),
            const((1, POOL_W)),
            const((D, D)),
        ],
        out_specs=tile(D),
        out_shape=jax.ShapeDtypeStruct((N_TOK, D), F32),
        input_output_aliases={0: 0},
        compiler_params=_cparams(("arbitrary",)),
        name="attn_pool_out_%d" % s_len,
    )(x, mods, q, kk_seq, vv_seq, p, p, p, band, pool_w, pool_s, w_out)


def _conv_kernel(x_ref, xp_ref, xn_ref, m_ref, g1_ref, wi_ref, cw_ref, wo_ref, o_ref):
    t = pl.program_id(0)
    is_prompt = t < N_PROMPT_SEQ
    pos = t & (DEC_SEQ // TM - 1)
    has_prev = jnp.where(is_prompt | (pos == 0), 0.0, 1.0)
    has_next = jnp.where(is_prompt | (pos == DEC_SEQ // TM - 1), 0.0, 1.0)
    x = x_ref[...]
    xext = jnp.concatenate([xp_ref[...], x, xn_ref[...]], axis=0)
    h = _modnorm(xext, g1_ref[...], m_ref[0, 0:1, :], m_ref[0, 1:2, :])
    z = _dot(h.astype(BF16), wi_ref[...])
    gb = z[HALO:HALO + TM, :D]
    u = z[:, D:2 * D] * z[:, 2 * D:]
    row = lax.broadcasted_iota(jnp.int32, (TM + 2 * HALO, 1), 0)
    u = u * jnp.where(row < HALO, has_prev, jnp.where(row >= TM + HALO, has_next, 1.0))
    rows = TM + 2 * HALO
    cw = cw_ref[...]
    y = (pltpu.roll(u, 1, 0)[HALO:HALO + TM] * cw[0:1, :]
         + u[HALO:HALO + TM] * cw[1:2, :]
         + pltpu.roll(u, rows - 1, 0)[HALO:HALO + TM] * cw[2:3, :])
    out = _dot((gb * y).astype(BF16), wo_ref[...])
    o_ref[...] = x + m_ref[0, 2:3, :] * out


def _conv_layer(x, mods, g1, w_in, conv_w, w_out):
    n_tiles = N_TOK // TM
    n_halo_blocks = N_TOK // HALO
    per_tile = TM // HALO
    const = lambda shape: pl.BlockSpec(shape, lambda t: (0,) * len(shape))
    return pl.pallas_call(
        _conv_kernel,
        grid=(n_tiles,),
        in_specs=[
            pl.BlockSpec((TM, D), lambda t: (t, 0)),
            pl.BlockSpec((HALO, D), lambda t: (jnp.maximum(t * per_tile - 1, 0), 0)),
            pl.BlockSpec((HALO, D), lambda t: (jnp.minimum((t + 1) * per_tile, n_halo_blocks - 1), 0)),
            pl.BlockSpec((1, 6, D), lambda t: (_mod_row(t, TM), 0, 0)),
            const((1, D)),
            const((D, 3 * D)),
            const((8, D)),
            const((D, D)),
        ],
        out_specs=pl.BlockSpec((TM, D), lambda t: (t, 0)),
        out_shape=jax.ShapeDtypeStruct((N_TOK, D), F32),
        compiler_params=_cparams(("arbitrary",)),
        name="conv_mixer",
    )(x, x, x, mods, g1, w_in, conv_w, w_out)


NEG_INF = float("-inf")
SUB = 8
NO_ROW = 1 << 20


def _col_max(vals):
    while len(vals) > 1:
        vals = [jnp.maximum(vals[a], vals[a + 1]) for a in range(0, len(vals) - 1, 2)] + vals[len(vals) & ~1:]
    return jnp.max(vals[0], axis=0, keepdims=True)


def _col_min(vals):
    while len(vals) > 1:
        vals = [jnp.minimum(vals[a], vals[a + 1]) for a in range(0, len(vals) - 1, 2)] + vals[len(vals) & ~1:]
    return jnp.min(vals[0], axis=0, keepdims=True)


def _col_argmax(vals, rows):
    while len(vals) > 1:
        nv, nr = [], []
        for a in range(0, len(vals) - 1, 2):
            nv.append(jnp.maximum(vals[a], vals[a + 1]))
            nr.append(jnp.where(vals[a] >= vals[a + 1], rows[a], rows[a + 1]))
        vals, rows = nv + vals[len(vals) & ~1:], nr + rows[len(rows) & ~1:]
    m = jnp.max(vals[0], axis=0, keepdims=True)
    return m, jnp.min(jnp.where(vals[0] == m, rows[0], NO_ROW), axis=0, keepdims=True)


def _top16_keys(s, key_rows):
    blocks = [s[SUB * v:SUB * (v + 1), :] for v in range(PEER_NKEYS // SUB)]
    vals, idxs = [], []
    for _ in range(PEER_TOPK):
        m, ix = _col_argmax(blocks, key_rows)
        blocks = [jnp.where(r == ix, NEG_INF, b) for b, r in zip(blocks, key_rows)]
        vals.append(m)
        idxs.append(ix)
    return jnp.concatenate(vals, axis=0), jnp.concatenate(idxs, axis=0)


def _pair_candidates(sv1, sv2, sub):
    a_lo, a_hi, b_lo, b_hi = sv1[0:SUB], sv1[SUB:], sv2[0:SUB], sv2[SUB:]
    row = lambda x, r: x[r:r + 1, :]
    keep = lambda cond, v: jnp.where(cond, v, NEG_INF)
    return [
        (row(sv1, 0) + b_lo, sub),
        (row(sv1, 0) + b_hi, sub + SUB),
        (keep(sub >= 1, a_lo + row(sv2, 0)), sub * PEER_TOPK),
        (a_hi + row(sv2, 0), (sub + SUB) * PEER_TOPK),
        (keep(sub >= 1, row(sv1, 1) + b_lo), sub + PEER_TOPK),
        (keep(sub >= 2, a_lo + row(sv2, 1)), sub * PEER_TOPK + 1),
        (keep((sub >= 2) & (sub <= 4), row(sv1, 2) + b_lo), sub + 2 * PEER_TOPK),
        (keep((sub >= 3) & (sub <= 4), a_lo + row(sv2, 2)), sub * PEER_TOPK + 2),
        (keep(sub == 3, row(sv1, 3) + b_lo), sub + 3 * PEER_TOPK),
    ]


def _route_kernel(x_ref, m_ref, g2_ref, wq_ref, sk_ref, hb_ref, gate_ref,
                  q_scr, i1t_scr, i2t_scr, gtt_scr, i1_scr, i2_scr, gt_scr, gscr):
    h = _modnorm(x_ref[...], g2_ref[...], m_ref[0, 3:4, :], m_ref[0, 4:5, :])
    hb = h.astype(BF16)
    hb_ref[...] = hb
    q = _dot(hb, wq_ref[...])
    for hp in range(2 * PEER_HEADS):
        q_scr[hp] = q[:, hp * PEER_DK:(hp + 1) * PEER_DK]
    sub = lax.broadcasted_iota(jnp.int32, (SUB, TR), 0)
    key_rows = [sub + SUB * v for v in range(PEER_NKEYS // SUB)]
    sk_split = [_split_bf16(sk_ref[p]) for p in range(2)]

    def per_head(hd, carry):
        top = []
        for p in range(2):
            q_hi, q_lo = _split_bf16(q_scr[2 * hd + p])
            k_hi, k_lo = sk_split[p]
            s = _dot_nt(k_hi, q_hi) + (_dot_nt(k_hi, q_lo) + _dot_nt(k_lo, q_hi))
            top.append(_top16_keys(s, key_rows))
        (sv1, si1), (sv2, si2) = top
        cands = _pair_candidates(sv1, sv2, sub)
        vals, flats = [c[0] for c in cands], [c[1] for c in cands]
        fvs, ffs = [], []
        for _ in range(PEER_TOPK):
            m = _col_max(vals)
            f = _col_min([jnp.where(v == m, fl, NO_ROW) for v, fl in zip(vals, flats)])
            vals = [jnp.where(fl == f, NEG_INF, v) for v, fl in zip(vals, flats)]
            fvs.append(m)
            ffs.append(f)
        fv, ff = jnp.concatenate(fvs, axis=0), jnp.concatenate(ffs, axis=0)
        fa, fb = ff >> 4, ff & (PEER_TOPK - 1)
        i1 = jnp.zeros((PEER_TOPK, TR), jnp.int32)
        i2 = jnp.zeros((PEER_TOPK, TR), jnp.int32)
        for a in range(PEER_TOPK):
            i1 = jnp.where(fa == a, si1[a:a + 1, :], i1)
            i2 = jnp.where(fb == a, si2[a:a + 1, :], i2)
        e = jnp.exp(fv - fv[0:1, :])
        g = e / jnp.sum(e, axis=0, keepdims=True)
        rows = pl.ds(pl.multiple_of(hd * PEER_TOPK, PEER_TOPK), PEER_TOPK)
        i1t_scr[rows, :] = i1.astype(F32)
        i2t_scr[rows, :] = i2.astype(F32)
        gtt_scr[rows, :] = g
        return carry

    lax.fori_loop(0, PEER_HEADS, per_head, 0)

    i1_scr[...] = i1t_scr[...].T
    i2_scr[...] = i2t_scr[...].T
    gt_scr[...] = gtt_scr[...].T
    key = lax.broadcasted_iota(jnp.int32, (PEER_NKEYS, 128), 0).astype(F32)

    def per_token(n, carry):
        row = pl.ds(n, 1)
        left = jnp.where(key == i1_scr[row, :], gt_scr[row, :], 0.0).astype(BF16)
        right = jnp.where(key == i2_scr[row, :], 1.0, 0.0).astype(BF16)
        gscr[pl.ds(n, PEER_NKEYS, stride=G_PITCH), :] = _dot_nt(left, right)
        return carry

    lax.fori_loop(0, TR, per_token, 0, unroll=16)

    def copy_out(i, carry):
        gate_ref[i] = gscr[pl.ds(pl.multiple_of(i * G_PITCH, 8), TR), :].astype(BF16)
        return carry

    lax.fori_loop(0, PEER_NKEYS, copy_out, 0, unroll=4)


def _peer_route(x, mods, g2, wq, subkeys):
    n_tiles = N_TOK // TR
    const = lambda shape: pl.BlockSpec(shape, lambda t: (0,) * len(shape))
    return pl.pallas_call(
        _route_kernel,
        grid=(n_tiles,),
        in_specs=[
            pl.BlockSpec((TR, D), lambda t: (t, 0)),
            pl.BlockSpec((1, 6, D), lambda t: (_mod_row(t, TR), 0, 0)),
            const((1, D)),
            const((D, PEER_HEADS * 2 * PEER_DK)),
            const((2, PEER_NKEYS, PEER_DK)),
        ],
        out_specs=[
            pl.BlockSpec((TR, D), lambda t: (t, 0)),
            pl.BlockSpec((PEER_NKEYS, TR, PEER_NKEYS), lambda t: (0, t, 0)),
        ],
        out_shape=[
            jax.ShapeDtypeStruct((N_TOK, D), BF16),
            jax.ShapeDtypeStruct((PEER_NKEYS, N_TOK, PEER_NKEYS), BF16),
        ],
        scratch_shapes=[
            pltpu.VMEM((2 * PEER_HEADS, TR, PEER_DK), F32),
            pltpu.VMEM((PEER_HEADS * PEER_TOPK, TR), F32),
            pltpu.VMEM((PEER_HEADS * PEER_TOPK, TR), F32),
            pltpu.VMEM((PEER_HEADS * PEER_TOPK, TR), F32),
            pltpu.VMEM((TR, PEER_HEADS * PEER_TOPK), F32),
            pltpu.VMEM((TR, PEER_HEADS * PEER_TOPK), F32),
            pltpu.VMEM((TR, PEER_HEADS * PEER_TOPK), F32),
            pltpu.VMEM((PEER_NKEYS * G_PITCH, PEER_NKEYS), F32),
        ],
        compiler_params=_cparams(("arbitrary",)),
        name="peer_route",
    )(x, mods, g2, wq, subkeys)


def _expert_kernel(x_ref, m_ref, hb_ref, gate_ref, u_ref, v_ref, o_ref):
    c = pl.program_id(1)

    @pl.when(c == 0)
    def _():
        o_ref[...] = jnp.zeros_like(o_ref)

    hb = hb_ref[...]
    slabs = []
    for s in range(EC // EC_SUB):
        pre = _dot_nt(hb, u_ref[s * EC_SUB:(s + 1) * EC_SUB, :].astype(BF16))
        act = 0.5 * pre * (1.0 + lax.erf(pre * (2.0 ** -0.5)))
        for a in range(EC_SUB // PEER_NKEYS):
            gate = gate_ref[s * (EC_SUB // PEER_NKEYS) + a].astype(F32)
            slabs.append((act[:, a * PEER_NKEYS:(a + 1) * PEER_NKEYS] * gate).astype(BF16))
    o_ref[...] += _dot(jnp.concatenate(slabs, axis=1), v_ref[...].astype(BF16))

    @pl.when(c == pl.num_programs(1) - 1)
    def _():
        o_ref[...] = x_ref[...] + m_ref[0, 5:6, :] * o_ref[...]


def _peer_experts(x, mods, hb, gate, u, v):
    n_tiles = N_TOK // TE
    n_chunks = PEER_EXPERTS // EC
    once = pl.Buffered(1)
    return pl.pallas_call(
        _expert_kernel,
        grid=(n_tiles, n_chunks),
        in_specs=[
            pl.BlockSpec((TE, D), lambda t, c: (t, 0), pipeline_mode=once),
            pl.BlockSpec((1, 6, D), lambda t, c: (_mod_row(t, TE), 0, 0)),
            pl.BlockSpec((TE, D), lambda t, c: (t, 0), pipeline_mode=once),
            pl.BlockSpec((EC // PEER_NKEYS, TE, PEER_NKEYS), lambda t, c: (c, t, 0)),
            pl.BlockSpec((EC, D), lambda t, c: (c, 0)),
            pl.BlockSpec((EC, D), lambda t, c: (c, 0)),
        ],
        out_specs=pl.BlockSpec((TE, D), lambda t, c: (t, 0)),
        out_shape=jax.ShapeDtypeStruct((N_TOK, D), F32),
        compiler_params=_cparams(("arbitrary", "arbitrary")),
        name="peer_experts",
    )(x, mods, hb, gate, u, v)


def _final_kernel(x_ref, g_ref, o_ref):
    x = x_ref[...]
    ms = jnp.mean(x * x, axis=-1, keepdims=True)
    o_ref[...] = x * lax.rsqrt(ms + EPS) * g_ref[...]


def _final_norm(x, g):
    return pl.pallas_call(
        _final_kernel,
        grid=(N_TOK // TM,),
        in_specs=[pl.BlockSpec((TM, D), lambda t: (t, 0)), pl.BlockSpec((1, D), lambda t: (0, 0))],
        out_specs=pl.BlockSpec((TM, D), lambda t: (t, 0)),
        out_shape=jax.ShapeDtypeStruct((N_TOK, D), F32),
        compiler_params=_cparams(("arbitrary",)),
        name="final_norm",
    )(x, g)


def _rope_tables():
    t = jnp.arange(DEC_SEQ)
    row, col = t // GRID_W, t % GRID_W
    nf = HEAD_DIM // 4
    inv = ROPE_THETA ** (-jnp.arange(nf, dtype=F32) / nf)
    ang_r = row.astype(F32)[:, None] * inv[None, :]
    ang_c = col.astype(F32)[:, None] * inv[None, :]
    cos = jnp.concatenate([jnp.cos(ang_r)] * 2 + [jnp.cos(ang_c)] * 2, axis=1)
    sin = jnp.concatenate([-jnp.sin(ang_r), jnp.sin(ang_r), -jnp.sin(ang_c), jnp.sin(ang_c)], axis=1)
    cos = jnp.concatenate([jnp.ones((TM, HEAD_DIM), F32), cos], axis=0)
    sin = jnp.concatenate([jnp.zeros((TM, HEAD_DIM), F32), sin], axis=0)
    return jnp.tile(cos, (1, N_HEADS)), jnp.tile(sin, (1, N_HEADS))


def _band_matrices():
    t = np.arange(TM)[:, None]
    j = np.arange(TM + 2 * HALO)[None, :] - HALO
    bands = [((j >= t - w // 2) & (j < t + w - w // 2)).astype(np.float32) for w in POOL_WINDOWS]
    return jnp.asarray(np.stack(bands), dtype=BF16)


def _head_block_diag():
    i = np.arange(128)
    return jnp.asarray((i[:, None] // HEAD_DIM == i[None, :] // HEAD_DIM).astype(np.float32), dtype=BF16)


def kernel(x_prompt, x_sample, cache_k, cache_v, c, c_ctx, mod_w, mod_b, norm1_g, norm2_g, a_in_w, a_out_w,
           q_norm_g, k_norm_g, pool_w, pool_scale, c_in_w, c_conv_w, c_out_w, peer_wq, peer_subkeys, peer_u,
           peer_v, final_g):
    x = jnp.concatenate([x_prompt.reshape(NP_TOK, D), x_sample.reshape(NS_TOK, D)], axis=0)
    ct = jnp.zeros((D, 8), F32).at[:, 0].set(c_ctx).at[:, 1:3].set(c.T)
    mods = _modulation(ct, mod_w, mod_b).reshape(DEPTH, 8, 6, D)
    rope_c, rope_s = _rope_tables()
    band = _band_matrices()
    bd = _head_block_diag()
    wq = peer_wq.astype(BF16)
    new_k, new_v = [], []
    for l in range(DEPTH):
        g1 = norm1_g[l].reshape(1, D)
        if l % 2 == 0:
            i = l // 2
            q, kk, vv, knew, vnew, p = _attn_in(
                x, mods[l], g1, a_in_w[i].astype(BF16), jnp.tile(q_norm_g[i], N_HEADS).reshape(1, ATTN_W),
                jnp.tile(k_norm_g[i], N_KV).reshape(1, KV_W), bd, rope_c, rope_s)
            new_k.append(knew[:NP_TOK].reshape(N_PROMPT_SEQ, SEQ, N_KV, HEAD_DIM))
            new_v.append(vnew[:NP_TOK].reshape(N_PROMPT_SEQ, SEQ, N_KV, HEAD_DIM))
            ck = cache_k[:, i]
            cv = cache_v[:, i]
            ctx_kk = jnp.concatenate([ck, ck[:, :, ::-1]], axis=2).reshape(N_SAMPLE_SEQ, PAST, 2 * KV_W).astype(BF16)
            ctx_vv = jnp.concatenate([cv, cv[:, :, ::-1]], axis=2).reshape(N_SAMPLE_SEQ, PAST, 2 * KV_W).astype(BF16)
            kk_p = kk[:NP_TOK].reshape(N_PROMPT_SEQ, SEQ, 2 * KV_W)
            vv_p = vv[:NP_TOK].reshape(N_PROMPT_SEQ, SEQ, 2 * KV_W)
            kk_s = jnp.concatenate([kk[NP_TOK:].reshape(N_SAMPLE_SEQ, DEC_SEQ, 2 * KV_W), ctx_kk], axis=1)
            vv_s = jnp.concatenate([vv[NP_TOK:].reshape(N_SAMPLE_SEQ, DEC_SEQ, 2 * KV_W), ctx_vv], axis=1)
            common = (band, pool_w[i].astype(BF16), pool_scale[i].reshape(1, POOL_W), a_out_w[i].astype(BF16))
            x = _attn_out(x, mods[l], q, kk_p, vv_p, p, *common,
                          first_tile=0, n_tiles=N_PROMPT_SEQ, tiles_per_seq=SEQ // TM)
            x = _attn_out(x, mods[l], q, kk_s, vv_s, p, *common,
                          first_tile=N_PROMPT_SEQ, n_tiles=NS_TOK // TM, tiles_per_seq=DEC_SEQ // TM)
        else:
            j = l // 2
            cw = jnp.zeros((8, D), F32).at[:3].set(c_conv_w[j])
            x = _conv_layer(x, mods[l], g1, c_in_w[j].astype(BF16), cw, c_out_w[j].astype(BF16))
        hb, gate = _peer_route(x, mods[l], norm2_g[l].reshape(1, D), wq[l], peer_subkeys[l])
        x = _peer_experts(x, mods[l], hb, gate, peer_u[l], peer_v[l])
    y = _final_norm(x, final_g.reshape(1, D))
    y_prompt = y[:NP_TOK].reshape(N_PROMPT_SEQ, SEQ, D)
    y_sample = y[NP_TOK:].reshape(N_SAMPLE_SEQ, DEC_SEQ, D)
    return (y_prompt, y_sample, jnp.stack(new_k, axis=1), jnp.stack(new_v, axis=1))
```

```python
import functools

import numpy as np
import jax
import jax.numpy as jnp
from jax import lax
from jax.experimental import pallas as pl
from jax.experimental.pallas import tpu as pltpu

F32 = jnp.float32
BF16 = jnp.bfloat16

D = 1024
N_PROMPT_SEQ, SEQ = 32, 256
N_SAMPLE_SEQ, DEC_SEQ = 2, 2048
PAST = 256
NP_TOK = N_PROMPT_SEQ * SEQ
NS_TOK = N_SAMPLE_SEQ * DEC_SEQ
N_TOK = NP_TOK + NS_TOK
DEPTH = 4
GRID_W = 64
N_HEADS, N_KV, HEAD_DIM = 8, 2, 64
ATTN_W, KV_W, POOL_W = 512, 128, 512
POOL_WINDOWS = (2, 4, 8, 16)
A_IN_W = ATTN_W + 2 * KV_W + POOL_W
PEER_HEADS, PEER_NKEYS, PEER_TOPK, PEER_DK = 8, 128, 16, 128
PEER_EXPERTS = PEER_NKEYS * PEER_NKEYS
ROPE_THETA = 10000.0
EPS = 1e-6

TM = 256
HALO = 8
TR = 256
G_PITCH = TR + 8
TE = 1024
EC = 1024
EC_SUB = 512
VMEM_LIMIT = 56 * 1024 * 1024


def _cparams(sem):
    return pltpu.CompilerParams(dimension_semantics=sem, vmem_limit_bytes=VMEM_LIMIT)


def _mod_row(tile, tile_tokens):
    start = tile * tile_tokens
    return (start >= NP_TOK).astype(jnp.int32) + (start >= NP_TOK + DEC_SEQ).astype(jnp.int32)


def _modnorm(x, g, shift, scale):
    ms = jnp.mean(x * x, axis=-1, keepdims=True)
    y = x * lax.rsqrt(ms + EPS) * g
    return y * (1.0 + scale) + shift


def _split_bf16(a):
    hi = a.astype(BF16)
    lo = (a - hi.astype(F32)).astype(BF16)
    return hi, lo


def _dot(a, b):
    return jnp.dot(a, b, preferred_element_type=F32)


def _dot_nt(a, b):
    return lax.dot_general(a, b, (((1,), (1,)), ((), ())), preferred_element_type=F32)


def _mod_kernel(ct_ref, w_ref, b_ref, o_ref):
    ct = ct_ref[...]
    s = ct * jax.nn.sigmoid(ct)
    w = w_ref[0]
    rows = [jnp.sum(s[:, r:r + 1] * w, axis=0, keepdims=True) + b_ref[0] for r in range(3)]
    rows.append(jnp.zeros((5, w.shape[1]), F32))
    o_ref[0] = jnp.concatenate(rows, axis=0)


def _modulation(ct, mod_w, mod_b):
    tn = 1536
    return pl.pallas_call(
        _mod_kernel,
        grid=(DEPTH, 6 * D // tn),
        in_specs=[
            pl.BlockSpec((D, 8), lambda l, j: (0, 0)),
            pl.BlockSpec((1, D, tn), lambda l, j: (l, 0, j)),
            pl.BlockSpec((1, 1, tn), lambda l, j: (l, 0, j)),
        ],
        out_specs=pl.BlockSpec((1, 8, tn), lambda l, j: (l, 0, j)),
        out_shape=jax.ShapeDtypeStruct((DEPTH, 8, 6 * D), F32),
        compiler_params=_cparams(("arbitrary", "arbitrary")),
        name="adaln_modulation",
    )(ct, mod_w, mod_b.reshape(DEPTH, 1, 6 * D))


def _head_norm(x, g, bd):
    outs = []
    for c in range(x.shape[1] // 128):
        xc = x[:, c * 128:(c + 1) * 128]
        hi, lo = _split_bf16(xc * xc)
        ms = (_dot(hi, bd) + _dot(lo, bd)) * (1.0 / HEAD_DIM)
        outs.append(xc * lax.rsqrt(ms + EPS))
    return jnp.concatenate(outs, axis=1) * g


def _rope(x, cos, sin, low16):
    outs = []
    for c in range(x.shape[1] // 128):
        sl = slice(c * 128, (c + 1) * 128)
        xc = x[:, sl]
        partner = jnp.where(low16, pltpu.roll(xc, 112, 1), pltpu.roll(xc, 16, 1))
        outs.append(xc * cos[:, sl] + partner * sin[:, sl])
    return jnp.concatenate(outs, axis=1)


def _a1_kernel(x_ref, m_ref, g1_ref, w_ref, qg_ref, kg_ref, bd_ref, rc_ref, rs_ref,
               q_ref, kk_ref, vv_ref, knew_ref, vnew_ref, p_ref):
    h = _modnorm(x_ref[...], g1_ref[...], m_ref[0, 0:1, :], m_ref[0, 1:2, :])
    z = _dot(h.astype(BF16), w_ref[...])
    bd = bd_ref[...]
    q = _head_norm(z[:, :ATTN_W], qg_ref[...], bd)
    k = _head_norm(z[:, ATTN_W:ATTN_W + KV_W], kg_ref[...], bd)
    v = z[:, ATTN_W + KV_W:ATTN_W + 2 * KV_W]
    knew_ref[...] = k
    vnew_ref[...] = v
    p_ref[...] = z[:, ATTN_W + 2 * KV_W:]
    cos, sin = rc_ref[...], rs_ref[...]
    lane = lax.broadcasted_iota(jnp.int32, (TM, 128), 1)
    low16 = (lane & 31) < 16
    q = _rope(q, cos, sin, low16) * (HEAD_DIM ** -0.5)
    k = _rope(k, cos, sin, low16)
    q_ref[...] = q.astype(BF16)
    kk_ref[...] = jnp.concatenate([k, pltpu.roll(k, 64, 1)], axis=1).astype(BF16)
    vv_ref[...] = jnp.concatenate([v, pltpu.roll(v, 64, 1)], axis=1).astype(BF16)


def _rope_block(t):
    return jnp.where(t < N_PROMPT_SEQ, 0, 1 + (t - N_PROMPT_SEQ) % (DEC_SEQ // TM))


def _attn_in(x, mods, g1, w_in, qg, kg, bd, rope_c, rope_s):
    n_tiles = N_TOK // TM
    tile = lambda w: pl.BlockSpec((TM, w), lambda t: (t, 0))
    const = lambda shape: pl.BlockSpec(shape, lambda t: (0,) * len(shape))
    return pl.pallas_call(
        _a1_kernel,
        grid=(n_tiles,),
        in_specs=[
            tile(D),
            pl.BlockSpec((1, 6, D), lambda t: (_mod_row(t, TM), 0, 0)),
            const((1, D)),
            const((D, A_IN_W)),
            const((1, ATTN_W)),
            const((1, KV_W)),
            const((128, 128)),
            pl.BlockSpec((TM, ATTN_W), lambda t: (_rope_block(t), 0)),
            pl.BlockSpec((TM, ATTN_W), lambda t: (_rope_block(t), 0)),
        ],
        out_specs=[tile(ATTN_W), tile(2 * KV_W), tile(2 * KV_W), tile(KV_W), tile(KV_W), tile(POOL_W)],
        out_shape=[
            jax.ShapeDtypeStruct((N_TOK, ATTN_W), BF16),
            jax.ShapeDtypeStruct((N_TOK, 2 * KV_W), BF16),
            jax.ShapeDtypeStruct((N_TOK, 2 * KV_W), BF16),
            jax.ShapeDtypeStruct((N_TOK, KV_W), F32),
            jax.ShapeDtypeStruct((N_TOK, KV_W), F32),
            jax.ShapeDtypeStruct((N_TOK, POOL_W), F32),
        ],
        compiler_params=_cparams(("arbitrary",)),
        name="attn_in_proj",
    )(x, mods, g1, w_in, qg, kg, bd, rope_c, rope_s)


def _a2_kernel(x_ref, m_ref, q_ref, kk_ref, vv_ref, p_ref, pp_ref, pn_ref, band_ref, pw_ref, ps_ref,
               wo_ref, o_ref, *, tiles_per_seq):
    t = pl.program_id(0)
    q = q_ref[...]
    kk = kk_ref[0]
    vv = vv_ref[0]
    lane = lax.broadcasted_iota(jnp.int32, (TM, 128), 1)
    half_mask = (lane < HEAD_DIM, lane >= HEAD_DIM)
    zero_q = jnp.zeros((TM, 128), BF16)
    chunks = []
    for c in range(ATTN_W // 128):
        qc = q[:, c * 128:(c + 1) * 128]
        acc = None
        for half in range(2):
            kv = (2 * c + half) // (N_HEADS // N_KV)
            slab = slice(0, 128) if kv == half else slice(128, 256)
            s = _dot_nt(jnp.where(half_mask[half], qc, zero_q), kk[:, slab])
            e = jnp.exp(s - jnp.max(s, axis=-1, keepdims=True))
            l = jnp.sum(e, axis=-1, keepdims=True)
            o = _dot(e.astype(BF16), vv[:, slab]) / l
            o = jnp.where(half_mask[half], o, 0.0)
            acc = o if acc is None else acc + o
        chunks.append(acc)

    pos = t % tiles_per_seq
    has_prev = jnp.where(pos == 0, 0.0, 1.0)
    has_next = jnp.where(pos == tiles_per_seq - 1, 0.0, 1.0)
    p = p_ref[...]
    pext = jnp.concatenate([pp_ref[...] * has_prev, p, pn_ref[...] * has_next], axis=0)
    row = lax.broadcasted_iota(jnp.int32, (TM + 2 * HALO, 128), 0)
    valid = jnp.where(row < HALO, has_prev, jnp.where(row >= TM + HALO, has_next, 1.0)).astype(BF16)
    hi, lo = _split_bf16(pext)
    for g in range(len(POOL_WINDOWS)):
        sl = slice(g * 128, (g + 1) * 128)
        band = band_ref[g]
        tot = _dot(band, hi[:, sl]) + _dot(band, lo[:, sl])
        cnt = _dot(band, valid)
        dev = tot / cnt - p[:, sl]
        chunks.append(_dot(dev.astype(BF16), pw_ref[g]) * ps_ref[:, sl])
    mix = jnp.concatenate(chunks, axis=1).astype(BF16)
    o_ref[...] = x_ref[...] + m_ref[0, 2:3, :] * _dot(mix, wo_ref[...])


def _attn_out(x, mods, q, kk_seq, vv_seq, p, band, pool_w, pool_s, w_out, *, first_tile, n_tiles, tiles_per_seq):
    s_len = kk_seq.shape[1]
    off = first_tile
    n_halo_blocks = N_TOK // HALO
    per_tile = TM // HALO
    tile = lambda w: pl.BlockSpec((TM, w), lambda t: (t + off, 0))
    const = lambda shape: pl.BlockSpec(shape, lambda t: (0,) * len(shape))
    seq = pl.BlockSpec((1, s_len, 2 * KV_W), lambda t: (t // tiles_per_seq, 0, 0))
    return pl.pallas_call(
        functools.partial(_a2_kernel, tiles_per_seq=tiles_per_seq),
        grid=(n_tiles,),
        in_specs=[
            tile(D),
            pl.BlockSpec((1, 6, D), lambda t: (_mod_row(t + off, TM), 0, 0)),
            tile(ATTN_W),
            seq,
            seq,
            tile(POOL_W),
            pl.BlockSpec((HALO, POOL_W), lambda t: (jnp.maximum((t + off) * per_tile - 1, 0), 0)),
            pl.BlockSpec((HALO, POOL_W), lambda t: (jnp.minimum((t + off + 1) * per_tile, n_halo_blocks - 1), 0)),
            const((len(POOL_WINDOWS), TM, TM + 2 * HALO)),
            const((len(POOL_WINDOWS), 128, 128)),
            const((1, POOL_W)),
            const((D, D)),
        ],
        out_specs=tile(D),
        out_shape=jax.ShapeDtypeStruct((N_TOK, D), F32),
        input_output_aliases={0: 0},
        compiler_params=_cparams(("arbitrary",)),
        name="attn_pool_out_%d" % s_len,
    )(x, mods, q, kk_seq, vv_seq, p, p, p, band, pool_w, pool_s, w_out)


def _conv_kernel(x_ref, xp_ref, xn_ref, m_ref, g1_ref, wi_ref, cw_ref, wo_ref, o_ref):
    t = pl.program_id(0)
    is_prompt = t < N_PROMPT_SEQ
    pos = t & (DEC_SEQ // TM - 1)
    has_prev = jnp.where(is_prompt | (pos == 0), 0.0, 1.0)
    has_next = jnp.where(is_prompt | (pos == DEC_SEQ // TM - 1), 0.0, 1.0)
    x = x_ref[...]
    xext = jnp.concatenate([xp_ref[...], x, xn_ref[...]], axis=0)
    h = _modnorm(xext, g1_ref[...], m_ref[0, 0:1, :], m_ref[0, 1:2, :])
    z = _dot(h.astype(BF16), wi_ref[...])
    gb = z[HALO:HALO + TM, :D]
    u = z[:, D:2 * D] * z[:, 2 * D:]
    row = lax.broadcasted_iota(jnp.int32, (TM + 2 * HALO, 1), 0)
    u = u * jnp.where(row < HALO, has_prev, jnp.where(row >= TM + HALO, has_next, 1.0))
    rows = TM + 2 * HALO
    cw = cw_ref[...]
    y = (pltpu.roll(u, 1, 0)[HALO:HALO + TM] * cw[0:1, :]
         + u[HALO:HALO + TM] * cw[1:2, :]
         + pltpu.roll(u, rows - 1, 0)[HALO:HALO + TM] * cw[2:3, :])
    out = _dot((gb * y).astype(BF16), wo_ref[...])
    o_ref[...] = x + m_ref[0, 2:3, :] * out


def _conv_layer(x, mods, g1, w_in, conv_w, w_out):
    n_tiles = N_TOK // TM
    n_halo_blocks = N_TOK // HALO
    per_tile = TM // HALO
    const = lambda shape: pl.BlockSpec(shape, lambda t: (0,) * len(shape))
    return pl.pallas_call(
        _conv_kernel,
        grid=(n_tiles,),
        in_specs=[
            pl.BlockSpec((TM, D), lambda t: (t, 0)),
            pl.BlockSpec((HALO, D), lambda t: (jnp.maximum(t * per_tile - 1, 0), 0)),
            pl.BlockSpec((HALO, D), lambda t: (jnp.minimum((t + 1) * per_tile, n_halo_blocks - 1), 0)),
            pl.BlockSpec((1, 6, D), lambda t: (_mod_row(t, TM), 0, 0)),
            const((1, D)),
            const((D, 3 * D)),
            const((8, D)),
            const((D, D)),
        ],
        out_specs=pl.BlockSpec((TM, D), lambda t: (t, 0)),
        out_shape=jax.ShapeDtypeStruct((N_TOK, D), F32),
        compiler_params=_cparams(("arbitrary",)),
        name="conv_mixer",
    )(x, x, x, mods, g1, w_in, conv_w, w_out)


NEG_INF = float("-inf")
SUB = 8
NO_ROW = 1 << 20


def _col_max(vals):
    while len(vals) > 1:
        vals = [jnp.maximum(vals[a], vals[a + 1]) for a in range(0, len(vals) - 1, 2)] + vals[len(vals) & ~1:]
    return jnp.max(vals[0], axis=0, keepdims=True)


def _col_min(vals):
    while len(vals) > 1:
        vals = [jnp.minimum(vals[a], vals[a + 1]) for a in range(0, len(vals) - 1, 2)] + vals[len(vals) & ~1:]
    return jnp.min(vals[0], axis=0, keepdims=True)


def _col_argmax(vals, sub):
    ids = list(range(len(vals)))
    while len(vals) > 1:
        nv, ni = [], []
        for a in range(0, len(vals) - 1, 2):
            nv.append(jnp.maximum(vals[a], vals[a + 1]))
            ni.append(jnp.where(vals[a] >= vals[a + 1], ids[a], ids[a + 1]))
        vals, ids = nv + vals[len(vals) & ~1:], ni + ids[len(ids) & ~1:]
    rows = ids[0] * SUB + sub
    m = jnp.max(vals[0], axis=0, keepdims=True)
    return m, jnp.min(jnp.where(vals[0] == m, rows, NO_ROW), axis=0, keepdims=True)


def _top16_keys(s, sub):
    blocks = [s[SUB * v:SUB * (v + 1), :] for v in range(PEER_NKEYS // SUB)]
    vals, idxs = [], []
    for _ in range(PEER_TOPK):
        m, ix = _col_argmax(blocks, sub)
        hit = jnp.broadcast_to(ix, sub.shape) - sub
        blocks = [jnp.where(hit == SUB * v, NEG_INF, b) for v, b in enumerate(blocks)]
        vals.append(m)
        idxs.append(ix)
    return jnp.concatenate(vals, axis=0), jnp.concatenate(idxs, axis=0)


def _pair_candidates(sv1, sv2, sub):
    a_lo, a_hi, b_lo, b_hi = sv1[0:SUB], sv1[SUB:], sv2[0:SUB], sv2[SUB:]
    row = lambda x, r: x[r:r + 1, :]
    keep = lambda cond, v: jnp.where(cond, v, NEG_INF)
    return [
        (row(sv1, 0) + b_lo, sub),
        (row(sv1, 0) + b_hi, sub + SUB),
        (keep(sub >= 1, a_lo + row(sv2, 0)), sub * PEER_TOPK),
        (a_hi + row(sv2, 0), (sub + SUB) * PEER_TOPK),
        (keep(sub >= 1, row(sv1, 1) + b_lo), sub + PEER_TOPK),
        (keep(sub >= 2, a_lo + row(sv2, 1)), sub * PEER_TOPK + 1),
        (keep((sub >= 2) & (sub <= 4), row(sv1, 2) + b_lo), sub + 2 * PEER_TOPK),
        (keep((sub >= 3) & (sub <= 4), a_lo + row(sv2, 2)), sub * PEER_TOPK + 2),
        (keep(sub == 3, row(sv1, 3) + b_lo), sub + 3 * PEER_TOPK),
    ]


def _route_kernel(x_ref, m_ref, g2_ref, wq_ref, sk_ref, hb_ref, gate_ref,
                  q_scr, i1t_scr, i2t_scr, gtt_scr, i1_scr, i2_scr, gt_scr, gscr):
    @pl.when(pl.program_id(0) == 0)
    def _():
        i1_scr[...] = jnp.zeros_like(i1_scr)
        i2_scr[...] = jnp.zeros_like(i2_scr)
        gt_scr[...] = jnp.zeros_like(gt_scr)

    key = lax.broadcasted_iota(jnp.int32, (PEER_NKEYS, 128), 0).astype(F32)

    def expand_token(n):
        row = pl.ds(n, 1)
        left = jnp.where(key == i1_scr[row, :], gt_scr[row, :], 0.0).astype(BF16)
        right = jnp.where(key == i2_scr[row, :], 1.0, 0.0).astype(BF16)
        gscr[pl.ds(n, PEER_NKEYS, stride=G_PITCH), :] = _dot_nt(left, right)

    h = _modnorm(x_ref[...], g2_ref[...], m_ref[0, 3:4, :], m_ref[0, 4:5, :])
    hb = h.astype(BF16)
    hb_ref[...] = hb
    q = _dot(hb, wq_ref[...])
    for hp in range(2 * PEER_HEADS):
        q_scr[hp] = q[:, hp * PEER_DK:(hp + 1) * PEER_DK]
    sub = lax.broadcasted_iota(jnp.int32, (SUB, TR), 0)
    sk_split = [_split_bf16(sk_ref[p]) for p in range(2)]

    def key_stage(hd):
        top = ()
        for p in range(2):
            q_hi, q_lo = _split_bf16(q_scr[2 * hd + p])
            k_hi, k_lo = sk_split[p]
            s = _dot_nt(k_hi, q_hi) + (_dot_nt(k_hi, q_lo) + _dot_nt(k_lo, q_hi))
            top += _top16_keys(s, sub)
        return top

    def pair_stage(hd, top):
        sv1, si1, sv2, si2 = top
        cands = _pair_candidates(sv1, sv2, sub)
        vals, flats = [c[0] for c in cands], [c[1] for c in cands]
        fvs, ffs = [], []
        for _ in range(PEER_TOPK):
            m = _col_max(vals)
            f = _col_min([jnp.where(v == m, fl, NO_ROW) for v, fl in zip(vals, flats)])
            vals = [jnp.where(fl == f, NEG_INF, v) for v, fl in zip(vals, flats)]
            fvs.append(m)
            ffs.append(f)
        fv, ff = jnp.concatenate(fvs, axis=0), jnp.concatenate(ffs, axis=0)
        fa, fb = ff >> 4, ff & (PEER_TOPK - 1)
        i1 = jnp.zeros((PEER_TOPK, TR), jnp.int32)
        i2 = jnp.zeros((PEER_TOPK, TR), jnp.int32)
        for a in range(PEER_TOPK):
            i1 = jnp.where(fa == a, si1[a:a + 1, :], i1)
            i2 = jnp.where(fb == a, si2[a:a + 1, :], i2)
        e = jnp.exp(fv - fv[0:1, :])
        g = e / jnp.sum(e, axis=0, keepdims=True)
        start = hd * PEER_TOPK
        rows = pl.ds(start if isinstance(start, int) else pl.multiple_of(start, PEER_TOPK), PEER_TOPK)
        i1t_scr[rows, :] = i1.astype(F32)
        i2t_scr[rows, :] = i2.astype(F32)
        gtt_scr[rows, :] = g

    n_first, n_mid = TR // 8, (TR - TR // 8) // 8
    top = key_stage(0)
    for k in range(n_first):
        expand_token(k)

    def per_head(hd, top_prev):
        top_cur = key_stage(hd)
        pair_stage(hd - 1, top_prev)
        for k in range(n_mid):
            expand_token(n_first + (hd - 1) * n_mid + k)
        return top_cur

    top = lax.fori_loop(1, PEER_HEADS, per_head, top)
    pair_stage(PEER_HEADS - 1, top)
    for n in range(n_first + (PEER_HEADS - 1) * n_mid, TR):
        expand_token(n)

    def copy_out(i, carry):
        gate_ref[i] = gscr[pl.ds(pl.multiple_of(i * G_PITCH, 8), TR), :].astype(BF16)
        return carry

    lax.fori_loop(0, PEER_NKEYS, copy_out, 0, unroll=4)

    i1_scr[...] = i1t_scr[...].T
    i2_scr[...] = i2t_scr[...].T
    gt_scr[...] = gtt_scr[...].T


def _peer_route(x, mods, g2, wq, subkeys):
    n_tiles = N_TOK // TR
    const = lambda shape: pl.BlockSpec(shape, lambda t: (0,) * len(shape))
    cur = lambda t: jnp.minimum(t, n_tiles - 1)
    return pl.pallas_call(
        _route_kernel,
        grid=(n_tiles + 1,),
        in_specs=[
            pl.BlockSpec((TR, D), lambda t: (cur(t), 0)),
            pl.BlockSpec((1, 6, D), lambda t: (_mod_row(cur(t), TR), 0, 0)),
            const((1, D)),
            const((D, PEER_HEADS * 2 * PEER_DK)),
            const((2, PEER_NKEYS, PEER_DK)),
        ],
        out_specs=[
            pl.BlockSpec((TR, D), lambda t: (cur(t), 0)),
            pl.BlockSpec((PEER_NKEYS, TR, PEER_NKEYS), lambda t: (0, jnp.maximum(t - 1, 0), 0)),
        ],
        out_shape=[
            jax.ShapeDtypeStruct((N_TOK, D), BF16),
            jax.ShapeDtypeStruct((PEER_NKEYS, N_TOK, PEER_NKEYS), BF16),
        ],
        scratch_shapes=[
            pltpu.VMEM((2 * PEER_HEADS, TR, PEER_DK), F32),
            pltpu.VMEM((PEER_HEADS * PEER_TOPK, TR), F32),
            pltpu.VMEM((PEER_HEADS * PEER_TOPK, TR), F32),
            pltpu.VMEM((PEER_HEADS * PEER_TOPK, TR), F32),
            pltpu.VMEM((TR, PEER_HEADS * PEER_TOPK), F32),
            pltpu.VMEM((TR, PEER_HEADS * PEER_TOPK), F32),
            pltpu.VMEM((TR, PEER_HEADS * PEER_TOPK), F32),
            pltpu.VMEM((PEER_NKEYS * G_PITCH, PEER_NKEYS), F32),
        ],
        compiler_params=_cparams(("arbitrary",)),
        name="peer_route",
    )(x, mods, g2, wq, subkeys)


def _expert_kernel(x_ref, m_ref, hb_ref, gate_ref, u_ref, v_ref, o_ref):
    c = pl.program_id(1)

    @pl.when(c == 0)
    def _():
        o_ref[...] = jnp.zeros_like(o_ref)

    hb = hb_ref[...]
    slabs = []
    for s in range(EC // EC_SUB):
        pre = _dot_nt(hb, u_ref[0, s * EC_SUB:(s + 1) * EC_SUB, :].astype(BF16))
        act = 0.5 * pre * (1.0 + lax.erf(pre * (2.0 ** -0.5)))
        for a in range(EC_SUB // PEER_NKEYS):
            gate = gate_ref[s * (EC_SUB // PEER_NKEYS) + a].astype(F32)
            slabs.append((act[:, a * PEER_NKEYS:(a + 1) * PEER_NKEYS] * gate).astype(BF16))
    o_ref[...] += _dot(jnp.concatenate(slabs, axis=1), v_ref[0].astype(BF16))

    @pl.when(c == pl.num_programs(1) - 1)
    def _():
        o_ref[...] = x_ref[...] + m_ref[0, 5:6, :] * o_ref[...]


def _peer_experts(x, mods, hb, gate, u_all, v_all, layer):
    n_tiles = N_TOK // TE
    n_chunks = PEER_EXPERTS // EC
    once = pl.Buffered(1)
    return pl.pallas_call(
        _expert_kernel,
        grid=(n_tiles, n_chunks),
        in_specs=[
            pl.BlockSpec((TE, D), lambda t, c: (t, 0), pipeline_mode=once),
            pl.BlockSpec((1, 6, D), lambda t, c: (_mod_row(t, TE), 0, 0)),
            pl.BlockSpec((TE, D), lambda t, c: (t, 0), pipeline_mode=once),
            pl.BlockSpec((EC // PEER_NKEYS, TE, PEER_NKEYS), lambda t, c: (c, t, 0)),
            pl.BlockSpec((1, EC, D), lambda t, c: (layer, c, 0)),
            pl.BlockSpec((1, EC, D), lambda t, c: (layer, c, 0)),
        ],
        out_specs=pl.BlockSpec((TE, D), lambda t, c: (t, 0)),
        out_shape=jax.ShapeDtypeStruct((N_TOK, D), F32),
        compiler_params=_cparams(("arbitrary", "arbitrary")),
        name="peer_experts",
    )(x, mods, hb, gate, u_all, v_all)


def _final_kernel(x_ref, g_ref, o_ref):
    x = x_ref[...]
    ms = jnp.mean(x * x, axis=-1, keepdims=True)
    o_ref[...] = x * lax.rsqrt(ms + EPS) * g_ref[...]


def _final_norm(x, g):
    return pl.pallas_call(
        _final_kernel,
        grid=(N_TOK // TM,),
        in_specs=[pl.BlockSpec((TM, D), lambda t: (t, 0)), pl.BlockSpec((1, D), lambda t: (0, 0))],
        out_specs=pl.BlockSpec((TM, D), lambda t: (t, 0)),
        out_shape=jax.ShapeDtypeStruct((N_TOK, D), F32),
        compiler_params=_cparams(("arbitrary",)),
        name="final_norm",
    )(x, g)


def _rope_tables():
    t = jnp.arange(DEC_SEQ)
    row, col = t // GRID_W, t % GRID_W
    nf = HEAD_DIM // 4
    inv = ROPE_THETA ** (-jnp.arange(nf, dtype=F32) / nf)
    ang_r = row.astype(F32)[:, None] * inv[None, :]
    ang_c = col.astype(F32)[:, None] * inv[None, :]
    cos = jnp.concatenate([jnp.cos(ang_r)] * 2 + [jnp.cos(ang_c)] * 2, axis=1)
    sin = jnp.concatenate([-jnp.sin(ang_r), jnp.sin(ang_r), -jnp.sin(ang_c), jnp.sin(ang_c)], axis=1)
    cos = jnp.concatenate([jnp.ones((TM, HEAD_DIM), F32), cos], axis=0)
    sin = jnp.concatenate([jnp.zeros((TM, HEAD_DIM), F32), sin], axis=0)
    return jnp.tile(cos, (1, N_HEADS)), jnp.tile(sin, (1, N_HEADS))


def _band_matrices():
    t = np.arange(TM)[:, None]
    j = np.arange(TM + 2 * HALO)[None, :] - HALO
    bands = [((j >= t - w // 2) & (j < t + w - w // 2)).astype(np.float32) for w in POOL_WINDOWS]
    return jnp.asarray(np.stack(bands), dtype=BF16)


def _head_block_diag():
    i = np.arange(128)
    return jnp.asarray((i[:, None] // HEAD_DIM == i[None, :] // HEAD_DIM).astype(np.float32), dtype=BF16)


def kernel(x_prompt, x_sample, cache_k, cache_v, c, c_ctx, mod_w, mod_b, norm1_g, norm2_g, a_in_w, a_out_w,
           q_norm_g, k_norm_g, pool_w, pool_scale, c_in_w, c_conv_w, c_out_w, peer_wq, peer_subkeys, peer_u,
           peer_v, final_g):
    x = jnp.concatenate([x_prompt.reshape(NP_TOK, D), x_sample.reshape(NS_TOK, D)], axis=0)
    ct = jnp.zeros((D, 8), F32).at[:, 0].set(c_ctx).at[:, 1:3].set(c.T)
    mods = _modulation(ct, mod_w, mod_b).reshape(DEPTH, 8, 6, D)
    rope_c, rope_s = _rope_tables()
    band = _band_matrices()
    bd = _head_block_diag()
    wq = peer_wq.astype(BF16)
    new_k, new_v = [], []
    for l in range(DEPTH):
        g1 = norm1_g[l].reshape(1, D)
        if l % 2 == 0:
            i = l // 2
            q, kk, vv, knew, vnew, p = _attn_in(
                x, mods[l], g1, a_in_w[i].astype(BF16), jnp.tile(q_norm_g[i], N_HEADS).reshape(1, ATTN_W),
                jnp.tile(k_norm_g[i], N_KV).reshape(1, KV_W), bd, rope_c, rope_s)
            new_k.append(knew[:NP_TOK].reshape(N_PROMPT_SEQ, SEQ, N_KV, HEAD_DIM))
            new_v.append(vnew[:NP_TOK].reshape(N_PROMPT_SEQ, SEQ, N_KV, HEAD_DIM))
            ck = cache_k[:, i]
            cv = cache_v[:, i]
            ctx_kk = jnp.concatenate([ck, ck[:, :, ::-1]], axis=2).reshape(N_SAMPLE_SEQ, PAST, 2 * KV_W).astype(BF16)
            ctx_vv = jnp.concatenate([cv, cv[:, :, ::-1]], axis=2).reshape(N_SAMPLE_SEQ, PAST, 2 * KV_W).astype(BF16)
            kk_p = kk[:NP_TOK].reshape(N_PROMPT_SEQ, SEQ, 2 * KV_W)
            vv_p = vv[:NP_TOK].reshape(N_PROMPT_SEQ, SEQ, 2 * KV_W)
            kk_s = jnp.concatenate([kk[NP_TOK:].reshape(N_SAMPLE_SEQ, DEC_SEQ, 2 * KV_W), ctx_kk], axis=1)
            vv_s = jnp.concatenate([vv[NP_TOK:].reshape(N_SAMPLE_SEQ, DEC_SEQ, 2 * KV_W), ctx_vv], axis=1)
            common = (band, pool_w[i].astype(BF16), pool_scale[i].reshape(1, POOL_W), a_out_w[i].astype(BF16))
            x = _attn_out(x, mods[l], q, kk_p, vv_p, p, *common,
                          first_tile=0, n_tiles=N_PROMPT_SEQ, tiles_per_seq=SEQ // TM)
            x = _attn_out(x, mods[l], q, kk_s, vv_s, p, *common,
                          first_tile=N_PROMPT_SEQ, n_tiles=NS_TOK // TM, tiles_per_seq=DEC_SEQ // TM)
        else:
            j = l // 2
            cw = jnp.zeros((8, D), F32).at[:3].set(c_conv_w[j])
            x = _conv_layer(x, mods[l], g1, c_in_w[j].astype(BF16), cw, c_out_w[j].astype(BF16))
        hb, gate = _peer_route(x, mods[l], norm2_g[l].reshape(1, D), wq[l], peer_subkeys[l])
        x = _peer_experts(x, mods[l], hb, gate, peer_u, peer_v, l)
    y = _final_norm(x, final_g.reshape(1, D))
    y_prompt = y[:NP_TOK].reshape(N_PROMPT_SEQ, SEQ, D)
    y_sample = y[NP_TOK:].reshape(N_SAMPLE_SEQ, DEC_SEQ, D)
    return (y_prompt, y_sample, jnp.stack(new_k, axis=1), jnp.stack(new_v, axis=1))
```

```python
import functools

import numpy as np
import jax
import jax.numpy as jnp
from jax import lax
from jax.experimental import pallas as pl
from jax.experimental.pallas import tpu as pltpu

F32 = jnp.float32
BF16 = jnp.bfloat16

D = 1024
N_PROMPT_SEQ, SEQ = 32, 256
N_SAMPLE_SEQ, DEC_SEQ = 2, 2048
PAST = 256
NP_TOK = N_PROMPT_SEQ * SEQ
NS_TOK = N_SAMPLE_SEQ * DEC_SEQ
N_TOK = NP_TOK + NS_TOK
DEPTH = 4
GRID_W = 64
N_HEADS, N_KV, HEAD_DIM = 8, 2, 64
ATTN_W, KV_W, POOL_W = 512, 128, 512
POOL_WINDOWS = (2, 4, 8, 16)
A_IN_W = ATTN_W + 2 * KV_W + POOL_W
PEER_HEADS, PEER_NKEYS, PEER_TOPK, PEER_DK = 8, 128, 16, 128
PEER_EXPERTS = PEER_NKEYS * PEER_NKEYS
ROPE_THETA = 10000.0
EPS = 1e-6

TM = 256
HALO = 8
TR = 256
G_PITCH = TR + 8
TE = 1024
EC = 1024
EC_SUB = 512
VMEM_LIMIT = 56 * 1024 * 1024


def _cparams(sem):
    return pltpu.CompilerParams(dimension_semantics=sem, vmem_limit_bytes=VMEM_LIMIT)


def _mod_row(tile, tile_tokens):
    start = tile * tile_tokens
    return (start >= NP_TOK).astype(jnp.int32) + (start >= NP_TOK + DEC_SEQ).astype(jnp.int32)


def _modnorm(x, g, shift, scale):
    ms = jnp.mean(x * x, axis=-1, keepdims=True)
    y = x * lax.rsqrt(ms + EPS) * g
    return y * (1.0 + scale) + shift


def _split_bf16(a):
    hi = a.astype(BF16)
    lo = (a - hi.astype(F32)).astype(BF16)
    return hi, lo


def _dot(a, b):
    return jnp.dot(a, b, preferred_element_type=F32)


def _dot_nt(a, b):
    return lax.dot_general(a, b, (((1,), (1,)), ((), ())), preferred_element_type=F32)


def _mod_kernel(ct_ref, w_ref, b_ref, o_ref):
    ct = ct_ref[...]
    s = ct * jax.nn.sigmoid(ct)
    w = w_ref[0]
    rows = [jnp.sum(s[:, r:r + 1] * w, axis=0, keepdims=True) + b_ref[0] for r in range(3)]
    rows.append(jnp.zeros((5, w.shape[1]), F32))
    o_ref[0] = jnp.concatenate(rows, axis=0)


def _modulation(ct, mod_w, mod_b):
    tn = 1536
    return pl.pallas_call(
        _mod_kernel,
        grid=(DEPTH, 6 * D // tn),
        in_specs=[
            pl.BlockSpec((D, 8), lambda l, j: (0, 0)),
            pl.BlockSpec((1, D, tn), lambda l, j: (l, 0, j)),
            pl.BlockSpec((1, 1, tn), lambda l, j: (l, 0, j)),
        ],
        out_specs=pl.BlockSpec((1, 8, tn), lambda l, j: (l, 0, j)),
        out_shape=jax.ShapeDtypeStruct((DEPTH, 8, 6 * D), F32),
        compiler_params=_cparams(("arbitrary", "arbitrary")),
        name="adaln_modulation",
    )(ct, mod_w, mod_b.reshape(DEPTH, 1, 6 * D))


def _head_norm(x, g, bd):
    outs = []
    for c in range(x.shape[1] // 128):
        xc = x[:, c * 128:(c + 1) * 128]
        hi, lo = _split_bf16(xc * xc)
        ms = (_dot(hi, bd) + _dot(lo, bd)) * (1.0 / HEAD_DIM)
        outs.append(xc * lax.rsqrt(ms + EPS))
    return jnp.concatenate(outs, axis=1) * g


def _rope(x, cos, sin, low16):
    outs = []
    for c in range(x.shape[1] // 128):
        sl = slice(c * 128, (c + 1) * 128)
        xc = x[:, sl]
        partner = jnp.where(low16, pltpu.roll(xc, 112, 1), pltpu.roll(xc, 16, 1))
        outs.append(xc * cos[:, sl] + partner * sin[:, sl])
    return jnp.concatenate(outs, axis=1)


def _a1_kernel(x_ref, m_ref, g1_ref, w_ref, qg_ref, kg_ref, bd_ref, rc_ref, rs_ref,
               q_ref, kk_ref, vv_ref, knew_ref, vnew_ref, p_ref):
    h = _modnorm(x_ref[...], g1_ref[...], m_ref[0, 0:1, :], m_ref[0, 1:2, :])
    z = _dot(h.astype(BF16), w_ref[...])
    bd = bd_ref[...]
    q = _head_norm(z[:, :ATTN_W], qg_ref[...], bd)
    k = _head_norm(z[:, ATTN_W:ATTN_W + KV_W], kg_ref[...], bd)
    v = z[:, ATTN_W + KV_W:ATTN_W + 2 * KV_W]
    knew_ref[...] = k
    vnew_ref[...] = v
    p_ref[...] = z[:, ATTN_W + 2 * KV_W:]
    cos, sin = rc_ref[...], rs_ref[...]
    lane = lax.broadcasted_iota(jnp.int32, (TM, 128), 1)
    low16 = (lane & 31) < 16
    q = _rope(q, cos, sin, low16) * (HEAD_DIM ** -0.5)
    k = _rope(k, cos, sin, low16)
    q_ref[...] = q.astype(BF16)
    kk_ref[...] = jnp.concatenate([k, pltpu.roll(k, 64, 1)], axis=1).astype(BF16)
    vv_ref[...] = jnp.concatenate([v, pltpu.roll(v, 64, 1)], axis=1).astype(BF16)


def _rope_block(t):
    return jnp.where(t < N_PROMPT_SEQ, 0, 1 + (t - N_PROMPT_SEQ) % (DEC_SEQ // TM))


def _attn_in(x, mods, g1, w_in, qg, kg, bd, rope_c, rope_s):
    n_tiles = N_TOK // TM
    tile = lambda w: pl.BlockSpec((TM, w), lambda t: (t, 0))
    const = lambda shape: pl.BlockSpec(shape, lambda t: (0,) * len(shape))
    return pl.pallas_call(
        _a1_kernel,
        grid=(n_tiles,),
        in_specs=[
            tile(D),
            pl.BlockSpec((1, 6, D), lambda t: (_mod_row(t, TM), 0, 0)),
            const((1, D)),
            const((D, A_IN_W)),
            const((1, ATTN_W)),
            const((1, KV_W)),
            const((128, 128)),
            pl.BlockSpec((TM, ATTN_W), lambda t: (_rope_block(t), 0)),
            pl.BlockSpec((TM, ATTN_W), lambda t: (_rope_block(t), 0)),
        ],
        out_specs=[tile(ATTN_W), tile(2 * KV_W), tile(2 * KV_W), tile(KV_W), tile(KV_W), tile(POOL_W)],
        out_shape=[
            jax.ShapeDtypeStruct((N_TOK, ATTN_W), BF16),
            jax.ShapeDtypeStruct((N_TOK, 2 * KV_W), BF16),
            jax.ShapeDtypeStruct((N_TOK, 2 * KV_W), BF16),
            jax.ShapeDtypeStruct((N_TOK, KV_W), F32),
            jax.ShapeDtypeStruct((N_TOK, KV_W), F32),
            jax.ShapeDtypeStruct((N_TOK, POOL_W), F32),
        ],
        compiler_params=_cparams(("arbitrary",)),
        name="attn_in_proj",
    )(x, mods, g1, w_in, qg, kg, bd, rope_c, rope_s)


def _a2_kernel(x_ref, m_ref, q_ref, kk_ref, vv_ref, p_ref, pp_ref, pn_ref, pw_ref, ps_ref,
               wo_ref, o_ref, *, tiles_per_seq):
    t = pl.program_id(0)
    q = q_ref[...]
    kk = kk_ref[0]
    vv = vv_ref[0]
    lane = lax.broadcasted_iota(jnp.int32, (TM, 128), 1)
    half_mask = (lane < HEAD_DIM, lane >= HEAD_DIM)
    zero_q = jnp.zeros((TM, 128), BF16)
    chunks = []
    for c in range(ATTN_W // 128):
        qc = q[:, c * 128:(c + 1) * 128]
        acc = None
        for half in range(2):
            kv = (2 * c + half) // (N_HEADS // N_KV)
            slab = slice(0, 128) if kv == half else slice(128, 256)
            s = _dot_nt(jnp.where(half_mask[half], qc, zero_q), kk[:, slab])
            e = jnp.exp(s - jnp.max(s, axis=-1, keepdims=True))
            l = jnp.sum(e, axis=-1, keepdims=True)
            o = _dot(e.astype(BF16), vv[:, slab]) / l
            o = jnp.where(half_mask[half], o, 0.0)
            acc = o if acc is None else acc + o
        chunks.append(acc)

    pos = t % tiles_per_seq
    has_prev = jnp.where(pos == 0, 0.0, 1.0)
    has_next = jnp.where(pos == tiles_per_seq - 1, 0.0, 1.0)
    p = p_ref[...]
    pext = jnp.concatenate([pp_ref[...] * has_prev, p, pn_ref[...] * has_next], axis=0)
    rows_ext = TM + 2 * HALO
    trow = lax.broadcasted_iota(jnp.int32, (TM, 128), 0).astype(F32)
    for g, w in enumerate(POOL_WINDOWS):
        sl = slice(g * 128, (g + 1) * 128)
        a = pext[:, sl]
        s = a + pltpu.roll(a, 1, 0)
        half = 1
        while 2 * half < w:
            s = pltpu.roll(s, half, 0) + pltpu.roll(s, rows_ext - half, 0)
            half *= 2
        cnt = (w - jnp.maximum(w // 2 - trow, 0.0) * (1.0 - has_prev)
               - jnp.maximum(trow + (w // 2 - TM), 0.0) * (1.0 - has_next))
        dev = s[HALO:HALO + TM] / cnt - p[:, sl]
        chunks.append(_dot(dev.astype(BF16), pw_ref[g]) * ps_ref[:, sl])
    mix = jnp.concatenate(chunks, axis=1).astype(BF16)
    o_ref[...] = x_ref[...] + m_ref[0, 2:3, :] * _dot(mix, wo_ref[...])


def _attn_out(x, mods, q, kk_seq, vv_seq, p, pool_w, pool_s, w_out, *, first_tile, n_tiles, tiles_per_seq):
    s_len = kk_seq.shape[1]
    off = first_tile
    n_halo_blocks = N_TOK // HALO
    per_tile = TM // HALO
    tile = lambda w: pl.BlockSpec((TM, w), lambda t: (t + off, 0))
    const = lambda shape: pl.BlockSpec(shape, lambda t: (0,) * len(shape))
    seq = pl.BlockSpec((1, s_len, 2 * KV_W), lambda t: (t // tiles_per_seq, 0, 0))
    return pl.pallas_call(
        functools.partial(_a2_kernel, tiles_per_seq=tiles_per_seq),
        grid=(n_tiles,),
        in_specs=[
            tile(D),
            pl.BlockSpec((1, 6, D), lambda t: (_mod_row(t + off, TM), 0, 0)),
            tile(ATTN_W),
            seq,
            seq,
            tile(POOL_W),
            pl.BlockSpec((HALO, POOL_W), lambda t: (jnp.maximum((t + off) * per_tile - 1, 0), 0)),
            pl.BlockSpec((HALO, POOL_W), lambda t: (jnp.minimum((t + off + 1) * per_tile, n_halo_blocks - 1), 0)),
            const((len(POOL_WINDOWS), 128, 128)),
            const((1, POOL_W)),
            const((D, D)),
        ],
        out_specs=tile(D),
        out_shape=jax.ShapeDtypeStruct((N_TOK, D), F32),
        input_output_aliases={0: 0},
        compiler_params=_cparams(("arbitrary",)),
        name="attn_pool_out_%d" % s_len,
    )(x, mods, q, kk_seq, vv_seq, p, p, p, pool_w, pool_s, w_out)


def _conv_kernel(x_ref, xp_ref, xn_ref, m_ref, g1_ref, wi_ref, cw_ref, wo_ref, o_ref):
    t = pl.program_id(0)
    is_prompt = t < N_PROMPT_SEQ
    pos = t & (DEC_SEQ // TM - 1)
    has_prev = jnp.where(is_prompt | (pos == 0), 0.0, 1.0)
    has_next = jnp.where(is_prompt | (pos == DEC_SEQ // TM - 1), 0.0, 1.0)
    x = x_ref[...]
    xext = jnp.concatenate([xp_ref[...], x, xn_ref[...]], axis=0)
    h = _modnorm(xext, g1_ref[...], m_ref[0, 0:1, :], m_ref[0, 1:2, :])
    z = _dot(h.astype(BF16), wi_ref[...])
    gb = z[HALO:HALO + TM, :D]
    u = z[:, D:2 * D] * z[:, 2 * D:]
    row = lax.broadcasted_iota(jnp.int32, (TM + 2 * HALO, 1), 0)
    u = u * jnp.where(row < HALO, has_prev, jnp.where(row >= TM + HALO, has_next, 1.0))
    rows = TM + 2 * HALO
    cw = cw_ref[...]
    y = (pltpu.roll(u, 1, 0)[HALO:HALO + TM] * cw[0:1, :]
         + u[HALO:HALO + TM] * cw[1:2, :]
         + pltpu.roll(u, rows - 1, 0)[HALO:HALO + TM] * cw[2:3, :])
    out = _dot((gb * y).astype(BF16), wo_ref[...])
    o_ref[...] = x + m_ref[0, 2:3, :] * out


def _conv_layer(x, mods, g1, w_in, conv_w, w_out):
    n_tiles = N_TOK // TM
    n_halo_blocks = N_TOK // HALO
    per_tile = TM // HALO
    const = lambda shape: pl.BlockSpec(shape, lambda t: (0,) * len(shape))
    return pl.pallas_call(
        _conv_kernel,
        grid=(n_tiles,),
        in_specs=[
            pl.BlockSpec((TM, D), lambda t: (t, 0)),
            pl.BlockSpec((HALO, D), lambda t: (jnp.maximum(t * per_tile - 1, 0), 0)),
            pl.BlockSpec((HALO, D), lambda t: (jnp.minimum((t + 1) * per_tile, n_halo_blocks - 1), 0)),
            pl.BlockSpec((1, 6, D), lambda t: (_mod_row(t, TM), 0, 0)),
            const((1, D)),
            const((D, 3 * D)),
            const((8, D)),
            const((D, D)),
        ],
        out_specs=pl.BlockSpec((TM, D), lambda t: (t, 0)),
        out_shape=jax.ShapeDtypeStruct((N_TOK, D), F32),
        compiler_params=_cparams(("arbitrary",)),
        name="conv_mixer",
    )(x, x, x, mods, g1, w_in, conv_w, w_out)


NEG_INF = float("-inf")
SUB = 8
NO_ROW = float(1 << 20)


def _sort_network(n):
    def merge(lo, hi, r):
        step = r * 2
        if step < hi - lo:
            yield from merge(lo, hi, step)
            yield from merge(lo + r, hi, step)
            yield from [(i, i + r) for i in range(lo + r, hi - r, step)]
        else:
            yield (lo, lo + r)

    def sort(lo, hi):
        if hi - lo >= 1:
            mid = lo + (hi - lo) // 2
            yield from sort(lo, mid)
            yield from sort(mid + 1, hi)
            yield from merge(lo, hi, 1)

    full = 1 << (n - 1).bit_length()
    return tuple((a, b) for a, b in sort(0, full - 1) if b < n)


def _top16_of_blocks(vs, ts, row_of, n_out=PEER_TOPK):
    vs, ts = list(vs), list(ts)
    for a, b in _sort_network(len(vs)):
        va, vb, ta, tb = vs[a], vs[b], ts[a], ts[b]
        if isinstance(ta, float) and isinstance(tb, float):
            first = (va >= vb) if ta < tb else (va > vb)
        else:
            first = (va > vb) | ((va == vb) & (ta < tb))
        vs[a], vs[b] = jnp.maximum(va, vb), jnp.minimum(va, vb)
        ts[a], ts[b] = jnp.where(first, ta, tb), jnp.where(first, tb, ta)
    vals, rows_out = [], []
    for r in range(n_out):
        rows = row_of(ts[0])
        m = jnp.max(vs[0], axis=0, keepdims=True)
        ix = jnp.min(jnp.where(vs[0] == m, rows, NO_ROW), axis=0, keepdims=True)
        vals.append(m)
        rows_out.append(ix)
        taken = rows == ix
        for k in range(min(len(vs), n_out - r - 1)):
            if k + 1 < len(vs):
                vs[k] = jnp.where(taken, vs[k + 1], vs[k])
                ts[k] = jnp.where(taken, ts[k + 1], ts[k])
            else:
                vs[k] = jnp.where(taken, NEG_INF, vs[k])
    return jnp.concatenate(vals, axis=0), jnp.concatenate(rows_out, axis=0)


def _top16_keys(s, sub):
    n = PEER_NKEYS // SUB
    return _top16_of_blocks([s[SUB * v:SUB * (v + 1), :] for v in range(n)], [float(v) for v in range(n)],
                            lambda blk: blk * float(SUB) + sub)


def _pair_candidates(sv1, sv2, sub):
    a_lo, a_hi, b_lo, b_hi = sv1[0:SUB], sv1[SUB:], sv2[0:SUB], sv2[SUB:]
    row = lambda x, r: x[r:r + 1, :]
    keep = lambda cond, v: jnp.where(cond, v, NEG_INF)
    k = float(PEER_TOPK)
    tail = keep((sub >= 2) & (sub <= 4), row(sv1, 2) + b_lo)
    tail_flat = sub + 2 * k
    for r, (a, b) in ((5, (3, 2)), (6, (4, 2)), (7, (3, 3))):
        tail = jnp.where(sub == r, row(sv1, a) + row(sv2, b), tail)
        tail_flat = jnp.where(sub == r, a * k + b, tail_flat)
    return [
        (row(sv1, 0) + b_lo, sub),
        (row(sv1, 0) + b_hi, sub + SUB),
        (keep(sub >= 1, a_lo + row(sv2, 0)), sub * k),
        (a_hi + row(sv2, 0), (sub + SUB) * k),
        (keep(sub >= 1, row(sv1, 1) + b_lo), sub + k),
        (keep(sub >= 2, a_lo + row(sv2, 1)), sub * k + 1),
        (tail, tail_flat),
    ]


def _route_kernel(x_ref, m_ref, g2_ref, wq_ref, sk_ref, hb_ref, gate_ref,
                  q_scr, i1t_scr, i2t_scr, gtt_scr, i1_scr, i2_scr, gt_scr, gscr):
    @pl.when(pl.program_id(0) == 0)
    def _():
        i1_scr[...] = jnp.zeros_like(i1_scr)
        i2_scr[...] = jnp.zeros_like(i2_scr)
        gt_scr[...] = jnp.zeros_like(gt_scr)

    key = lax.broadcasted_iota(jnp.int32, (PEER_NKEYS, 128), 0).astype(F32)

    def expand_token(n):
        row = pl.ds(n, 1)
        left = jnp.where(key == i1_scr[row, :], gt_scr[row, :], 0.0).astype(BF16)
        right = jnp.where(key == i2_scr[row, :], 1.0, 0.0).astype(BF16)
        gscr[pl.ds(n, PEER_NKEYS, stride=G_PITCH), :] = _dot_nt(left, right)

    def copy_out(i, carry):
        gate_ref[i] = gscr[pl.ds(pl.multiple_of(i * G_PITCH, 8), TR), :].astype(BF16)
        return carry

    last = pl.num_programs(0) - 1

    @pl.when(pl.program_id(0) == last)
    def _():
        hb_ref[...] = _modnorm(x_ref[...], g2_ref[...], m_ref[0, 3:4, :], m_ref[0, 4:5, :]).astype(BF16)

        def expand_group(j, carry):
            for k in range(16):
                expand_token(j * 16 + k)
            return carry

        lax.fori_loop(0, TR // 16, expand_group, 0)
        lax.fori_loop(0, PEER_NKEYS, copy_out, 0, unroll=4)

    @pl.when(pl.program_id(0) < last)
    def _():
        _route_select(x_ref, m_ref, g2_ref, wq_ref, sk_ref, hb_ref, q_scr, i1t_scr, i2t_scr, gtt_scr, expand_token)
        lax.fori_loop(0, PEER_NKEYS, copy_out, 0, unroll=4)
        i1_scr[...] = i1t_scr[...].T
        i2_scr[...] = i2t_scr[...].T
        gt_scr[...] = gtt_scr[...].T


def _route_select(x_ref, m_ref, g2_ref, wq_ref, sk_ref, hb_ref, q_scr, i1t_scr, i2t_scr, gtt_scr, expand_token):
    h = _modnorm(x_ref[...], g2_ref[...], m_ref[0, 3:4, :], m_ref[0, 4:5, :])
    hb = h.astype(BF16)
    hb_ref[...] = hb
    q = _dot(hb, wq_ref[...])
    for hp in range(2 * PEER_HEADS):
        q_scr[hp] = q[:, hp * PEER_DK:(hp + 1) * PEER_DK]
    sub = lax.broadcasted_iota(jnp.int32, (SUB, TR), 0).astype(F32)
    sk_split = [_split_bf16(sk_ref[p]) for p in range(2)]

    def key_stage(hd):
        top = ()
        for p in range(2):
            q_hi, q_lo = _split_bf16(q_scr[2 * hd + p])
            k_hi, k_lo = sk_split[p]
            s = _dot_nt(k_hi, q_hi) + (_dot_nt(k_hi, q_lo) + _dot_nt(k_lo, q_hi))
            top += _top16_keys(s, sub)
        return top

    def pair_stage(hd, top):
        sv1, si1, sv2, si2 = top
        cands = _pair_candidates(sv1, sv2, sub)
        fv, ff = _top16_of_blocks([c[0] for c in cands], [c[1] for c in cands], lambda flat: flat)
        ff = ff.astype(jnp.int32)
        fa, fb = ff >> 4, ff & (PEER_TOPK - 1)
        i1 = jnp.zeros((PEER_TOPK, TR), F32)
        i2 = jnp.zeros((PEER_TOPK, TR), F32)
        for a in range(PEER_TOPK):
            i1 = jnp.where(fa == a, si1[a:a + 1, :], i1)
            i2 = jnp.where(fb == a, si2[a:a + 1, :], i2)
        e = jnp.exp(fv - fv[0:1, :])
        g = e / jnp.sum(e, axis=0, keepdims=True)
        start = hd * PEER_TOPK
        rows = pl.ds(start if isinstance(start, int) else pl.multiple_of(start, PEER_TOPK), PEER_TOPK)
        i1t_scr[rows, :] = i1
        i2t_scr[rows, :] = i2
        gtt_scr[rows, :] = g

    n_first, n_mid = TR // 8, (TR - TR // 8) // 8
    top = key_stage(0)
    for k in range(n_first):
        expand_token(k)

    def per_head(hd, top_prev):
        top_cur = key_stage(hd)
        pair_stage(hd - 1, top_prev)
        for k in range(n_mid):
            expand_token(n_first + (hd - 1) * n_mid + k)
        return top_cur

    top = lax.fori_loop(1, PEER_HEADS, per_head, top)
    pair_stage(PEER_HEADS - 1, top)
    for n in range(n_first + (PEER_HEADS - 1) * n_mid, TR):
        expand_token(n)


def _peer_route(x, mods, g2, wq, subkeys):
    n_tiles = N_TOK // TR
    const = lambda shape: pl.BlockSpec(shape, lambda t: (0,) * len(shape))
    cur = lambda t: jnp.minimum(t, n_tiles - 1)
    return pl.pallas_call(
        _route_kernel,
        grid=(n_tiles + 1,),
        in_specs=[
            pl.BlockSpec((TR, D), lambda t: (cur(t), 0)),
            pl.BlockSpec((1, 6, D), lambda t: (_mod_row(cur(t), TR), 0, 0)),
            const((1, D)),
            const((D, PEER_HEADS * 2 * PEER_DK)),
            const((2, PEER_NKEYS, PEER_DK)),
        ],
        out_specs=[
            pl.BlockSpec((TR, D), lambda t: (cur(t), 0)),
            pl.BlockSpec((PEER_NKEYS, TR, PEER_NKEYS), lambda t: (0, jnp.maximum(t - 1, 0), 0)),
        ],
        out_shape=[
            jax.ShapeDtypeStruct((N_TOK, D), BF16),
            jax.ShapeDtypeStruct((PEER_NKEYS, N_TOK, PEER_NKEYS), BF16),
        ],
        scratch_shapes=[
            pltpu.VMEM((2 * PEER_HEADS, TR, PEER_DK), F32),
            pltpu.VMEM((PEER_HEADS * PEER_TOPK, TR), F32),
            pltpu.VMEM((PEER_HEADS * PEER_TOPK, TR), F32),
            pltpu.VMEM((PEER_HEADS * PEER_TOPK, TR), F32),
            pltpu.VMEM((TR, PEER_HEADS * PEER_TOPK), F32),
            pltpu.VMEM((TR, PEER_HEADS * PEER_TOPK), F32),
            pltpu.VMEM((TR, PEER_HEADS * PEER_TOPK), F32),
            pltpu.VMEM((PEER_NKEYS * G_PITCH, PEER_NKEYS), F32),
        ],
        compiler_params=_cparams(("arbitrary",)),
        name="peer_route",
    )(x, mods, g2, wq, subkeys)


def _expert_kernel(x_ref, m_ref, hb_ref, gate_ref, u_ref, v_ref, *rest):
    o_ref = rest[-1]
    c = pl.program_id(1)

    @pl.when(c == 0)
    def _():
        o_ref[...] = jnp.zeros_like(o_ref)

    hb = hb_ref[...]
    slabs = []
    for s in range(EC // EC_SUB):
        pre = _dot_nt(hb, u_ref[0, s * EC_SUB:(s + 1) * EC_SUB, :].astype(BF16))
        act = 0.5 * pre * (1.0 + lax.erf(pre * (2.0 ** -0.5)))
        for a in range(EC_SUB // PEER_NKEYS):
            gate = gate_ref[s * (EC_SUB // PEER_NKEYS) + a].astype(F32)
            slabs.append((act[:, a * PEER_NKEYS:(a + 1) * PEER_NKEYS] * gate).astype(BF16))
    o_ref[...] += _dot(jnp.concatenate(slabs, axis=1), v_ref[0].astype(BF16))

    @pl.when(c == pl.num_programs(1) - 1)
    def _():
        y = x_ref[...] + m_ref[0, 5:6, :] * o_ref[...]
        if len(rest) == 2:
            y = y * lax.rsqrt(jnp.mean(y * y, axis=-1, keepdims=True) + EPS) * rest[0][...]
        o_ref[...] = y


def _peer_experts(x, mods, hb, gate, u_all, v_all, layer, final_g=None):
    n_tiles = N_TOK // TE
    n_chunks = PEER_EXPERTS // EC
    once = pl.Buffered(1)
    in_specs = [
        pl.BlockSpec((TE, D), lambda t, c: (t, 0), pipeline_mode=once),
        pl.BlockSpec((1, 6, D), lambda t, c: (_mod_row(t, TE), 0, 0)),
        pl.BlockSpec((TE, D), lambda t, c: (t, 0), pipeline_mode=once),
        pl.BlockSpec((EC // PEER_NKEYS, TE, PEER_NKEYS), lambda t, c: (c, t, 0)),
        pl.BlockSpec((1, EC, D), lambda t, c: (layer, c, 0)),
        pl.BlockSpec((1, EC, D), lambda t, c: (layer, c, 0)),
    ]
    operands = [x, mods, hb, gate, u_all, v_all]
    if final_g is not None:
        in_specs.append(pl.BlockSpec((1, D), lambda t, c: (0, 0)))
        operands.append(final_g)
    return pl.pallas_call(
        _expert_kernel,
        grid=(n_tiles, n_chunks),
        in_specs=in_specs,
        out_specs=pl.BlockSpec((TE, D), lambda t, c: (t, 0)),
        out_shape=jax.ShapeDtypeStruct((N_TOK, D), F32),
        compiler_params=_cparams(("arbitrary", "arbitrary")),
        name="peer_experts",
    )(*operands)


def _rope_tables():
    t = jnp.arange(DEC_SEQ)
    row, col = t // GRID_W, t % GRID_W
    nf = HEAD_DIM // 4
    inv = ROPE_THETA ** (-jnp.arange(nf, dtype=F32) / nf)
    ang_r = row.astype(F32)[:, None] * inv[None, :]
    ang_c = col.astype(F32)[:, None] * inv[None, :]
    cos = jnp.concatenate([jnp.cos(ang_r)] * 2 + [jnp.cos(ang_c)] * 2, axis=1)
    sin = jnp.concatenate([-jnp.sin(ang_r), jnp.sin(ang_r), -jnp.sin(ang_c), jnp.sin(ang_c)], axis=1)
    cos = jnp.concatenate([jnp.ones((TM, HEAD_DIM), F32), cos], axis=0)
    sin = jnp.concatenate([jnp.zeros((TM, HEAD_DIM), F32), sin], axis=0)
    return jnp.tile(cos, (1, N_HEADS)), jnp.tile(sin, (1, N_HEADS))


def _head_block_diag():
    i = np.arange(128)
    return jnp.asarray((i[:, None] // HEAD_DIM == i[None, :] // HEAD_DIM).astype(np.float32), dtype=BF16)


def kernel(x_prompt, x_sample, cache_k, cache_v, c, c_ctx, mod_w, mod_b, norm1_g, norm2_g, a_in_w, a_out_w,
           q_norm_g, k_norm_g, pool_w, pool_scale, c_in_w, c_conv_w, c_out_w, peer_wq, peer_subkeys, peer_u,
           peer_v, final_g):
    x = jnp.concatenate([x_prompt.reshape(NP_TOK, D), x_sample.reshape(NS_TOK, D)], axis=0)
    ct = jnp.zeros((D, 8), F32).at[:, 0].set(c_ctx).at[:, 1:3].set(c.T)
    mods = _modulation(ct, mod_w, mod_b).reshape(DEPTH, 8, 6, D)
    rope_c, rope_s = _rope_tables()
    bd = _head_block_diag()
    wq = peer_wq.astype(BF16)
    new_k, new_v = [], []
    for l in range(DEPTH):
        g1 = norm1_g[l].reshape(1, D)
        if l % 2 == 0:
            i = l // 2
            q, kk, vv, knew, vnew, p = _attn_in(
                x, mods[l], g1, a_in_w[i].astype(BF16), jnp.tile(q_norm_g[i], N_HEADS).reshape(1, ATTN_W),
                jnp.tile(k_norm_g[i], N_KV).reshape(1, KV_W), bd, rope_c, rope_s)
            new_k.append(knew[:NP_TOK].reshape(N_PROMPT_SEQ, SEQ, N_KV, HEAD_DIM))
            new_v.append(vnew[:NP_TOK].reshape(N_PROMPT_SEQ, SEQ, N_KV, HEAD_DIM))
            ck = cache_k[:, i]
            cv = cache_v[:, i]
            ctx_kk = jnp.concatenate([ck, ck[:, :, ::-1]], axis=2).reshape(N_SAMPLE_SEQ, PAST, 2 * KV_W).astype(BF16)
            ctx_vv = jnp.concatenate([cv, cv[:, :, ::-1]], axis=2).reshape(N_SAMPLE_SEQ, PAST, 2 * KV_W).astype(BF16)
            kk_p = kk[:NP_TOK].reshape(N_PROMPT_SEQ, SEQ, 2 * KV_W)
            vv_p = vv[:NP_TOK].reshape(N_PROMPT_SEQ, SEQ, 2 * KV_W)
            kk_s = jnp.concatenate([kk[NP_TOK:].reshape(N_SAMPLE_SEQ, DEC_SEQ, 2 * KV_W), ctx_kk], axis=1)
            vv_s = jnp.concatenate([vv[NP_TOK:].reshape(N_SAMPLE_SEQ, DEC_SEQ, 2 * KV_W), ctx_vv], axis=1)
            common = (pool_w[i].astype(BF16), pool_scale[i].reshape(1, POOL_W), a_out_w[i].astype(BF16))
            x = _attn_out(x, mods[l], q, kk_p, vv_p, p, *common,
                          first_tile=0, n_tiles=N_PROMPT_SEQ, tiles_per_seq=SEQ // TM)
            x = _attn_out(x, mods[l], q, kk_s, vv_s, p, *common,
                          first_tile=N_PROMPT_SEQ, n_tiles=NS_TOK // TM, tiles_per_seq=DEC_SEQ // TM)
        else:
            j = l // 2
            cw = jnp.zeros((8, D), F32).at[:3].set(c_conv_w[j])
            x = _conv_layer(x, mods[l], g1, c_in_w[j].astype(BF16), cw, c_out_w[j].astype(BF16))
        hb, gate = _peer_route(x, mods[l], norm2_g[l].reshape(1, D), wq[l], peer_subkeys[l])
        x = _peer_experts(x, mods[l], hb, gate, peer_u, peer_v, l,
                          final_g=final_g.reshape(1, D) if l == DEPTH - 1 else None)
    y = x
    y_prompt = y[:NP_TOK].reshape(N_PROMPT_SEQ, SEQ, D)
    y_sample = y[NP_TOK:].reshape(N_SAMPLE_SEQ, DEC_SEQ, D)
    return (y_prompt, y_sample, jnp.stack(new_k, axis=1), jnp.stack(new_v, axis=1))
```

```python
import functools

import numpy as np
import jax
import jax.numpy as jnp
from jax import lax
from jax.experimental import pallas as pl
from jax.experimental.pallas import tpu as pltpu

F32 = jnp.float32
BF16 = jnp.bfloat16

D = 1024
N_PROMPT_SEQ, SEQ = 32, 256
N_SAMPLE_SEQ, DEC_SEQ = 2, 2048
PAST = 256
NP_TOK = N_PROMPT_SEQ * SEQ
NS_TOK = N_SAMPLE_SEQ * DEC_SEQ
N_TOK = NP_TOK + NS_TOK
DEPTH = 4
GRID_W = 64
N_HEADS, N_KV, HEAD_DIM = 8, 2, 64
ATTN_W, KV_W, POOL_W = 512, 128, 512
POOL_WINDOWS = (2, 4, 8, 16)
A_IN_W = ATTN_W + 2 * KV_W + POOL_W
PEER_HEADS, PEER_NKEYS, PEER_TOPK, PEER_DK = 8, 128, 16, 128
PEER_EXPERTS = PEER_NKEYS * PEER_NKEYS
ROPE_THETA = 10000.0
EPS = 1e-6

TM = 256
HALO = 8
TR = 256
G_PITCH = TR + 8
TE = 1024
EC = 1024
EC_SUB = 512
VMEM_LIMIT = 56 * 1024 * 1024


def _cparams(sem):
    return pltpu.CompilerParams(dimension_semantics=sem, vmem_limit_bytes=VMEM_LIMIT)


def _mod_row(tile, tile_tokens):
    start = tile * tile_tokens
    return (start >= NP_TOK).astype(jnp.int32) + (start >= NP_TOK + DEC_SEQ).astype(jnp.int32)


def _modnorm(x, g, shift, scale):
    ms = jnp.mean(x * x, axis=-1, keepdims=True)
    y = x * lax.rsqrt(ms + EPS) * g
    return y * (1.0 + scale) + shift


def _split_bf16(a):
    hi = a.astype(BF16)
    lo = (a - hi.astype(F32)).astype(BF16)
    return hi, lo


def _dot(a, b):
    return jnp.dot(a, b, preferred_element_type=F32)


def _dot_nt(a, b):
    return lax.dot_general(a, b, (((1,), (1,)), ((), ())), preferred_element_type=F32)


def _mod_kernel(ct_ref, w_ref, b_ref, o_ref):
    ct = ct_ref[...]
    s = ct * jax.nn.sigmoid(ct)
    w = w_ref[0]
    rows = [jnp.sum(s[:, r:r + 1] * w, axis=0, keepdims=True) + b_ref[0] for r in range(3)]
    rows.append(jnp.zeros((5, w.shape[1]), F32))
    o_ref[0] = jnp.concatenate(rows, axis=0)


def _modulation(ct, mod_w, mod_b):
    tn = 1536
    return pl.pallas_call(
        _mod_kernel,
        grid=(DEPTH, 6 * D // tn),
        in_specs=[
            pl.BlockSpec((D, 8), lambda l, j: (0, 0)),
            pl.BlockSpec((1, D, tn), lambda l, j: (l, 0, j)),
            pl.BlockSpec((1, 1, tn), lambda l, j: (l, 0, j)),
        ],
        out_specs=pl.BlockSpec((1, 8, tn), lambda l, j: (l, 0, j)),
        out_shape=jax.ShapeDtypeStruct((DEPTH, 8, 6 * D), F32),
        compiler_params=_cparams(("arbitrary", "arbitrary")),
        name="adaln_modulation",
    )(ct, mod_w, mod_b.reshape(DEPTH, 1, 6 * D))


def _head_norm(x, g, bd):
    outs = []
    for c in range(x.shape[1] // 128):
        xc = x[:, c * 128:(c + 1) * 128]
        hi, lo = _split_bf16(xc * xc)
        ms = (_dot(hi, bd) + _dot(lo, bd)) * (1.0 / HEAD_DIM)
        outs.append(xc * lax.rsqrt(ms + EPS))
    return jnp.concatenate(outs, axis=1) * g


def _rope(x, cos, sin, low16):
    outs = []
    for c in range(x.shape[1] // 128):
        sl = slice(c * 128, (c + 1) * 128)
        xc = x[:, sl]
        partner = jnp.where(low16, pltpu.roll(xc, 112, 1), pltpu.roll(xc, 16, 1))
        outs.append(xc * cos[:, sl] + partner * sin[:, sl])
    return jnp.concatenate(outs, axis=1)


def _a1_kernel(x_ref, m_ref, g1_ref, w_ref, qg_ref, kg_ref, bd_ref, rc_ref, rs_ref,
               q_ref, kk_ref, vv_ref, knew_ref, vnew_ref, p_ref):
    h = _modnorm(x_ref[...], g1_ref[...], m_ref[0, 0:1, :], m_ref[0, 1:2, :])
    z = _dot(h.astype(BF16), w_ref[...])
    bd = bd_ref[...]
    q = _head_norm(z[:, :ATTN_W], qg_ref[...], bd)
    k = _head_norm(z[:, ATTN_W:ATTN_W + KV_W], kg_ref[...], bd)
    v = z[:, ATTN_W + KV_W:ATTN_W + 2 * KV_W]
    knew_ref[...] = k
    vnew_ref[...] = v
    p_ref[...] = z[:, ATTN_W + 2 * KV_W:]
    cos, sin = rc_ref[...], rs_ref[...]
    lane = lax.broadcasted_iota(jnp.int32, (TM, 128), 1)
    low16 = (lane & 31) < 16
    q = _rope(q, cos, sin, low16) * (HEAD_DIM ** -0.5)
    k = _rope(k, cos, sin, low16)
    q_ref[...] = q.astype(BF16)
    kk_ref[...] = jnp.concatenate([k, pltpu.roll(k, 64, 1)], axis=1).astype(BF16)
    vv_ref[...] = jnp.concatenate([v, pltpu.roll(v, 64, 1)], axis=1).astype(BF16)


def _rope_block(t):
    return jnp.where(t < N_PROMPT_SEQ, 0, 1 + (t - N_PROMPT_SEQ) % (DEC_SEQ // TM))


def _attn_in(x, mods, g1, w_in, qg, kg, bd, rope_c, rope_s):
    n_tiles = N_TOK // TM
    tile = lambda w: pl.BlockSpec((TM, w), lambda t: (t, 0))
    const = lambda shape: pl.BlockSpec(shape, lambda t: (0,) * len(shape))
    return pl.pallas_call(
        _a1_kernel,
        grid=(n_tiles,),
        in_specs=[
            tile(D),
            pl.BlockSpec((1, 6, D), lambda t: (_mod_row(t, TM), 0, 0)),
            const((1, D)),
            const((D, A_IN_W)),
            const((1, ATTN_W)),
            const((1, KV_W)),
            const((128, 128)),
            pl.BlockSpec((TM, ATTN_W), lambda t: (_rope_block(t), 0)),
            pl.BlockSpec((TM, ATTN_W), lambda t: (_rope_block(t), 0)),
        ],
        out_specs=[tile(ATTN_W), tile(2 * KV_W), tile(2 * KV_W), tile(KV_W), tile(KV_W), tile(POOL_W)],
        out_shape=[
            jax.ShapeDtypeStruct((N_TOK, ATTN_W), BF16),
            jax.ShapeDtypeStruct((N_TOK, 2 * KV_W), BF16),
            jax.ShapeDtypeStruct((N_TOK, 2 * KV_W), BF16),
            jax.ShapeDtypeStruct((N_TOK, KV_W), F32),
            jax.ShapeDtypeStruct((N_TOK, KV_W), F32),
            jax.ShapeDtypeStruct((N_TOK, POOL_W), F32),
        ],
        compiler_params=_cparams(("arbitrary",)),
        name="attn_in_proj",
    )(x, mods, g1, w_in, qg, kg, bd, rope_c, rope_s)


def _a2_kernel(x_ref, m_ref, q_ref, kk_ref, vv_ref, p_ref, pp_ref, pn_ref, pw_ref, ps_ref,
               wo_ref, o_ref, *, tiles_per_seq):
    t = pl.program_id(0)
    q = q_ref[...]
    kk = kk_ref[0]
    vv = vv_ref[0]
    lane = lax.broadcasted_iota(jnp.int32, (TM, 128), 1)
    half_mask = (lane < HEAD_DIM, lane >= HEAD_DIM)
    zero_q = jnp.zeros((TM, 128), BF16)
    chunks = []
    for c in range(ATTN_W // 128):
        qc = q[:, c * 128:(c + 1) * 128]
        acc = None
        for half in range(2):
            kv = (2 * c + half) // (N_HEADS // N_KV)
            slab = slice(0, 128) if kv == half else slice(128, 256)
            s = _dot_nt(jnp.where(half_mask[half], qc, zero_q), kk[:, slab])
            e = jnp.exp(s - jnp.max(s, axis=-1, keepdims=True))
            l = jnp.sum(e, axis=-1, keepdims=True)
            o = _dot(e.astype(BF16), vv[:, slab]) / l
            o = jnp.where(half_mask[half], o, 0.0)
            acc = o if acc is None else acc + o
        chunks.append(acc)

    pos = t % tiles_per_seq
    has_prev = jnp.where(pos == 0, 0.0, 1.0)
    has_next = jnp.where(pos == tiles_per_seq - 1, 0.0, 1.0)
    p = p_ref[...]
    pext = jnp.concatenate([pp_ref[...] * has_prev, p, pn_ref[...] * has_next], axis=0)
    rows_ext = TM + 2 * HALO
    trow = lax.broadcasted_iota(jnp.int32, (TM, 128), 0).astype(F32)
    for g, w in enumerate(POOL_WINDOWS):
        sl = slice(g * 128, (g + 1) * 128)
        a = pext[:, sl]
        s = a + pltpu.roll(a, 1, 0)
        half = 1
        while 2 * half < w:
            s = pltpu.roll(s, half, 0) + pltpu.roll(s, rows_ext - half, 0)
            half *= 2
        cnt = (w - jnp.maximum(w // 2 - trow, 0.0) * (1.0 - has_prev)
               - jnp.maximum(trow + (w // 2 - TM), 0.0) * (1.0 - has_next))
        dev = s[HALO:HALO + TM] / cnt - p[:, sl]
        chunks.append(_dot(dev.astype(BF16), pw_ref[g]) * ps_ref[:, sl])
    mix = jnp.concatenate(chunks, axis=1).astype(BF16)
    o_ref[...] = x_ref[...] + m_ref[0, 2:3, :] * _dot(mix, wo_ref[...])


def _attn_out(x, mods, q, kk_seq, vv_seq, p, pool_w, pool_s, w_out, *, first_tile, n_tiles, tiles_per_seq):
    s_len = kk_seq.shape[1]
    off = first_tile
    n_halo_blocks = N_TOK // HALO
    per_tile = TM // HALO
    tile = lambda w: pl.BlockSpec((TM, w), lambda t: (t + off, 0))
    const = lambda shape: pl.BlockSpec(shape, lambda t: (0,) * len(shape))
    seq = pl.BlockSpec((1, s_len, 2 * KV_W), lambda t: (t // tiles_per_seq, 0, 0))
    return pl.pallas_call(
        functools.partial(_a2_kernel, tiles_per_seq=tiles_per_seq),
        grid=(n_tiles,),
        in_specs=[
            tile(D),
            pl.BlockSpec((1, 6, D), lambda t: (_mod_row(t + off, TM), 0, 0)),
            tile(ATTN_W),
            seq,
            seq,
            tile(POOL_W),
            pl.BlockSpec((HALO, POOL_W), lambda t: (jnp.maximum((t + off) * per_tile - 1, 0), 0)),
            pl.BlockSpec((HALO, POOL_W), lambda t: (jnp.minimum((t + off + 1) * per_tile, n_halo_blocks - 1), 0)),
            const((len(POOL_WINDOWS), 128, 128)),
            const((1, POOL_W)),
            const((D, D)),
        ],
        out_specs=tile(D),
        out_shape=jax.ShapeDtypeStruct((N_TOK, D), F32),
        input_output_aliases={0: 0},
        compiler_params=_cparams(("arbitrary",)),
        name="attn_pool_out_%d" % s_len,
    )(x, mods, q, kk_seq, vv_seq, p, p, p, pool_w, pool_s, w_out)


def _conv_kernel(x_ref, xp_ref, xn_ref, m_ref, g1_ref, wi_ref, cw_ref, wo_ref, o_ref):
    t = pl.program_id(0)
    is_prompt = t < N_PROMPT_SEQ
    pos = t & (DEC_SEQ // TM - 1)
    has_prev = jnp.where(is_prompt | (pos == 0), 0.0, 1.0)
    has_next = jnp.where(is_prompt | (pos == DEC_SEQ // TM - 1), 0.0, 1.0)
    x = x_ref[...]
    xext = jnp.concatenate([xp_ref[...], x, xn_ref[...]], axis=0)
    h = _modnorm(xext, g1_ref[...], m_ref[0, 0:1, :], m_ref[0, 1:2, :])
    z = _dot(h.astype(BF16), wi_ref[...])
    gb = z[HALO:HALO + TM, :D]
    u = z[:, D:2 * D] * z[:, 2 * D:]
    row = lax.broadcasted_iota(jnp.int32, (TM + 2 * HALO, 1), 0)
    u = u * jnp.where(row < HALO, has_prev, jnp.where(row >= TM + HALO, has_next, 1.0))
    rows = TM + 2 * HALO
    cw = cw_ref[...]
    y = (pltpu.roll(u, 1, 0)[HALO:HALO + TM] * cw[0:1, :]
         + u[HALO:HALO + TM] * cw[1:2, :]
         + pltpu.roll(u, rows - 1, 0)[HALO:HALO + TM] * cw[2:3, :])
    out = _dot((gb * y).astype(BF16), wo_ref[...])
    o_ref[...] = x + m_ref[0, 2:3, :] * out


def _conv_layer(x, mods, g1, w_in, conv_w, w_out):
    n_tiles = N_TOK // TM
    n_halo_blocks = N_TOK // HALO
    per_tile = TM // HALO
    const = lambda shape: pl.BlockSpec(shape, lambda t: (0,) * len(shape))
    return pl.pallas_call(
        _conv_kernel,
        grid=(n_tiles,),
        in_specs=[
            pl.BlockSpec((TM, D), lambda t: (t, 0)),
            pl.BlockSpec((HALO, D), lambda t: (jnp.maximum(t * per_tile - 1, 0), 0)),
            pl.BlockSpec((HALO, D), lambda t: (jnp.minimum((t + 1) * per_tile, n_halo_blocks - 1), 0)),
            pl.BlockSpec((1, 6, D), lambda t: (_mod_row(t, TM), 0, 0)),
            const((1, D)),
            const((D, 3 * D)),
            const((8, D)),
            const((D, D)),
        ],
        out_specs=pl.BlockSpec((TM, D), lambda t: (t, 0)),
        out_shape=jax.ShapeDtypeStruct((N_TOK, D), F32),
        compiler_params=_cparams(("arbitrary",)),
        name="conv_mixer",
    )(x, x, x, mods, g1, w_in, conv_w, w_out)


NEG_INF = float("-inf")
SUB = 8
NO_ROW = float(1 << 20)


def _sort_network(n):
    def merge(lo, hi, r):
        step = r * 2
        if step < hi - lo:
            yield from merge(lo, hi, step)
            yield from merge(lo + r, hi, step)
            yield from [(i, i + r) for i in range(lo + r, hi - r, step)]
        else:
            yield (lo, lo + r)

    def sort(lo, hi):
        if hi - lo >= 1:
            mid = lo + (hi - lo) // 2
            yield from sort(lo, mid)
            yield from sort(mid + 1, hi)
            yield from merge(lo, hi, 1)

    full = 1 << (n - 1).bit_length()
    return tuple((a, b) for a, b in sort(0, full - 1) if b < n)


def _top16_of_blocks(vs, ts, row_of, n_out=PEER_TOPK):
    vs, ts = list(vs), list(ts)
    for a, b in _sort_network(len(vs)):
        va, vb, ta, tb = vs[a], vs[b], ts[a], ts[b]
        if isinstance(ta, float) and isinstance(tb, float):
            first = (va >= vb) if ta < tb else (va > vb)
        else:
            first = (va > vb) | ((va == vb) & (ta < tb))
        vs[a], vs[b] = jnp.maximum(va, vb), jnp.minimum(va, vb)
        ts[a], ts[b] = jnp.where(first, ta, tb), jnp.where(first, tb, ta)
    vals, rows_out = [], []
    for r in range(n_out):
        rows = row_of(ts[0])
        m = jnp.max(vs[0], axis=0, keepdims=True)
        ix = jnp.min(jnp.where(vs[0] == m, rows, NO_ROW), axis=0, keepdims=True)
        vals.append(m)
        rows_out.append(ix)
        taken = rows == ix
        for k in range(min(len(vs), n_out - r - 1)):
            if k + 1 < len(vs):
                vs[k] = jnp.where(taken, vs[k + 1], vs[k])
                ts[k] = jnp.where(taken, ts[k + 1], ts[k])
            else:
                vs[k] = jnp.where(taken, NEG_INF, vs[k])
    return jnp.concatenate(vals, axis=0), jnp.concatenate(rows_out, axis=0)


def _top16_keys(s, sub):
    n = PEER_NKEYS // SUB
    return _top16_of_blocks([s[SUB * v:SUB * (v + 1), :] for v in range(n)], [float(v) for v in range(n)],
                            lambda blk: blk * float(SUB) + sub)


def _pair_candidates(sv1, sv2, sub):
    a_lo, a_hi, b_lo, b_hi = sv1[0:SUB], sv1[SUB:], sv2[0:SUB], sv2[SUB:]
    row = lambda x, r: x[r:r + 1, :]
    keep = lambda cond, v: jnp.where(cond, v, NEG_INF)
    k = float(PEER_TOPK)
    tail = keep((sub >= 2) & (sub <= 4), row(sv1, 2) + b_lo)
    tail_flat = sub + 2 * k
    for r, (a, b) in ((5, (3, 2)), (6, (4, 2)), (7, (3, 3))):
        tail = jnp.where(sub == r, row(sv1, a) + row(sv2, b), tail)
        tail_flat = jnp.where(sub == r, a * k + b, tail_flat)
    return [
        (row(sv1, 0) + b_lo, sub),
        (row(sv1, 0) + b_hi, sub + SUB),
        (keep(sub >= 1, a_lo + row(sv2, 0)), sub * k),
        (a_hi + row(sv2, 0), (sub + SUB) * k),
        (keep(sub >= 1, row(sv1, 1) + b_lo), sub + k),
        (keep(sub >= 2, a_lo + row(sv2, 1)), sub * k + 1),
        (tail, tail_flat),
    ]


def _route_kernel(x_ref, m_ref, g2_ref, wq_ref, sk_ref, hb_ref, gate_ref,
                  q_scr, i1t_scr, i2t_scr, gtt_scr, i1_scr, i2_scr, gt_scr, gscr):
    @pl.when(pl.program_id(0) == 0)
    def _():
        i1_scr[...] = jnp.zeros_like(i1_scr)
        i2_scr[...] = jnp.zeros_like(i2_scr)
        gt_scr[...] = jnp.zeros_like(gt_scr)

    key = lax.broadcasted_iota(jnp.int32, (PEER_NKEYS, 128), 0).astype(F32)

    def expand_token(n):
        row = pl.ds(n, 1)
        left = jnp.where(key == i1_scr[row, :], gt_scr[row, :], 0.0).astype(BF16)
        right = jnp.where(key == i2_scr[row, :], 1.0, 0.0).astype(BF16)
        gscr[pl.ds(n, PEER_NKEYS, stride=G_PITCH), :] = _dot_nt(left, right)

    def copy_out(i, carry):
        gate_ref[i] = gscr[pl.ds(pl.multiple_of(i * G_PITCH, 8), TR), :].astype(BF16)
        return carry

    last = pl.num_programs(0) - 1

    @pl.when(pl.program_id(0) == last)
    def _():
        hb_ref[...] = _modnorm(x_ref[...], g2_ref[...], m_ref[0, 3:4, :], m_ref[0, 4:5, :]).astype(BF16)

        def expand_group(j, carry):
            for k in range(16):
                expand_token(j * 16 + k)
            return carry

        lax.fori_loop(0, TR // 16, expand_group, 0)
        lax.fori_loop(0, PEER_NKEYS, copy_out, 0, unroll=4)

    @pl.when(pl.program_id(0) < last)
    def _():
        _route_select(x_ref, m_ref, g2_ref, wq_ref, sk_ref, hb_ref, q_scr, i1t_scr, i2t_scr, gtt_scr, expand_token)
        lax.fori_loop(0, PEER_NKEYS, copy_out, 0, unroll=4)
        i1_scr[...] = i1t_scr[...].T
        i2_scr[...] = i2t_scr[...].T
        gt_scr[...] = gtt_scr[...].T


def _route_select(x_ref, m_ref, g2_ref, wq_ref, sk_ref, hb_ref, q_scr, i1t_scr, i2t_scr, gtt_scr, expand_token):
    h = _modnorm(x_ref[...], g2_ref[...], m_ref[0, 3:4, :], m_ref[0, 4:5, :])
    hb = h.astype(BF16)
    hb_ref[...] = hb
    q = _dot(hb, wq_ref[...])
    for hp in range(2 * PEER_HEADS):
        q_scr[hp] = q[:, hp * PEER_DK:(hp + 1) * PEER_DK]
    sub = lax.broadcasted_iota(jnp.int32, (SUB, TR), 0).astype(F32)
    sk_split = [_split_bf16(sk_ref[p]) for p in range(2)]

    def key_stage(hd):
        top = ()
        for p in range(2):
            q_hi, q_lo = _split_bf16(q_scr[2 * hd + p])
            k_hi, k_lo = sk_split[p]
            s = _dot_nt(k_hi, q_hi) + (_dot_nt(k_hi, q_lo) + _dot_nt(k_lo, q_hi))
            top += _top16_keys(s, sub)
        return top

    def pair_stage(hd, top):
        sv1, si1, sv2, si2 = top
        cands = _pair_candidates(sv1, sv2, sub)
        fv, ff = _top16_of_blocks([c[0] for c in cands], [c[1] for c in cands], lambda flat: flat)
        ff = ff.astype(jnp.int32)
        fa, fb = ff >> 4, ff & (PEER_TOPK - 1)
        i1 = jnp.zeros((PEER_TOPK, TR), F32)
        i2 = jnp.zeros((PEER_TOPK, TR), F32)
        for a in range(PEER_TOPK):
            i1 = jnp.where(fa == a, si1[a:a + 1, :], i1)
            i2 = jnp.where(fb == a, si2[a:a + 1, :], i2)
        e = jnp.exp(fv - fv[0:1, :])
        g = e / jnp.sum(e, axis=0, keepdims=True)
        start = hd * PEER_TOPK
        rows = pl.ds(start if isinstance(start, int) else pl.multiple_of(start, PEER_TOPK), PEER_TOPK)
        i1t_scr[rows, :] = i1
        i2t_scr[rows, :] = i2
        gtt_scr[rows, :] = g

    n_first, n_mid = TR // 8, (TR - TR // 8) // 8
    top = key_stage(0)
    for k in range(n_first):
        expand_token(k)

    def per_head(hd, top_prev):
        top_cur = key_stage(hd)
        pair_stage(hd - 1, top_prev)
        for k in range(n_mid):
            expand_token(n_first + (hd - 1) * n_mid + k)
        return top_cur

    top = lax.fori_loop(1, PEER_HEADS, per_head, top)
    pair_stage(PEER_HEADS - 1, top)
    for n in range(n_first + (PEER_HEADS - 1) * n_mid, TR):
        expand_token(n)


def _peer_route(x, mods, g2, wq, subkeys):
    n_tiles = N_TOK // TR
    const = lambda shape: pl.BlockSpec(shape, lambda t: (0,) * len(shape))
    cur = lambda t: jnp.minimum(t, n_tiles - 1)
    return pl.pallas_call(
        _route_kernel,
        grid=(n_tiles + 1,),
        in_specs=[
            pl.BlockSpec((TR, D), lambda t: (cur(t), 0)),
            pl.BlockSpec((1, 6, D), lambda t: (_mod_row(cur(t), TR), 0, 0)),
            const((1, D)),
            const((D, PEER_HEADS * 2 * PEER_DK)),
            const((2, PEER_NKEYS, PEER_DK)),
        ],
        out_specs=[
            pl.BlockSpec((TR, D), lambda t: (cur(t), 0)),
            pl.BlockSpec((PEER_NKEYS, TR, PEER_NKEYS), lambda t: (0, jnp.maximum(t - 1, 0), 0)),
        ],
        out_shape=[
            jax.ShapeDtypeStruct((N_TOK, D), BF16),
            jax.ShapeDtypeStruct((PEER_NKEYS, N_TOK, PEER_NKEYS), BF16),
        ],
        scratch_shapes=[
            pltpu.VMEM((2 * PEER_HEADS, TR, PEER_DK), F32),
            pltpu.VMEM((PEER_HEADS * PEER_TOPK, TR), F32),
            pltpu.VMEM((PEER_HEADS * PEER_TOPK, TR), F32),
            pltpu.VMEM((PEER_HEADS * PEER_TOPK, TR), F32),
            pltpu.VMEM((TR, PEER_HEADS * PEER_TOPK), F32),
            pltpu.VMEM((TR, PEER_HEADS * PEER_TOPK), F32),
            pltpu.VMEM((TR, PEER_HEADS * PEER_TOPK), F32),
            pltpu.VMEM((PEER_NKEYS * G_PITCH, PEER_NKEYS), F32),
        ],
        compiler_params=_cparams(("arbitrary",)),
        name="peer_route",
    )(x, mods, g2, wq, subkeys)


F8 = jnp.float8_e4m3fn
F8_TOP = 256.0
TINY = 1e-30


def _row_scale(a):
    return jnp.maximum(jnp.max(jnp.abs(a), axis=1, keepdims=True), TINY) * (1.0 / F8_TOP)


def _quant_rows_kernel(w_ref, q_ref, s_ref):
    w = w_ref[0]
    scale = _row_scale(w)
    q_ref[0] = (w * (1.0 / scale)).astype(F8)
    s_ref[0, 0] = jnp.transpose(jnp.broadcast_to(scale, (EC, 128)))[0:1, :]


def _quantize_rows(w_all):
    n_chunks = PEER_EXPERTS // EC
    return pl.pallas_call(
        _quant_rows_kernel,
        grid=(DEPTH, n_chunks),
        in_specs=[pl.BlockSpec((1, EC, D), lambda l, j: (l, j, 0))],
        out_specs=[pl.BlockSpec((1, EC, D), lambda l, j: (l, j, 0)),
                   pl.BlockSpec((1, 1, 1, EC), lambda l, j: (l, j, 0, 0))],
        out_shape=[jax.ShapeDtypeStruct((DEPTH, PEER_EXPERTS, D), F8),
                   jax.ShapeDtypeStruct((DEPTH, n_chunks, 1, EC), F32)],
        compiler_params=_cparams(("arbitrary", "arbitrary")),
        name="quantize_rows",
    )(w_all)


def _expert_kernel(x_ref, m_ref, hb_ref, gate_ref, u8_ref, su_ref, v_ref, *rest):
    o_ref, h8_scr, t_scr = rest[-3:]
    c = pl.program_id(1)

    @pl.when(c == 0)
    def _():
        o_ref[...] = jnp.zeros_like(o_ref)
        hf = hb_ref[...].astype(F32)
        t = _row_scale(hf)
        h8_scr[...] = (hf * (1.0 / t)).astype(F8)
        t_scr[...] = t

    h8 = h8_scr[...]
    t = t_scr[...]
    slabs = []
    for s in range(EC // EC_SUB):
        cols = slice(s * EC_SUB, (s + 1) * EC_SUB)
        pre = _dot_nt(h8, u8_ref[0, cols, :]) * t * su_ref[0, 0, :, cols]
        act = 0.5 * pre * (1.0 + lax.erf(pre * (2.0 ** -0.5)))
        for a in range(EC_SUB // PEER_NKEYS):
            gate = gate_ref[s * (EC_SUB // PEER_NKEYS) + a].astype(F32)
            slabs.append((act[:, a * PEER_NKEYS:(a + 1) * PEER_NKEYS] * gate).astype(BF16))
    o_ref[...] += _dot(jnp.concatenate(slabs, axis=1), v_ref[0].astype(BF16))

    @pl.when(c == pl.num_programs(1) - 1)
    def _():
        y = x_ref[...] + m_ref[0, 5:6, :] * o_ref[...]
        if len(rest) == 4:
            y = y * lax.rsqrt(jnp.mean(y * y, axis=-1, keepdims=True) + EPS) * rest[0][...]
        o_ref[...] = y


def _peer_experts(x, mods, hb, gate, u8_all, su_all, v_all, layer, final_g=None):
    n_tiles = N_TOK // TE
    n_chunks = PEER_EXPERTS // EC
    once = pl.Buffered(1)
    in_specs = [
        pl.BlockSpec((TE, D), lambda t, c: (t, 0), pipeline_mode=once),
        pl.BlockSpec((1, 6, D), lambda t, c: (_mod_row(t, TE), 0, 0)),
        pl.BlockSpec((TE, D), lambda t, c: (t, 0), pipeline_mode=once),
        pl.BlockSpec((EC // PEER_NKEYS, TE, PEER_NKEYS), lambda t, c: (c, t, 0)),
        pl.BlockSpec((1, EC, D), lambda t, c: (layer, c, 0)),
        pl.BlockSpec((1, 1, 1, EC), lambda t, c: (layer, c, 0, 0)),
        pl.BlockSpec((1, EC, D), lambda t, c: (layer, c, 0)),
    ]
    operands = [x, mods, hb, gate, u8_all, su_all, v_all]
    if final_g is not None:
        in_specs.append(pl.BlockSpec((1, D), lambda t, c: (0, 0)))
        operands.append(final_g)
    return pl.pallas_call(
        _expert_kernel,
        grid=(n_tiles, n_chunks),
        in_specs=in_specs,
        out_specs=pl.BlockSpec((TE, D), lambda t, c: (t, 0)),
        out_shape=jax.ShapeDtypeStruct((N_TOK, D), F32),
        scratch_shapes=[pltpu.VMEM((TE, D), F8), pltpu.VMEM((TE, 1), F32)],
        compiler_params=_cparams(("arbitrary", "arbitrary")),
        name="peer_experts",
    )(*operands)


def _rope_tables():
    t = jnp.arange(DEC_SEQ)
    row, col = t // GRID_W, t % GRID_W
    nf = HEAD_DIM // 4
    inv = ROPE_THETA ** (-jnp.arange(nf, dtype=F32) / nf)
    ang_r = row.astype(F32)[:, None] * inv[None, :]
    ang_c = col.astype(F32)[:, None] * inv[None, :]
    cos = jnp.concatenate([jnp.cos(ang_r)] * 2 + [jnp.cos(ang_c)] * 2, axis=1)
    sin = jnp.concatenate([-jnp.sin(ang_r), jnp.sin(ang_r), -jnp.sin(ang_c), jnp.sin(ang_c)], axis=1)
    cos = jnp.concatenate([jnp.ones((TM, HEAD_DIM), F32), cos], axis=0)
    sin = jnp.concatenate([jnp.zeros((TM, HEAD_DIM), F32), sin], axis=0)
    return jnp.tile(cos, (1, N_HEADS)), jnp.tile(sin, (1, N_HEADS))


def _head_block_diag():
    i = np.arange(128)
    return jnp.asarray((i[:, None] // HEAD_DIM == i[None, :] // HEAD_DIM).astype(np.float32), dtype=BF16)


def kernel(x_prompt, x_sample, cache_k, cache_v, c, c_ctx, mod_w, mod_b, norm1_g, norm2_g, a_in_w, a_out_w,
           q_norm_g, k_norm_g, pool_w, pool_scale, c_in_w, c_conv_w, c_out_w, peer_wq, peer_subkeys, peer_u,
           peer_v, final_g):
    x = jnp.concatenate([x_prompt.reshape(NP_TOK, D), x_sample.reshape(NS_TOK, D)], axis=0)
    ct = jnp.zeros((D, 8), F32).at[:, 0].set(c_ctx).at[:, 1:3].set(c.T)
    mods = _modulation(ct, mod_w, mod_b).reshape(DEPTH, 8, 6, D)
    rope_c, rope_s = _rope_tables()
    bd = _head_block_diag()
    wq = peer_wq.astype(BF16)
    u8, su = _quantize_rows(peer_u)
    new_k, new_v = [], []
    for l in range(DEPTH):
        g1 = norm1_g[l].reshape(1, D)
        if l % 2 == 0:
            i = l // 2
            q, kk, vv, knew, vnew, p = _attn_in(
                x, mods[l], g1, a_in_w[i].astype(BF16), jnp.tile(q_norm_g[i], N_HEADS).reshape(1, ATTN_W),
                jnp.tile(k_norm_g[i], N_KV).reshape(1, KV_W), bd, rope_c, rope_s)
            new_k.append(knew[:NP_TOK].reshape(N_PROMPT_SEQ, SEQ, N_KV, HEAD_DIM))
            new_v.append(vnew[:NP_TOK].reshape(N_PROMPT_SEQ, SEQ, N_KV, HEAD_DIM))
            ck = cache_k[:, i]
            cv = cache_v[:, i]
            ctx_kk = jnp.concatenate([ck, ck[:, :, ::-1]], axis=2).reshape(N_SAMPLE_SEQ, PAST, 2 * KV_W).astype(BF16)
            ctx_vv = jnp.concatenate([cv, cv[:, :, ::-1]], axis=2).reshape(N_SAMPLE_SEQ, PAST, 2 * KV_W).astype(BF16)
            kk_p = kk[:NP_TOK].reshape(N_PROMPT_SEQ, SEQ, 2 * KV_W)
            vv_p = vv[:NP_TOK].reshape(N_PROMPT_SEQ, SEQ, 2 * KV_W)
            kk_s = jnp.concatenate([kk[NP_TOK:].reshape(N_SAMPLE_SEQ, DEC_SEQ, 2 * KV_W), ctx_kk], axis=1)
            vv_s = jnp.concatenate([vv[NP_TOK:].reshape(N_SAMPLE_SEQ, DEC_SEQ, 2 * KV_W), ctx_vv], axis=1)
            common = (pool_w[i].astype(BF16), pool_scale[i].reshape(1, POOL_W), a_out_w[i].astype(BF16))
            x = _attn_out(x, mods[l], q, kk_p, vv_p, p, *common,
                          first_tile=0, n_tiles=N_PROMPT_SEQ, tiles_per_seq=SEQ // TM)
            x = _attn_out(x, mods[l], q, kk_s, vv_s, p, *common,
                          first_tile=N_PROMPT_SEQ, n_tiles=NS_TOK // TM, tiles_per_seq=DEC_SEQ // TM)
        else:
            j = l // 2
            cw = jnp.zeros((8, D), F32).at[:3].set(c_conv_w[j])
            x = _conv_layer(x, mods[l], g1, c_in_w[j].astype(BF16), cw, c_out_w[j].astype(BF16))
        hb, gate = _peer_route(x, mods[l], norm2_g[l].reshape(1, D), wq[l], peer_subkeys[l])
        x = _peer_experts(x, mods[l], hb, gate, u8, su, peer_v, l,
                          final_g=final_g.reshape(1, D) if l == DEPTH - 1 else None)
    y = x
    y_prompt = y[:NP_TOK].reshape(N_PROMPT_SEQ, SEQ, D)
    y_sample = y[NP_TOK:].reshape(N_SAMPLE_SEQ, DEC_SEQ, D)
    return (y_prompt, y_sample, jnp.stack(new_k, axis=1), jnp.stack(new_v, axis=1))
```

```python
import functools

import numpy as np
import jax
import jax.numpy as jnp
from jax import lax
from jax.experimental import pallas as pl
from jax.experimental.pallas import tpu as pltpu

F32 = jnp.float32
BF16 = jnp.bfloat16

D = 1024
N_PROMPT_SEQ, SEQ = 32, 256
N_SAMPLE_SEQ, DEC_SEQ = 2, 2048
PAST = 256
NP_TOK = N_PROMPT_SEQ * SEQ
NS_TOK = N_SAMPLE_SEQ * DEC_SEQ
N_TOK = NP_TOK + NS_TOK
DEPTH = 4
GRID_W = 64
N_HEADS, N_KV, HEAD_DIM = 8, 2, 64
ATTN_W, KV_W, POOL_W = 512, 128, 512
POOL_WINDOWS = (2, 4, 8, 16)
A_IN_W = ATTN_W + 2 * KV_W + POOL_W
PEER_HEADS, PEER_NKEYS, PEER_TOPK, PEER_DK = 8, 128, 16, 128
PEER_EXPERTS = PEER_NKEYS * PEER_NKEYS
ROPE_THETA = 10000.0
EPS = 1e-6

TM = 256
HALO = 8
TR = 256
G_PITCH = TR + 8
TE = 1024
EC = 1024
EC_SUB = 512
VMEM_LIMIT = 56 * 1024 * 1024


def _cparams(sem):
    return pltpu.CompilerParams(dimension_semantics=sem, vmem_limit_bytes=VMEM_LIMIT)


def _mod_row(tile, tile_tokens):
    start = tile * tile_tokens
    return (start >= NP_TOK).astype(jnp.int32) + (start >= NP_TOK + DEC_SEQ).astype(jnp.int32)


def _modnorm(x, g, shift, scale):
    ms = jnp.mean(x * x, axis=-1, keepdims=True)
    y = x * lax.rsqrt(ms + EPS) * g
    return y * (1.0 + scale) + shift


def _split_bf16(a):
    hi = a.astype(BF16)
    lo = (a - hi.astype(F32)).astype(BF16)
    return hi, lo


def _dot(a, b):
    return jnp.dot(a, b, preferred_element_type=F32)


def _dot_nt(a, b):
    return lax.dot_general(a, b, (((1,), (1,)), ((), ())), preferred_element_type=F32)


def _mod_kernel(ct_ref, w_ref, b_ref, o_ref):
    ct = ct_ref[...]
    s = ct * jax.nn.sigmoid(ct)
    w = w_ref[0]
    rows = [jnp.sum(s[:, r:r + 1] * w, axis=0, keepdims=True) + b_ref[0] for r in range(3)]
    rows.append(jnp.zeros((5, w.shape[1]), F32))
    o_ref[0] = jnp.concatenate(rows, axis=0)


def _modulation(ct, mod_w, mod_b):
    tn = 1536
    return pl.pallas_call(
        _mod_kernel,
        grid=(DEPTH, 6 * D // tn),
        in_specs=[
            pl.BlockSpec((D, 8), lambda l, j: (0, 0)),
            pl.BlockSpec((1, D, tn), lambda l, j: (l, 0, j)),
            pl.BlockSpec((1, 1, tn), lambda l, j: (l, 0, j)),
        ],
        out_specs=pl.BlockSpec((1, 8, tn), lambda l, j: (l, 0, j)),
        out_shape=jax.ShapeDtypeStruct((DEPTH, 8, 6 * D), F32),
        compiler_params=_cparams(("arbitrary", "arbitrary")),
        name="adaln_modulation",
    )(ct, mod_w, mod_b.reshape(DEPTH, 1, 6 * D))


def _head_norm(x, g, bd):
    outs = []
    for c in range(x.shape[1] // 128):
        xc = x[:, c * 128:(c + 1) * 128]
        hi, lo = _split_bf16(xc * xc)
        ms = (_dot(hi, bd) + _dot(lo, bd)) * (1.0 / HEAD_DIM)
        outs.append(xc * lax.rsqrt(ms + EPS))
    return jnp.concatenate(outs, axis=1) * g


def _rope(x, cos, sin, low16):
    outs = []
    for c in range(x.shape[1] // 128):
        sl = slice(c * 128, (c + 1) * 128)
        xc = x[:, sl]
        partner = jnp.where(low16, pltpu.roll(xc, 112, 1), pltpu.roll(xc, 16, 1))
        outs.append(xc * cos[:, sl] + partner * sin[:, sl])
    return jnp.concatenate(outs, axis=1)


def _a1_kernel(x_ref, m_ref, g1_ref, w_ref, qg_ref, kg_ref, bd_ref, rc_ref, rs_ref,
               q_ref, kk_ref, vv_ref, knew_ref, vnew_ref, p_ref):
    h = _modnorm(x_ref[...], g1_ref[...], m_ref[0, 0:1, :], m_ref[0, 1:2, :])
    z = _dot(h.astype(BF16), w_ref[...])
    bd = bd_ref[...]
    q = _head_norm(z[:, :ATTN_W], qg_ref[...], bd)
    k = _head_norm(z[:, ATTN_W:ATTN_W + KV_W], kg_ref[...], bd)
    v = z[:, ATTN_W + KV_W:ATTN_W + 2 * KV_W]
    knew_ref[...] = k
    vnew_ref[...] = v
    p_ref[...] = z[:, ATTN_W + 2 * KV_W:]
    cos, sin = rc_ref[...], rs_ref[...]
    lane = lax.broadcasted_iota(jnp.int32, (TM, 128), 1)
    low16 = (lane & 31) < 16
    q = _rope(q, cos, sin, low16) * (HEAD_DIM ** -0.5)
    k = _rope(k, cos, sin, low16)
    q_ref[...] = q.astype(BF16)
    kk_ref[...] = jnp.concatenate([k, pltpu.roll(k, 64, 1)], axis=1).astype(BF16)
    vv_ref[...] = jnp.concatenate([v, pltpu.roll(v, 64, 1)], axis=1).astype(BF16)


def _rope_block(t):
    return jnp.where(t < N_PROMPT_SEQ, 0, 1 + (t - N_PROMPT_SEQ) % (DEC_SEQ // TM))


def _attn_in(x, mods, g1, w_in, qg, kg, bd, rope_c, rope_s):
    n_tiles = N_TOK // TM
    tile = lambda w: pl.BlockSpec((TM, w), lambda t: (t, 0))
    const = lambda shape: pl.BlockSpec(shape, lambda t: (0,) * len(shape))
    return pl.pallas_call(
        _a1_kernel,
        grid=(n_tiles,),
        in_specs=[
            tile(D),
            pl.BlockSpec((1, 6, D), lambda t: (_mod_row(t, TM), 0, 0)),
            const((1, D)),
            const((D, A_IN_W)),
            const((1, ATTN_W)),
            const((1, KV_W)),
            const((128, 128)),
            pl.BlockSpec((TM, ATTN_W), lambda t: (_rope_block(t), 0)),
            pl.BlockSpec((TM, ATTN_W), lambda t: (_rope_block(t), 0)),
        ],
        out_specs=[tile(ATTN_W), tile(2 * KV_W), tile(2 * KV_W), tile(KV_W), tile(KV_W), tile(POOL_W)],
        out_shape=[
            jax.ShapeDtypeStruct((N_TOK, ATTN_W), BF16),
            jax.ShapeDtypeStruct((N_TOK, 2 * KV_W), BF16),
            jax.ShapeDtypeStruct((N_TOK, 2 * KV_W), BF16),
            jax.ShapeDtypeStruct((N_TOK, KV_W), F32),
            jax.ShapeDtypeStruct((N_TOK, KV_W), F32),
            jax.ShapeDtypeStruct((N_TOK, POOL_W), F32),
        ],
        compiler_params=_cparams(("arbitrary",)),
        name="attn_in_proj",
    )(x, mods, g1, w_in, qg, kg, bd, rope_c, rope_s)


def _a2_kernel(x_ref, m_ref, q_ref, kk_ref, vv_ref, p_ref, pp_ref, pn_ref, pw_ref, ps_ref,
               wo_ref, o_ref, *, tiles_per_seq):
    t = pl.program_id(0)
    q = q_ref[...]
    kk = kk_ref[0]
    vv = vv_ref[0]
    lane = lax.broadcasted_iota(jnp.int32, (TM, 128), 1)
    half_mask = (lane < HEAD_DIM, lane >= HEAD_DIM)
    zero_q = jnp.zeros((TM, 128), BF16)
    chunks = []
    for c in range(ATTN_W // 128):
        qc = q[:, c * 128:(c + 1) * 128]
        acc = None
        for half in range(2):
            kv = (2 * c + half) // (N_HEADS // N_KV)
            slab = slice(0, 128) if kv == half else slice(128, 256)
            s = _dot_nt(jnp.where(half_mask[half], qc, zero_q), kk[:, slab])
            e = jnp.exp(s - jnp.max(s, axis=-1, keepdims=True))
            l = jnp.sum(e, axis=-1, keepdims=True)
            o = _dot(e.astype(BF16), vv[:, slab]) / l
            o = jnp.where(half_mask[half], o, 0.0)
            acc = o if acc is None else acc + o
        chunks.append(acc)

    pos = t % tiles_per_seq
    has_prev = jnp.where(pos == 0, 0.0, 1.0)
    has_next = jnp.where(pos == tiles_per_seq - 1, 0.0, 1.0)
    p = p_ref[...]
    pext = jnp.concatenate([pp_ref[...] * has_prev, p, pn_ref[...] * has_next], axis=0)
    rows_ext = TM + 2 * HALO
    trow = lax.broadcasted_iota(jnp.int32, (TM, 128), 0).astype(F32)
    for g, w in enumerate(POOL_WINDOWS):
        sl = slice(g * 128, (g + 1) * 128)
        a = pext[:, sl]
        s = a + pltpu.roll(a, 1, 0)
        half = 1
        while 2 * half < w:
            s = pltpu.roll(s, half, 0) + pltpu.roll(s, rows_ext - half, 0)
            half *= 2
        cnt = (w - jnp.maximum(w // 2 - trow, 0.0) * (1.0 - has_prev)
               - jnp.maximum(trow + (w // 2 - TM), 0.0) * (1.0 - has_next))
        dev = s[HALO:HALO + TM] / cnt - p[:, sl]
        chunks.append(_dot(dev.astype(BF16), pw_ref[g]) * ps_ref[:, sl])
    mix = jnp.concatenate(chunks, axis=1).astype(BF16)
    o_ref[...] = x_ref[...] + m_ref[0, 2:3, :] * _dot(mix, wo_ref[...])


def _attn_out(x, mods, q, kk_seq, vv_seq, p, pool_w, pool_s, w_out, *, first_tile, n_tiles, tiles_per_seq):
    s_len = kk_seq.shape[1]
    off = first_tile
    n_halo_blocks = N_TOK // HALO
    per_tile = TM // HALO
    tile = lambda w: pl.BlockSpec((TM, w), lambda t: (t + off, 0))
    const = lambda shape: pl.BlockSpec(shape, lambda t: (0,) * len(shape))
    seq = pl.BlockSpec((1, s_len, 2 * KV_W), lambda t: (t // tiles_per_seq, 0, 0))
    return pl.pallas_call(
        functools.partial(_a2_kernel, tiles_per_seq=tiles_per_seq),
        grid=(n_tiles,),
        in_specs=[
            tile(D),
            pl.BlockSpec((1, 6, D), lambda t: (_mod_row(t + off, TM), 0, 0)),
            tile(ATTN_W),
            seq,
            seq,
            tile(POOL_W),
            pl.BlockSpec((HALO, POOL_W), lambda t: (jnp.maximum((t + off) * per_tile - 1, 0), 0)),
            pl.BlockSpec((HALO, POOL_W), lambda t: (jnp.minimum((t + off + 1) * per_tile, n_halo_blocks - 1), 0)),
            const((len(POOL_WINDOWS), 128, 128)),
            const((1, POOL_W)),
            const((D, D)),
        ],
        out_specs=tile(D),
        out_shape=jax.ShapeDtypeStruct((N_TOK, D), F32),
        input_output_aliases={0: 0},
        compiler_params=_cparams(("arbitrary",)),
        name="attn_pool_out_%d" % s_len,
    )(x, mods, q, kk_seq, vv_seq, p, p, p, pool_w, pool_s, w_out)


def _conv_kernel(x_ref, xp_ref, xn_ref, m_ref, g1_ref, wi_ref, cw_ref, wo_ref, o_ref):
    t = pl.program_id(0)
    is_prompt = t < N_PROMPT_SEQ
    pos = t & (DEC_SEQ // TM - 1)
    has_prev = jnp.where(is_prompt | (pos == 0), 0.0, 1.0)
    has_next = jnp.where(is_prompt | (pos == DEC_SEQ // TM - 1), 0.0, 1.0)
    x = x_ref[...]
    xext = jnp.concatenate([xp_ref[...], x, xn_ref[...]], axis=0)
    h = _modnorm(xext, g1_ref[...], m_ref[0, 0:1, :], m_ref[0, 1:2, :])
    z = _dot(h.astype(BF16), wi_ref[...])
    gb = z[HALO:HALO + TM, :D]
    u = z[:, D:2 * D] * z[:, 2 * D:]
    row = lax.broadcasted_iota(jnp.int32, (TM + 2 * HALO, 1), 0)
    u = u * jnp.where(row < HALO, has_prev, jnp.where(row >= TM + HALO, has_next, 1.0))
    rows = TM + 2 * HALO
    cw = cw_ref[...]
    y = (pltpu.roll(u, 1, 0)[HALO:HALO + TM] * cw[0:1, :]
         + u[HALO:HALO + TM] * cw[1:2, :]
         + pltpu.roll(u, rows - 1, 0)[HALO:HALO + TM] * cw[2:3, :])
    out = _dot((gb * y).astype(BF16), wo_ref[...])
    o_ref[...] = x + m_ref[0, 2:3, :] * out


def _conv_layer(x, mods, g1, w_in, conv_w, w_out):
    n_tiles = N_TOK // TM
    n_halo_blocks = N_TOK // HALO
    per_tile = TM // HALO
    const = lambda shape: pl.BlockSpec(shape, lambda t: (0,) * len(shape))
    return pl.pallas_call(
        _conv_kernel,
        grid=(n_tiles,),
        in_specs=[
            pl.BlockSpec((TM, D), lambda t: (t, 0)),
            pl.BlockSpec((HALO, D), lambda t: (jnp.maximum(t * per_tile - 1, 0), 0)),
            pl.BlockSpec((HALO, D), lambda t: (jnp.minimum((t + 1) * per_tile, n_halo_blocks - 1), 0)),
            pl.BlockSpec((1, 6, D), lambda t: (_mod_row(t, TM), 0, 0)),
            const((1, D)),
            const((D, 3 * D)),
            const((8, D)),
            const((D, D)),
        ],
        out_specs=pl.BlockSpec((TM, D), lambda t: (t, 0)),
        out_shape=jax.ShapeDtypeStruct((N_TOK, D), F32),
        compiler_params=_cparams(("arbitrary",)),
        name="conv_mixer",
    )(x, x, x, mods, g1, w_in, conv_w, w_out)


NEG_INF = float("-inf")
SUB = 8
NO_ROW = float(1 << 20)


def _sort_network(n):
    def merge(lo, hi, r):
        step = r * 2
        if step < hi - lo:
            yield from merge(lo, hi, step)
            yield from merge(lo + r, hi, step)
            yield from [(i, i + r) for i in range(lo + r, hi - r, step)]
        else:
            yield (lo, lo + r)

    def sort(lo, hi):
        if hi - lo >= 1:
            mid = lo + (hi - lo) // 2
            yield from sort(lo, mid)
            yield from sort(mid + 1, hi)
            yield from merge(lo, hi, 1)

    full = 1 << (n - 1).bit_length()
    return tuple((a, b) for a, b in sort(0, full - 1) if b < n)


def _top16_of_blocks(vs, ts, row_of, n_out=PEER_TOPK):
    vs, ts = list(vs), list(ts)
    for a, b in _sort_network(len(vs)):
        va, vb, ta, tb = vs[a], vs[b], ts[a], ts[b]
        if isinstance(ta, float) and isinstance(tb, float):
            first = (va >= vb) if ta < tb else (va > vb)
        else:
            first = (va > vb) | ((va == vb) & (ta < tb))
        vs[a], vs[b] = jnp.maximum(va, vb), jnp.minimum(va, vb)
        ts[a], ts[b] = jnp.where(first, ta, tb), jnp.where(first, tb, ta)
    vals, rows_out = [], []
    for r in range(n_out):
        rows = row_of(ts[0])
        m = jnp.max(vs[0], axis=0, keepdims=True)
        ix = jnp.min(jnp.where(vs[0] == m, rows, NO_ROW), axis=0, keepdims=True)
        vals.append(m)
        rows_out.append(ix)
        taken = rows == ix
        for k in range(min(len(vs), n_out - r - 1)):
            if k + 1 < len(vs):
                vs[k] = jnp.where(taken, vs[k + 1], vs[k])
                ts[k] = jnp.where(taken, ts[k + 1], ts[k])
            else:
                vs[k] = jnp.where(taken, NEG_INF, vs[k])
    return jnp.concatenate(vals, axis=0), jnp.concatenate(rows_out, axis=0)


def _top16_keys(s, sub):
    n = PEER_NKEYS // SUB
    return _top16_of_blocks([s[SUB * v:SUB * (v + 1), :] for v in range(n)], [float(v) for v in range(n)],
                            lambda blk: blk * float(SUB) + sub)


def _pair_candidates(sv1, sv2, sub):
    a_lo, a_hi, b_lo, b_hi = sv1[0:SUB], sv1[SUB:], sv2[0:SUB], sv2[SUB:]
    row = lambda x, r: x[r:r + 1, :]
    keep = lambda cond, v: jnp.where(cond, v, NEG_INF)
    k = float(PEER_TOPK)
    tail = keep((sub >= 2) & (sub <= 4), row(sv1, 2) + b_lo)
    tail_flat = sub + 2 * k
    for r, (a, b) in ((5, (3, 2)), (6, (4, 2)), (7, (3, 3))):
        tail = jnp.where(sub == r, row(sv1, a) + row(sv2, b), tail)
        tail_flat = jnp.where(sub == r, a * k + b, tail_flat)
    return [
        (row(sv1, 0) + b_lo, sub),
        (row(sv1, 0) + b_hi, sub + SUB),
        (keep(sub >= 1, a_lo + row(sv2, 0)), sub * k),
        (a_hi + row(sv2, 0), (sub + SUB) * k),
        (keep(sub >= 1, row(sv1, 1) + b_lo), sub + k),
        (keep(sub >= 2, a_lo + row(sv2, 1)), sub * k + 1),
        (tail, tail_flat),
    ]


def _route_kernel(x_ref, m_ref, g2_ref, wq_ref, sk_ref, gs_ref, hb_ref, gate_ref,
                  q_scr, i1t_scr, i2t_scr, gtt_scr, i1_scr, i2_scr, gt_scr, gscr):
    @pl.when(pl.program_id(0) == 0)
    def _():
        i1_scr[...] = jnp.zeros_like(i1_scr)
        i2_scr[...] = jnp.zeros_like(i2_scr)
        gt_scr[...] = jnp.zeros_like(gt_scr)

    key = lax.broadcasted_iota(jnp.int32, (PEER_NKEYS, 128), 0).astype(F32)

    def expand_token(n):
        row = pl.ds(n, 1)
        left = jnp.where(key == i1_scr[row, :], gt_scr[row, :], 0.0).astype(BF16)
        right = jnp.where(key == i2_scr[row, :], 1.0, 0.0).astype(BF16)
        gscr[pl.ds(n, PEER_NKEYS, stride=G_PITCH), :] = _dot_nt(left, right)

    def copy_out(i, carry):
        rows = gscr[pl.ds(pl.multiple_of(i * G_PITCH, 8), TR), :]
        gate_ref[i] = (rows * gs_ref[pl.ds(i, 1), :]).astype(BF16)
        return carry

    last = pl.num_programs(0) - 1

    @pl.when(pl.program_id(0) == last)
    def _():
        hb_ref[...] = _modnorm(x_ref[...], g2_ref[...], m_ref[0, 3:4, :], m_ref[0, 4:5, :]).astype(BF16)

        def expand_group(j, carry):
            for k in range(16):
                expand_token(j * 16 + k)
            return carry

        lax.fori_loop(0, TR // 16, expand_group, 0)
        lax.fori_loop(0, PEER_NKEYS, copy_out, 0, unroll=4)

    @pl.when(pl.program_id(0) < last)
    def _():
        _route_select(x_ref, m_ref, g2_ref, wq_ref, sk_ref, hb_ref, q_scr, i1t_scr, i2t_scr, gtt_scr, expand_token)
        lax.fori_loop(0, PEER_NKEYS, copy_out, 0, unroll=4)
        i1_scr[...] = i1t_scr[...].T
        i2_scr[...] = i2t_scr[...].T
        gt_scr[...] = gtt_scr[...].T


def _route_select(x_ref, m_ref, g2_ref, wq_ref, sk_ref, hb_ref, q_scr, i1t_scr, i2t_scr, gtt_scr, expand_token):
    h = _modnorm(x_ref[...], g2_ref[...], m_ref[0, 3:4, :], m_ref[0, 4:5, :])
    hb = h.astype(BF16)
    hb_ref[...] = hb
    q = _dot(hb, wq_ref[...])
    for hp in range(2 * PEER_HEADS):
        q_scr[hp] = q[:, hp * PEER_DK:(hp + 1) * PEER_DK]
    sub = lax.broadcasted_iota(jnp.int32, (SUB, TR), 0).astype(F32)
    sk_split = [_split_bf16(sk_ref[p]) for p in range(2)]

    def key_stage(hd):
        top = ()
        for p in range(2):
            q_hi, q_lo = _split_bf16(q_scr[2 * hd + p])
            k_hi, k_lo = sk_split[p]
            s = _dot_nt(k_hi, q_hi) + (_dot_nt(k_hi, q_lo) + _dot_nt(k_lo, q_hi))
            top += _top16_keys(s, sub)
        return top

    def pair_stage(hd, top):
        sv1, si1, sv2, si2 = top
        cands = _pair_candidates(sv1, sv2, sub)
        fv, ff = _top16_of_blocks([c[0] for c in cands], [c[1] for c in cands], lambda flat: flat)
        ff = ff.astype(jnp.int32)
        fa, fb = ff >> 4, ff & (PEER_TOPK - 1)
        i1 = jnp.zeros((PEER_TOPK, TR), F32)
        i2 = jnp.zeros((PEER_TOPK, TR), F32)
        for a in range(PEER_TOPK):
            i1 = jnp.where(fa == a, si1[a:a + 1, :], i1)
            i2 = jnp.where(fb == a, si2[a:a + 1, :], i2)
        e = jnp.exp(fv - fv[0:1, :])
        g = e / jnp.sum(e, axis=0, keepdims=True)
        start = hd * PEER_TOPK
        rows = pl.ds(start if isinstance(start, int) else pl.multiple_of(start, PEER_TOPK), PEER_TOPK)
        i1t_scr[rows, :] = i1
        i2t_scr[rows, :] = i2
        gtt_scr[rows, :] = g

    n_first, n_mid = TR // 8, (TR - TR // 8) // 8
    top = key_stage(0)
    for k in range(n_first):
        expand_token(k)

    def per_head(hd, top_prev):
        top_cur = key_stage(hd)
        pair_stage(hd - 1, top_prev)
        for k in range(n_mid):
            expand_token(n_first + (hd - 1) * n_mid + k)
        return top_cur

    top = lax.fori_loop(1, PEER_HEADS, per_head, top)
    pair_stage(PEER_HEADS - 1, top)
    for n in range(n_first + (PEER_HEADS - 1) * n_mid, TR):
        expand_token(n)


def _peer_route(x, mods, g2, wq, subkeys, gate_scale):
    n_tiles = N_TOK // TR
    const = lambda shape: pl.BlockSpec(shape, lambda t: (0,) * len(shape))
    cur = lambda t: jnp.minimum(t, n_tiles - 1)
    return pl.pallas_call(
        _route_kernel,
        grid=(n_tiles + 1,),
        in_specs=[
            pl.BlockSpec((TR, D), lambda t: (cur(t), 0)),
            pl.BlockSpec((1, 6, D), lambda t: (_mod_row(cur(t), TR), 0, 0)),
            const((1, D)),
            const((D, PEER_HEADS * 2 * PEER_DK)),
            const((2, PEER_NKEYS, PEER_DK)),
            const((PEER_NKEYS, PEER_NKEYS)),
        ],
        out_specs=[
            pl.BlockSpec((TR, D), lambda t: (cur(t), 0)),
            pl.BlockSpec((PEER_NKEYS, TR, PEER_NKEYS), lambda t: (0, jnp.maximum(t - 1, 0), 0)),
        ],
        out_shape=[
            jax.ShapeDtypeStruct((N_TOK, D), BF16),
            jax.ShapeDtypeStruct((PEER_NKEYS, N_TOK, PEER_NKEYS), BF16),
        ],
        scratch_shapes=[
            pltpu.VMEM((2 * PEER_HEADS, TR, PEER_DK), F32),
            pltpu.VMEM((PEER_HEADS * PEER_TOPK, TR), F32),
            pltpu.VMEM((PEER_HEADS * PEER_TOPK, TR), F32),
            pltpu.VMEM((PEER_HEADS * PEER_TOPK, TR), F32),
            pltpu.VMEM((TR, PEER_HEADS * PEER_TOPK), F32),
            pltpu.VMEM((TR, PEER_HEADS * PEER_TOPK), F32),
            pltpu.VMEM((TR, PEER_HEADS * PEER_TOPK), F32),
            pltpu.VMEM((PEER_NKEYS * G_PITCH, PEER_NKEYS), F32),
        ],
        compiler_params=_cparams(("arbitrary",)),
        name="peer_route",
    )(x, mods, g2, wq, subkeys, gate_scale)


F8 = jnp.float8_e4m3fn
F8_TOP = 256.0
TINY = 1e-30


def _row_scale(a):
    return jnp.maximum(jnp.max(jnp.abs(a), axis=1, keepdims=True), TINY) * (1.0 / F8_TOP)


def _quant_rows_kernel(w_ref, q_ref, s_ref):
    w = w_ref[0]
    scale = _row_scale(w)
    q_ref[0] = (w * (1.0 / scale)).astype(F8)
    s_ref[0, 0] = jnp.transpose(jnp.broadcast_to(scale, (EC, 128)))[0:1, :]


def _quantize_rows(w_all):
    n_chunks = PEER_EXPERTS // EC
    return pl.pallas_call(
        _quant_rows_kernel,
        grid=(DEPTH, n_chunks),
        in_specs=[pl.BlockSpec((1, EC, D), lambda l, j: (l, j, 0))],
        out_specs=[pl.BlockSpec((1, EC, D), lambda l, j: (l, j, 0)),
                   pl.BlockSpec((1, 1, 1, EC), lambda l, j: (l, j, 0, 0))],
        out_shape=[jax.ShapeDtypeStruct((DEPTH, PEER_EXPERTS, D), F8),
                   jax.ShapeDtypeStruct((DEPTH, n_chunks, 1, EC), F32)],
        compiler_params=_cparams(("arbitrary", "arbitrary")),
        name="quantize_rows",
    )(w_all)


def _expert_kernel(x_ref, m_ref, hb_ref, gate_ref, u8_ref, su_ref, v8_ref, *rest):
    o_ref, h8_scr, t_scr = rest[-3:]
    c = pl.program_id(1)

    @pl.when(c == 0)
    def _():
        o_ref[...] = jnp.zeros_like(o_ref)
        hf = hb_ref[...].astype(F32)
        t = jnp.maximum(jnp.max(jnp.abs(hf), keepdims=True), TINY) * (1.0 / F8_TOP)
        h8_scr[...] = (hf * (1.0 / t)).astype(F8)
        t_scr[...] = t

    h8 = h8_scr[...]
    scale = su_ref[0, 0] * (t_scr[...] * (2.0 ** -0.5))
    slabs = []
    for s in range(EC // EC_SUB):
        cols = slice(s * EC_SUB, (s + 1) * EC_SUB)
        z = _dot_nt(h8, u8_ref[0, cols, :]) * scale[:, cols]
        act = z * (1.0 + lax.erf(z))
        for a in range(EC_SUB // PEER_NKEYS):
            gate = gate_ref[s * (EC_SUB // PEER_NKEYS) + a].astype(F32)
            slabs.append(act[:, a * PEER_NKEYS:(a + 1) * PEER_NKEYS] * gate)
    w = jnp.concatenate(slabs, axis=1)
    r = _row_scale(w)
    o_ref[...] += _dot((w * (1.0 / r)).astype(F8), v8_ref[0]) * r

    @pl.when(c == pl.num_programs(1) - 1)
    def _():
        y = x_ref[...] + m_ref[0, 5:6, :] * o_ref[...]
        if len(rest) == 4:
            y = y * lax.rsqrt(jnp.mean(y * y, axis=-1, keepdims=True) + EPS) * rest[0][...]
        o_ref[...] = y


def _peer_experts(x, mods, hb, gate, u8_all, su_all, v8_all, layer, final_g=None):
    n_tiles = N_TOK // TE
    n_chunks = PEER_EXPERTS // EC
    once = pl.Buffered(1)
    in_specs = [
        pl.BlockSpec((TE, D), lambda t, c: (t, 0), pipeline_mode=once),
        pl.BlockSpec((1, 6, D), lambda t, c: (_mod_row(t, TE), 0, 0)),
        pl.BlockSpec((TE, D), lambda t, c: (t, 0), pipeline_mode=once),
        pl.BlockSpec((EC // PEER_NKEYS, TE, PEER_NKEYS), lambda t, c: (c, t, 0)),
        pl.BlockSpec((1, EC, D), lambda t, c: (layer, c, 0)),
        pl.BlockSpec((1, 1, 1, EC), lambda t, c: (layer, c, 0, 0)),
        pl.BlockSpec((1, EC, D), lambda t, c: (layer, c, 0)),
    ]
    operands = [x, mods, hb, gate, u8_all, su_all, v8_all]
    if final_g is not None:
        in_specs.append(pl.BlockSpec((1, D), lambda t, c: (0, 0)))
        operands.append(final_g)
    return pl.pallas_call(
        _expert_kernel,
        grid=(n_tiles, n_chunks),
        in_specs=in_specs,
        out_specs=pl.BlockSpec((TE, D), lambda t, c: (t, 0)),
        out_shape=jax.ShapeDtypeStruct((N_TOK, D), F32),
        scratch_shapes=[pltpu.VMEM((TE, D), F8), pltpu.VMEM((1, 1), F32)],
        compiler_params=_cparams(("arbitrary", "arbitrary")),
        name="peer_experts",
    )(*operands)


def _rope_tables():
    t = jnp.arange(DEC_SEQ)
    row, col = t // GRID_W, t % GRID_W
    nf = HEAD_DIM // 4
    inv = ROPE_THETA ** (-jnp.arange(nf, dtype=F32) / nf)
    ang_r = row.astype(F32)[:, None] * inv[None, :]
    ang_c = col.astype(F32)[:, None] * inv[None, :]
    cos = jnp.concatenate([jnp.cos(ang_r)] * 2 + [jnp.cos(ang_c)] * 2, axis=1)
    sin = jnp.concatenate([-jnp.sin(ang_r), jnp.sin(ang_r), -jnp.sin(ang_c), jnp.sin(ang_c)], axis=1)
    cos = jnp.concatenate([jnp.ones((TM, HEAD_DIM), F32), cos], axis=0)
    sin = jnp.concatenate([jnp.zeros((TM, HEAD_DIM), F32), sin], axis=0)
    return jnp.tile(cos, (1, N_HEADS)), jnp.tile(sin, (1, N_HEADS))


def _head_block_diag():
    i = np.arange(128)
    return jnp.asarray((i[:, None] // HEAD_DIM == i[None, :] // HEAD_DIM).astype(np.float32), dtype=BF16)


def kernel(x_prompt, x_sample, cache_k, cache_v, c, c_ctx, mod_w, mod_b, norm1_g, norm2_g, a_in_w, a_out_w,
           q_norm_g, k_norm_g, pool_w, pool_scale, c_in_w, c_conv_w, c_out_w, peer_wq, peer_subkeys, peer_u,
           peer_v, final_g):
    x = jnp.concatenate([x_prompt.reshape(NP_TOK, D), x_sample.reshape(NS_TOK, D)], axis=0)
    ct = jnp.zeros((D, 8), F32).at[:, 0].set(c_ctx).at[:, 1:3].set(c.T)
    mods = _modulation(ct, mod_w, mod_b).reshape(DEPTH, 8, 6, D)
    rope_c, rope_s = _rope_tables()
    bd = _head_block_diag()
    wq = peer_wq.astype(BF16)
    u8, su = _quantize_rows(peer_u)
    v8, sv = _quantize_rows(peer_v)
    new_k, new_v = [], []
    for l in range(DEPTH):
        g1 = norm1_g[l].reshape(1, D)
        if l % 2 == 0:
            i = l // 2
            q, kk, vv, knew, vnew, p = _attn_in(
                x, mods[l], g1, a_in_w[i].astype(BF16), jnp.tile(q_norm_g[i], N_HEADS).reshape(1, ATTN_W),
                jnp.tile(k_norm_g[i], N_KV).reshape(1, KV_W), bd, rope_c, rope_s)
            new_k.append(knew[:NP_TOK].reshape(N_PROMPT_SEQ, SEQ, N_KV, HEAD_DIM))
            new_v.append(vnew[:NP_TOK].reshape(N_PROMPT_SEQ, SEQ, N_KV, HEAD_DIM))
            ck = cache_k[:, i]
            cv = cache_v[:, i]
            ctx_kk = jnp.concatenate([ck, ck[:, :, ::-1]], axis=2).reshape(N_SAMPLE_SEQ, PAST, 2 * KV_W).astype(BF16)
            ctx_vv = jnp.concatenate([cv, cv[:, :, ::-1]], axis=2).reshape(N_SAMPLE_SEQ, PAST, 2 * KV_W).astype(BF16)
            kk_p = kk[:NP_TOK].reshape(N_PROMPT_SEQ, SEQ, 2 * KV_W)
            vv_p = vv[:NP_TOK].reshape(N_PROMPT_SEQ, SEQ, 2 * KV_W)
            kk_s = jnp.concatenate([kk[NP_TOK:].reshape(N_SAMPLE_SEQ, DEC_SEQ, 2 * KV_W), ctx_kk], axis=1)
            vv_s = jnp.concatenate([vv[NP_TOK:].reshape(N_SAMPLE_SEQ, DEC_SEQ, 2 * KV_W), ctx_vv], axis=1)
            common = (pool_w[i].astype(BF16), pool_scale[i].reshape(1, POOL_W), a_out_w[i].astype(BF16))
            x = _attn_out(x, mods[l], q, kk_p, vv_p, p, *common,
                          first_tile=0, n_tiles=N_PROMPT_SEQ, tiles_per_seq=SEQ // TM)
            x = _attn_out(x, mods[l], q, kk_s, vv_s, p, *common,
                          first_tile=N_PROMPT_SEQ, n_tiles=NS_TOK // TM, tiles_per_seq=DEC_SEQ // TM)
        else:
            j = l // 2
            cw = jnp.zeros((8, D), F32).at[:3].set(c_conv_w[j])
            x = _conv_layer(x, mods[l], g1, c_in_w[j].astype(BF16), cw, c_out_w[j].astype(BF16))
        gate_scale = (2.0 ** -0.5) * sv[l].reshape(PEER_NKEYS, PEER_NKEYS)
        hb, gate = _peer_route(x, mods[l], norm2_g[l].reshape(1, D), wq[l], peer_subkeys[l], gate_scale)
        x = _peer_experts(x, mods[l], hb, gate, u8, su, v8, l,
                          final_g=final_g.reshape(1, D) if l == DEPTH - 1 else None)
    y = x
    y_prompt = y[:NP_TOK].reshape(N_PROMPT_SEQ, SEQ, D)
    y_sample = y[NP_TOK:].reshape(N_SAMPLE_SEQ, DEC_SEQ, D)
    return (y_prompt, y_sample, jnp.stack(new_k, axis=1), jnp.stack(new_v, axis=1))
```

```python
import functools

import numpy as np
import jax
import jax.numpy as jnp
from jax import lax
from jax.experimental import pallas as pl
from jax.experimental.pallas import tpu as pltpu

F32 = jnp.float32
BF16 = jnp.bfloat16

D = 1024
N_PROMPT_SEQ, SEQ = 32, 256
N_SAMPLE_SEQ, DEC_SEQ = 2, 2048
PAST = 256
NP_TOK = N_PROMPT_SEQ * SEQ
NS_TOK = N_SAMPLE_SEQ * DEC_SEQ
N_TOK = NP_TOK + NS_TOK
DEPTH = 4
GRID_W = 64
N_HEADS, N_KV, HEAD_DIM = 8, 2, 64
ATTN_W, KV_W, POOL_W = 512, 128, 512
POOL_WINDOWS = (2, 4, 8, 16)
A_IN_W = ATTN_W + 2 * KV_W + POOL_W
PEER_HEADS, PEER_NKEYS, PEER_TOPK, PEER_DK = 8, 128, 16, 128
PEER_EXPERTS = PEER_NKEYS * PEER_NKEYS
ROPE_THETA = 10000.0
EPS = 1e-6

TM = 256
HALO = 8
TR = 256
G_PITCH = TR + 8
TE = 1024
EC = 2048
EC_SUB = 512
VMEM_LIMIT = 56 * 1024 * 1024


def _cparams(sem):
    return pltpu.CompilerParams(dimension_semantics=sem, vmem_limit_bytes=VMEM_LIMIT)


def _mod_row(tile, tile_tokens):
    start = tile * tile_tokens
    return (start >= NP_TOK).astype(jnp.int32) + (start >= NP_TOK + DEC_SEQ).astype(jnp.int32)


def _modnorm(x, g, shift, scale):
    ms = jnp.mean(x * x, axis=-1, keepdims=True)
    y = x * lax.rsqrt(ms + EPS) * g
    return y * (1.0 + scale) + shift


def _split_bf16(a):
    hi = a.astype(BF16)
    lo = (a - hi.astype(F32)).astype(BF16)
    return hi, lo


def _dot(a, b):
    return jnp.dot(a, b, preferred_element_type=F32)


def _dot_nt(a, b):
    return lax.dot_general(a, b, (((1,), (1,)), ((), ())), preferred_element_type=F32)


def _mod_kernel(ct_ref, w_ref, b_ref, o_ref):
    ct = ct_ref[...]
    s = ct * jax.nn.sigmoid(ct)
    w = w_ref[0]
    rows = [jnp.sum(s[:, r:r + 1] * w, axis=0, keepdims=True) + b_ref[0] for r in range(3)]
    rows.append(jnp.zeros((5, w.shape[1]), F32))
    o_ref[0] = jnp.concatenate(rows, axis=0)


def _modulation(ct, mod_w, mod_b):
    tn = 1536
    return pl.pallas_call(
        _mod_kernel,
        grid=(DEPTH, 6 * D // tn),
        in_specs=[
            pl.BlockSpec((D, 8), lambda l, j: (0, 0)),
            pl.BlockSpec((1, D, tn), lambda l, j: (l, 0, j)),
            pl.BlockSpec((1, 1, tn), lambda l, j: (l, 0, j)),
        ],
        out_specs=pl.BlockSpec((1, 8, tn), lambda l, j: (l, 0, j)),
        out_shape=jax.ShapeDtypeStruct((DEPTH, 8, 6 * D), F32),
        compiler_params=_cparams(("arbitrary", "arbitrary")),
        name="adaln_modulation",
    )(ct, mod_w, mod_b.reshape(DEPTH, 1, 6 * D))


def _head_norm(x, g, bd):
    outs = []
    for c in range(x.shape[1] // 128):
        xc = x[:, c * 128:(c + 1) * 128]
        hi, lo = _split_bf16(xc * xc)
        ms = (_dot(hi, bd) + _dot(lo, bd)) * (1.0 / HEAD_DIM)
        outs.append(xc * lax.rsqrt(ms + EPS))
    return jnp.concatenate(outs, axis=1) * g


def _rope(x, cos, sin, low16):
    outs = []
    for c in range(x.shape[1] // 128):
        sl = slice(c * 128, (c + 1) * 128)
        xc = x[:, sl]
        partner = jnp.where(low16, pltpu.roll(xc, 112, 1), pltpu.roll(xc, 16, 1))
        outs.append(xc * cos[:, sl] + partner * sin[:, sl])
    return jnp.concatenate(outs, axis=1)


def _a1_kernel(x_ref, m_ref, g1_ref, w_ref, qg_ref, kg_ref, bd_ref, rc_ref, rs_ref,
               q_ref, kk_ref, vv_ref, knew_ref, vnew_ref, p_ref):
    h = _modnorm(x_ref[...], g1_ref[...], m_ref[0, 0:1, :], m_ref[0, 1:2, :])
    z = _dot(h.astype(BF16), w_ref[...])
    bd = bd_ref[...]
    q = _head_norm(z[:, :ATTN_W], qg_ref[...], bd)
    k = _head_norm(z[:, ATTN_W:ATTN_W + KV_W], kg_ref[...], bd)
    v = z[:, ATTN_W + KV_W:ATTN_W + 2 * KV_W]
    knew_ref[...] = k
    vnew_ref[...] = v
    p_ref[...] = z[:, ATTN_W + 2 * KV_W:]
    cos, sin = rc_ref[...], rs_ref[...]
    lane = lax.broadcasted_iota(jnp.int32, (TM, 128), 1)
    low16 = (lane & 31) < 16
    q = _rope(q, cos, sin, low16) * (HEAD_DIM ** -0.5)
    k = _rope(k, cos, sin, low16)
    q_ref[...] = q.astype(BF16)
    kk_ref[...] = jnp.concatenate([k, pltpu.roll(k, 64, 1)], axis=1).astype(BF16)
    vv_ref[...] = jnp.concatenate([v, pltpu.roll(v, 64, 1)], axis=1).astype(BF16)


def _rope_block(t):
    return jnp.where(t < N_PROMPT_SEQ, 0, 1 + (t - N_PROMPT_SEQ) % (DEC_SEQ // TM))


def _attn_in(x, mods, g1, w_in, qg, kg, bd, rope_c, rope_s):
    n_tiles = N_TOK // TM
    tile = lambda w: pl.BlockSpec((TM, w), lambda t: (t, 0))
    const = lambda shape: pl.BlockSpec(shape, lambda t: (0,) * len(shape))
    return pl.pallas_call(
        _a1_kernel,
        grid=(n_tiles,),
        in_specs=[
            tile(D),
            pl.BlockSpec((1, 6, D), lambda t: (_mod_row(t, TM), 0, 0)),
            const((1, D)),
            const((D, A_IN_W)),
            const((1, ATTN_W)),
            const((1, KV_W)),
            const((128, 128)),
            pl.BlockSpec((TM, ATTN_W), lambda t: (_rope_block(t), 0)),
            pl.BlockSpec((TM, ATTN_W), lambda t: (_rope_block(t), 0)),
        ],
        out_specs=[tile(ATTN_W), tile(2 * KV_W), tile(2 * KV_W), tile(KV_W), tile(KV_W), tile(POOL_W)],
        out_shape=[
            jax.ShapeDtypeStruct((N_TOK, ATTN_W), BF16),
            jax.ShapeDtypeStruct((N_TOK, 2 * KV_W), BF16),
            jax.ShapeDtypeStruct((N_TOK, 2 * KV_W), BF16),
            jax.ShapeDtypeStruct((N_TOK, KV_W), F32),
            jax.ShapeDtypeStruct((N_TOK, KV_W), F32),
            jax.ShapeDtypeStruct((N_TOK, POOL_W), F32),
        ],
        compiler_params=_cparams(("arbitrary",)),
        name="attn_in_proj",
    )(x, mods, g1, w_in, qg, kg, bd, rope_c, rope_s)


def _a2_kernel(x_ref, m_ref, q_ref, kk_ref, vv_ref, p_ref, pp_ref, pn_ref, pw_ref, ps_ref,
               wo_ref, o_ref, *, tiles_per_seq):
    t = pl.program_id(0)
    q = q_ref[...]
    kk = kk_ref[0]
    vv = vv_ref[0]
    lane = lax.broadcasted_iota(jnp.int32, (TM, 128), 1)
    half_mask = (lane < HEAD_DIM, lane >= HEAD_DIM)
    zero_q = jnp.zeros((TM, 128), BF16)
    chunks = []
    for c in range(ATTN_W // 128):
        qc = q[:, c * 128:(c + 1) * 128]
        acc = None
        for half in range(2):
            kv = (2 * c + half) // (N_HEADS // N_KV)
            slab = slice(0, 128) if kv == half else slice(128, 256)
            s = _dot_nt(jnp.where(half_mask[half], qc, zero_q), kk[:, slab])
            e = jnp.exp(s - jnp.max(s, axis=-1, keepdims=True))
            l = jnp.sum(e, axis=-1, keepdims=True)
            o = _dot(e.astype(BF16), vv[:, slab]) / l
            o = jnp.where(half_mask[half], o, 0.0)
            acc = o if acc is None else acc + o
        chunks.append(acc)

    pos = t % tiles_per_seq
    has_prev = jnp.where(pos == 0, 0.0, 1.0)
    has_next = jnp.where(pos == tiles_per_seq - 1, 0.0, 1.0)
    p = p_ref[...]
    pext = jnp.concatenate([pp_ref[...] * has_prev, p, pn_ref[...] * has_next], axis=0)
    rows_ext = TM + 2 * HALO
    trow = lax.broadcasted_iota(jnp.int32, (TM, 128), 0).astype(F32)
    for g, w in enumerate(POOL_WINDOWS):
        sl = slice(g * 128, (g + 1) * 128)
        a = pext[:, sl]
        s = a + pltpu.roll(a, 1, 0)
        half = 1
        while 2 * half < w:
            s = pltpu.roll(s, half, 0) + pltpu.roll(s, rows_ext - half, 0)
            half *= 2
        cnt = (w - jnp.maximum(w // 2 - trow, 0.0) * (1.0 - has_prev)
               - jnp.maximum(trow + (w // 2 - TM), 0.0) * (1.0 - has_next))
        dev = s[HALO:HALO + TM] / cnt - p[:, sl]
        chunks.append(_dot(dev.astype(BF16), pw_ref[g]) * ps_ref[:, sl])
    mix = jnp.concatenate(chunks, axis=1).astype(BF16)
    o_ref[...] = x_ref[...] + m_ref[0, 2:3, :] * _dot(mix, wo_ref[...])


def _attn_out(x, mods, q, kk_seq, vv_seq, p, pool_w, pool_s, w_out, *, first_tile, n_tiles, tiles_per_seq):
    s_len = kk_seq.shape[1]
    off = first_tile
    n_halo_blocks = N_TOK // HALO
    per_tile = TM // HALO
    tile = lambda w: pl.BlockSpec((TM, w), lambda t: (t + off, 0))
    const = lambda shape: pl.BlockSpec(shape, lambda t: (0,) * len(shape))
    seq = pl.BlockSpec((1, s_len, 2 * KV_W), lambda t: (t // tiles_per_seq, 0, 0))
    return pl.pallas_call(
        functools.partial(_a2_kernel, tiles_per_seq=tiles_per_seq),
        grid=(n_tiles,),
        in_specs=[
            tile(D),
            pl.BlockSpec((1, 6, D), lambda t: (_mod_row(t + off, TM), 0, 0)),
            tile(ATTN_W),
            seq,
            seq,
            tile(POOL_W),
            pl.BlockSpec((HALO, POOL_W), lambda t: (jnp.maximum((t + off) * per_tile - 1, 0), 0)),
            pl.BlockSpec((HALO, POOL_W), lambda t: (jnp.minimum((t + off + 1) * per_tile, n_halo_blocks - 1), 0)),
            const((len(POOL_WINDOWS), 128, 128)),
            const((1, POOL_W)),
            const((D, D)),
        ],
        out_specs=tile(D),
        out_shape=jax.ShapeDtypeStruct((N_TOK, D), F32),
        input_output_aliases={0: 0},
        compiler_params=_cparams(("arbitrary",)),
        name="attn_pool_out_%d" % s_len,
    )(x, mods, q, kk_seq, vv_seq, p, p, p, pool_w, pool_s, w_out)


def _conv_kernel(x_ref, xp_ref, xn_ref, m_ref, g1_ref, wi_ref, cw_ref, wo_ref, o_ref):
    t = pl.program_id(0)
    is_prompt = t < N_PROMPT_SEQ
    pos = t & (DEC_SEQ // TM - 1)
    has_prev = jnp.where(is_prompt | (pos == 0), 0.0, 1.0)
    has_next = jnp.where(is_prompt | (pos == DEC_SEQ // TM - 1), 0.0, 1.0)
    x = x_ref[...]
    xext = jnp.concatenate([xp_ref[...], x, xn_ref[...]], axis=0)
    h = _modnorm(xext, g1_ref[...], m_ref[0, 0:1, :], m_ref[0, 1:2, :])
    z = _dot(h.astype(BF16), wi_ref[...])
    gb = z[HALO:HALO + TM, :D]
    u = z[:, D:2 * D] * z[:, 2 * D:]
    row = lax.broadcasted_iota(jnp.int32, (TM + 2 * HALO, 1), 0)
    u = u * jnp.where(row < HALO, has_prev, jnp.where(row >= TM + HALO, has_next, 1.0))
    rows = TM + 2 * HALO
    cw = cw_ref[...]
    y = (pltpu.roll(u, 1, 0)[HALO:HALO + TM] * cw[0:1, :]
         + u[HALO:HALO + TM] * cw[1:2, :]
         + pltpu.roll(u, rows - 1, 0)[HALO:HALO + TM] * cw[2:3, :])
    out = _dot((gb * y).astype(BF16), wo_ref[...])
    o_ref[...] = x + m_ref[0, 2:3, :] * out


def _conv_layer(x, mods, g1, w_in, conv_w, w_out):
    n_tiles = N_TOK // TM
    n_halo_blocks = N_TOK // HALO
    per_tile = TM // HALO
    const = lambda shape: pl.BlockSpec(shape, lambda t: (0,) * len(shape))
    return pl.pallas_call(
        _conv_kernel,
        grid=(n_tiles,),
        in_specs=[
            pl.BlockSpec((TM, D), lambda t: (t, 0)),
            pl.BlockSpec((HALO, D), lambda t: (jnp.maximum(t * per_tile - 1, 0), 0)),
            pl.BlockSpec((HALO, D), lambda t: (jnp.minimum((t + 1) * per_tile, n_halo_blocks - 1), 0)),
            pl.BlockSpec((1, 6, D), lambda t: (_mod_row(t, TM), 0, 0)),
            const((1, D)),
            const((D, 3 * D)),
            const((8, D)),
            const((D, D)),
        ],
        out_specs=pl.BlockSpec((TM, D), lambda t: (t, 0)),
        out_shape=jax.ShapeDtypeStruct((N_TOK, D), F32),
        compiler_params=_cparams(("arbitrary",)),
        name="conv_mixer",
    )(x, x, x, mods, g1, w_in, conv_w, w_out)


NEG_INF = float("-inf")
SUB = 8
NO_ROW = float(1 << 20)


def _sort_network(n):
    def merge(lo, hi, r):
        step = r * 2
        if step < hi - lo:
            yield from merge(lo, hi, step)
            yield from merge(lo + r, hi, step)
            yield from [(i, i + r) for i in range(lo + r, hi - r, step)]
        else:
            yield (lo, lo + r)

    def sort(lo, hi):
        if hi - lo >= 1:
            mid = lo + (hi - lo) // 2
            yield from sort(lo, mid)
            yield from sort(mid + 1, hi)
            yield from merge(lo, hi, 1)

    full = 1 << (n - 1).bit_length()
    return tuple((a, b) for a, b in sort(0, full - 1) if b < n)


def _top16_of_blocks(vs, ts, row_of, n_out=PEER_TOPK):
    vs, ts = list(vs), list(ts)
    for a, b in _sort_network(len(vs)):
        va, vb, ta, tb = vs[a], vs[b], ts[a], ts[b]
        if isinstance(ta, float) and isinstance(tb, float):
            first = (va >= vb) if ta < tb else (va > vb)
        else:
            first = (va > vb) | ((va == vb) & (ta < tb))
        vs[a], vs[b] = jnp.maximum(va, vb), jnp.minimum(va, vb)
        ts[a], ts[b] = jnp.where(first, ta, tb), jnp.where(first, tb, ta)
    vals, rows_out = [], []
    for r in range(n_out):
        rows = row_of(ts[0])
        m = jnp.max(vs[0], axis=0, keepdims=True)
        ix = jnp.min(jnp.where(vs[0] == m, rows, NO_ROW), axis=0, keepdims=True)
        vals.append(m)
        rows_out.append(ix)
        taken = rows == ix
        for k in range(min(len(vs), n_out - r - 1)):
            if k + 1 < len(vs):
                vs[k] = jnp.where(taken, vs[k + 1], vs[k])
                ts[k] = jnp.where(taken, ts[k + 1], ts[k])
            else:
                vs[k] = jnp.where(taken, NEG_INF, vs[k])
    return jnp.concatenate(vals, axis=0), jnp.concatenate(rows_out, axis=0)


def _top16_keys(s, sub):
    n = PEER_NKEYS // SUB
    return _top16_of_blocks([s[SUB * v:SUB * (v + 1), :] for v in range(n)], [float(v) for v in range(n)],
                            lambda blk: blk * float(SUB) + sub)


def _pair_candidates(sv1, sv2, sub):
    a_lo, a_hi, b_lo, b_hi = sv1[0:SUB], sv1[SUB:], sv2[0:SUB], sv2[SUB:]
    row = lambda x, r: x[r:r + 1, :]
    keep = lambda cond, v: jnp.where(cond, v, NEG_INF)
    k = float(PEER_TOPK)
    tail = keep((sub >= 2) & (sub <= 4), row(sv1, 2) + b_lo)
    tail_flat = sub + 2 * k
    for r, (a, b) in ((5, (3, 2)), (6, (4, 2)), (7, (3, 3))):
        tail = jnp.where(sub == r, row(sv1, a) + row(sv2, b), tail)
        tail_flat = jnp.where(sub == r, a * k + b, tail_flat)
    return [
        (row(sv1, 0) + b_lo, sub),
        (row(sv1, 0) + b_hi, sub + SUB),
        (keep(sub >= 1, a_lo + row(sv2, 0)), sub * k),
        (a_hi + row(sv2, 0), (sub + SUB) * k),
        (keep(sub >= 1, row(sv1, 1) + b_lo), sub + k),
        (keep(sub >= 2, a_lo + row(sv2, 1)), sub * k + 1),
        (tail, tail_flat),
    ]


def _route_kernel(x_ref, m_ref, g2_ref, wq_ref, sk_ref, gs_ref, hb_ref, gate_ref,
                  q_scr, i1t_scr, i2t_scr, gtt_scr, i1_scr, i2_scr, gt_scr, gscr):
    @pl.when(pl.program_id(0) == 0)
    def _():
        i1_scr[...] = jnp.zeros_like(i1_scr)
        i2_scr[...] = jnp.zeros_like(i2_scr)
        gt_scr[...] = jnp.zeros_like(gt_scr)

    key = lax.broadcasted_iota(jnp.int32, (PEER_NKEYS, 128), 0).astype(F32)

    def expand_token(n):
        row = pl.ds(n, 1)
        left = jnp.where(key == i1_scr[row, :], gt_scr[row, :], 0.0).astype(BF16)
        right = jnp.where(key == i2_scr[row, :], 1.0, 0.0).astype(BF16)
        gscr[pl.ds(n, PEER_NKEYS, stride=G_PITCH), :] = _dot_nt(left, right)

    def copy_out(i, carry):
        rows = gscr[pl.ds(pl.multiple_of(i * G_PITCH, 8), TR), :]
        gate_ref[i] = (rows * gs_ref[pl.ds(i, 1), :]).astype(BF16)
        return carry

    last = pl.num_programs(0) - 1

    @pl.when(pl.program_id(0) == last)
    def _():
        hb_ref[...] = _modnorm(x_ref[...], g2_ref[...], m_ref[0, 3:4, :], m_ref[0, 4:5, :]).astype(BF16)

        def expand_group(j, carry):
            for k in range(16):
                expand_token(j * 16 + k)
            return carry

        lax.fori_loop(0, TR // 16, expand_group, 0)
        lax.fori_loop(0, PEER_NKEYS, copy_out, 0, unroll=4)

    @pl.when(pl.program_id(0) < last)
    def _():
        _route_select(x_ref, m_ref, g2_ref, wq_ref, sk_ref, hb_ref, q_scr, i1t_scr, i2t_scr, gtt_scr, expand_token)
        lax.fori_loop(0, PEER_NKEYS, copy_out, 0, unroll=4)
        i1_scr[...] = i1t_scr[...].T
        i2_scr[...] = i2t_scr[...].T
        gt_scr[...] = gtt_scr[...].T


def _route_select(x_ref, m_ref, g2_ref, wq_ref, sk_ref, hb_ref, q_scr, i1t_scr, i2t_scr, gtt_scr, expand_token):
    h = _modnorm(x_ref[...], g2_ref[...], m_ref[0, 3:4, :], m_ref[0, 4:5, :])
    hb = h.astype(BF16)
    hb_ref[...] = hb
    q = _dot(hb, wq_ref[...])
    for hp in range(2 * PEER_HEADS):
        q_scr[hp] = q[:, hp * PEER_DK:(hp + 1) * PEER_DK]
    sub = lax.broadcasted_iota(jnp.int32, (SUB, TR), 0).astype(F32)
    sk_split = [_split_bf16(sk_ref[p]) for p in range(2)]

    def key_stage(hd):
        top = ()
        for p in range(2):
            q_hi, q_lo = _split_bf16(q_scr[2 * hd + p])
            k_hi, k_lo = sk_split[p]
            s = _dot_nt(k_hi, q_hi) + (_dot_nt(k_hi, q_lo) + _dot_nt(k_lo, q_hi))
            top += _top16_keys(s, sub)
        return top

    def pair_stage(hd, top):
        sv1, si1, sv2, si2 = top
        cands = _pair_candidates(sv1, sv2, sub)
        fv, ff = _top16_of_blocks([c[0] for c in cands], [c[1] for c in cands], lambda flat: flat)
        ff = ff.astype(jnp.int32)
        fa, fb = ff >> 4, ff & (PEER_TOPK - 1)
        i1 = jnp.zeros((PEER_TOPK, TR), F32)
        i2 = jnp.zeros((PEER_TOPK, TR), F32)
        for a in range(PEER_TOPK):
            i1 = jnp.where(fa == a, si1[a:a + 1, :], i1)
            i2 = jnp.where(fb == a, si2[a:a + 1, :], i2)
        e = jnp.exp(fv - fv[0:1, :])
        g = e / jnp.sum(e, axis=0, keepdims=True)
        start = hd * PEER_TOPK
        rows = pl.ds(start if isinstance(start, int) else pl.multiple_of(start, PEER_TOPK), PEER_TOPK)
        i1t_scr[rows, :] = i1
        i2t_scr[rows, :] = i2
        gtt_scr[rows, :] = g

    n_first, n_mid = TR // 8, (TR - TR // 8) // 8
    top = key_stage(0)
    for k in range(n_first):
        expand_token(k)

    def per_head(hd, top_prev):
        top_cur = key_stage(hd)
        pair_stage(hd - 1, top_prev)
        for k in range(n_mid):
            expand_token(n_first + (hd - 1) * n_mid + k)
        return top_cur

    top = lax.fori_loop(1, PEER_HEADS, per_head, top)
    pair_stage(PEER_HEADS - 1, top)
    for n in range(n_first + (PEER_HEADS - 1) * n_mid, TR):
        expand_token(n)


def _peer_route(x, mods, g2, wq, subkeys, gate_scale):
    n_tiles = N_TOK // TR
    const = lambda shape: pl.BlockSpec(shape, lambda t: (0,) * len(shape))
    cur = lambda t: jnp.minimum(t, n_tiles - 1)
    return pl.pallas_call(
        _route_kernel,
        grid=(n_tiles + 1,),
        in_specs=[
            pl.BlockSpec((TR, D), lambda t: (cur(t), 0)),
            pl.BlockSpec((1, 6, D), lambda t: (_mod_row(cur(t), TR), 0, 0)),
            const((1, D)),
            const((D, PEER_HEADS * 2 * PEER_DK)),
            const((2, PEER_NKEYS, PEER_DK)),
            const((PEER_NKEYS, PEER_NKEYS)),
        ],
        out_specs=[
            pl.BlockSpec((TR, D), lambda t: (cur(t), 0)),
            pl.BlockSpec((PEER_NKEYS, TR, PEER_NKEYS), lambda t: (0, jnp.maximum(t - 1, 0), 0)),
        ],
        out_shape=[
            jax.ShapeDtypeStruct((N_TOK, D), BF16),
            jax.ShapeDtypeStruct((PEER_NKEYS, N_TOK, PEER_NKEYS), BF16),
        ],
        scratch_shapes=[
            pltpu.VMEM((2 * PEER_HEADS, TR, PEER_DK), F32),
            pltpu.VMEM((PEER_HEADS * PEER_TOPK, TR), F32),
            pltpu.VMEM((PEER_HEADS * PEER_TOPK, TR), F32),
            pltpu.VMEM((PEER_HEADS * PEER_TOPK, TR), F32),
            pltpu.VMEM((TR, PEER_HEADS * PEER_TOPK), F32),
            pltpu.VMEM((TR, PEER_HEADS * PEER_TOPK), F32),
            pltpu.VMEM((TR, PEER_HEADS * PEER_TOPK), F32),
            pltpu.VMEM((PEER_NKEYS * G_PITCH, PEER_NKEYS), F32),
        ],
        compiler_params=_cparams(("arbitrary",)),
        name="peer_route",
    )(x, mods, g2, wq, subkeys, gate_scale)


F8 = jnp.float8_e4m3fn
F8_TOP = 256.0
TINY = 1e-30


def _row_scale(a):
    return jnp.maximum(jnp.max(jnp.abs(a), axis=1, keepdims=True), TINY) * (1.0 / F8_TOP)


def _quant_rows_kernel(w_ref, q_ref, s_ref):
    w = w_ref[0]
    scale = _row_scale(w)
    q_ref[0] = (w * (1.0 / scale)).astype(F8)
    s_ref[0, 0] = jnp.transpose(jnp.broadcast_to(scale, (EC, 128)))[0:1, :]


def _quantize_rows(w_all):
    n_chunks = PEER_EXPERTS // EC
    return pl.pallas_call(
        _quant_rows_kernel,
        grid=(DEPTH, n_chunks),
        in_specs=[pl.BlockSpec((1, EC, D), lambda l, j: (l, j, 0))],
        out_specs=[pl.BlockSpec((1, EC, D), lambda l, j: (l, j, 0)),
                   pl.BlockSpec((1, 1, 1, EC), lambda l, j: (l, j, 0, 0))],
        out_shape=[jax.ShapeDtypeStruct((DEPTH, PEER_EXPERTS, D), F8),
                   jax.ShapeDtypeStruct((DEPTH, n_chunks, 1, EC), F32)],
        compiler_params=_cparams(("arbitrary", "arbitrary")),
        name="quantize_rows",
    )(w_all)


def _expert_kernel(x_ref, m_ref, hb_ref, gate_ref, u8_ref, su_ref, v8_ref, *rest):
    o_ref, h8_scr, t_scr = rest[-3:]
    c = pl.program_id(1)

    @pl.when(c == 0)
    def _():
        o_ref[...] = jnp.zeros_like(o_ref)
        hf = hb_ref[...].astype(F32)
        t = jnp.maximum(jnp.max(jnp.abs(hf), keepdims=True), TINY) * (1.0 / F8_TOP)
        h8_scr[...] = (hf * (1.0 / t)).astype(F8)
        t_scr[...] = t

    h8 = h8_scr[...]
    scale = su_ref[0, 0] * (t_scr[...] * (2.0 ** -0.5))
    slabs = []
    for s in range(EC // EC_SUB):
        cols = slice(s * EC_SUB, (s + 1) * EC_SUB)
        z = _dot_nt(h8, u8_ref[0, cols, :]) * scale[:, cols]
        act = z * (1.0 + lax.erf(z))
        for a in range(EC_SUB // PEER_NKEYS):
            gate = gate_ref[s * (EC_SUB // PEER_NKEYS) + a].astype(F32)
            slabs.append(act[:, a * PEER_NKEYS:(a + 1) * PEER_NKEYS] * gate)
    w = jnp.concatenate(slabs, axis=1)
    r = _row_scale(w)
    o_ref[...] += _dot((w * (1.0 / r)).astype(F8), v8_ref[0]) * r

    @pl.when(c == pl.num_programs(1) - 1)
    def _():
        y = x_ref[...] + m_ref[0, 5:6, :] * o_ref[...]
        if len(rest) == 4:
            y = y * lax.rsqrt(jnp.mean(y * y, axis=-1, keepdims=True) + EPS) * rest[0][...]
        o_ref[...] = y


def _peer_experts(x, mods, hb, gate, u8_all, su_all, v8_all, layer, final_g=None):
    n_tiles = N_TOK // TE
    n_chunks = PEER_EXPERTS // EC
    once = pl.Buffered(1)
    in_specs = [
        pl.BlockSpec((TE, D), lambda t, c: (t, 0), pipeline_mode=once),
        pl.BlockSpec((1, 6, D), lambda t, c: (_mod_row(t, TE), 0, 0)),
        pl.BlockSpec((TE, D), lambda t, c: (t, 0), pipeline_mode=once),
        pl.BlockSpec((EC // PEER_NKEYS, TE, PEER_NKEYS), lambda t, c: (c, t, 0)),
        pl.BlockSpec((1, EC, D), lambda t, c: (layer, c, 0)),
        pl.BlockSpec((1, 1, 1, EC), lambda t, c: (layer, c, 0, 0)),
        pl.BlockSpec((1, EC, D), lambda t, c: (layer, c, 0)),
    ]
    operands = [x, mods, hb, gate, u8_all, su_all, v8_all]
    if final_g is not None:
        in_specs.append(pl.BlockSpec((1, D), lambda t, c: (0, 0)))
        operands.append(final_g)
    return pl.pallas_call(
        _expert_kernel,
        grid=(n_tiles, n_chunks),
        in_specs=in_specs,
        out_specs=pl.BlockSpec((TE, D), lambda t, c: (t, 0)),
        out_shape=jax.ShapeDtypeStruct((N_TOK, D), F32),
        scratch_shapes=[pltpu.VMEM((TE, D), F8), pltpu.VMEM((1, 1), F32)],
        compiler_params=_cparams(("arbitrary", "arbitrary")),
        name="peer_experts",
    )(*operands)


def _rope_tables():
    t = jnp.arange(DEC_SEQ)
    row, col = t // GRID_W, t % GRID_W
    nf = HEAD_DIM // 4
    inv = ROPE_THETA ** (-jnp.arange(nf, dtype=F32) / nf)
    ang_r = row.astype(F32)[:, None] * inv[None, :]
    ang_c = col.astype(F32)[:, None] * inv[None, :]
    cos = jnp.concatenate([jnp.cos(ang_r)] * 2 + [jnp.cos(ang_c)] * 2, axis=1)
    sin = jnp.concatenate([-jnp.sin(ang_r), jnp.sin(ang_r), -jnp.sin(ang_c), jnp.sin(ang_c)], axis=1)
    cos = jnp.concatenate([jnp.ones((TM, HEAD_DIM), F32), cos], axis=0)
    sin = jnp.concatenate([jnp.zeros((TM, HEAD_DIM), F32), sin], axis=0)
    return jnp.tile(cos, (1, N_HEADS)), jnp.tile(sin, (1, N_HEADS))


def _head_block_diag():
    i = np.arange(128)
    return jnp.asarray((i[:, None] // HEAD_DIM == i[None, :] // HEAD_DIM).astype(np.float32), dtype=BF16)


def kernel(x_prompt, x_sample, cache_k, cache_v, c, c_ctx, mod_w, mod_b, norm1_g, norm2_g, a_in_w, a_out_w,
           q_norm_g, k_norm_g, pool_w, pool_scale, c_in_w, c_conv_w, c_out_w, peer_wq, peer_subkeys, peer_u,
           peer_v, final_g):
    x = jnp.concatenate([x_prompt.reshape(NP_TOK, D), x_sample.reshape(NS_TOK, D)], axis=0)
    ct = jnp.zeros((D, 8), F32).at[:, 0].set(c_ctx).at[:, 1:3].set(c.T)
    mods = _modulation(ct, mod_w, mod_b).reshape(DEPTH, 8, 6, D)
    rope_c, rope_s = _rope_tables()
    bd = _head_block_diag()
    wq = peer_wq.astype(BF16)
    u8, su = _quantize_rows(peer_u)
    v8, sv = _quantize_rows(peer_v)
    new_k, new_v = [], []
    for l in range(DEPTH):
        g1 = norm1_g[l].reshape(1, D)
        if l % 2 == 0:
            i = l // 2
            q, kk, vv, knew, vnew, p = _attn_in(
                x, mods[l], g1, a_in_w[i].astype(BF16), jnp.tile(q_norm_g[i], N_HEADS).reshape(1, ATTN_W),
                jnp.tile(k_norm_g[i], N_KV).reshape(1, KV_W), bd, rope_c, rope_s)
            new_k.append(knew[:NP_TOK].reshape(N_PROMPT_SEQ, SEQ, N_KV, HEAD_DIM))
            new_v.append(vnew[:NP_TOK].reshape(N_PROMPT_SEQ, SEQ, N_KV, HEAD_DIM))
            ck = cache_k[:, i]
            cv = cache_v[:, i]
            ctx_kk = jnp.concatenate([ck, ck[:, :, ::-1]], axis=2).reshape(N_SAMPLE_SEQ, PAST, 2 * KV_W).astype(BF16)
            ctx_vv = jnp.concatenate([cv, cv[:, :, ::-1]], axis=2).reshape(N_SAMPLE_SEQ, PAST, 2 * KV_W).astype(BF16)
            kk_p = kk[:NP_TOK].reshape(N_PROMPT_SEQ, SEQ, 2 * KV_W)
            vv_p = vv[:NP_TOK].reshape(N_PROMPT_SEQ, SEQ, 2 * KV_W)
            kk_s = jnp.concatenate([kk[NP_TOK:].reshape(N_SAMPLE_SEQ, DEC_SEQ, 2 * KV_W), ctx_kk], axis=1)
            vv_s = jnp.concatenate([vv[NP_TOK:].reshape(N_SAMPLE_SEQ, DEC_SEQ, 2 * KV_W), ctx_vv], axis=1)
            common = (pool_w[i].astype(BF16), pool_scale[i].reshape(1, POOL_W), a_out_w[i].astype(BF16))
            x = _attn_out(x, mods[l], q, kk_p, vv_p, p, *common,
                          first_tile=0, n_tiles=N_PROMPT_SEQ, tiles_per_seq=SEQ // TM)
            x = _attn_out(x, mods[l], q, kk_s, vv_s, p, *common,
                          first_tile=N_PROMPT_SEQ, n_tiles=NS_TOK // TM, tiles_per_seq=DEC_SEQ // TM)
        else:
            j = l // 2
            cw = jnp.zeros((8, D), F32).at[:3].set(c_conv_w[j])
            x = _conv_layer(x, mods[l], g1, c_in_w[j].astype(BF16), cw, c_out_w[j].astype(BF16))
        gate_scale = (2.0 ** -0.5) * sv[l].reshape(PEER_NKEYS, PEER_NKEYS)
        hb, gate = _peer_route(x, mods[l], norm2_g[l].reshape(1, D), wq[l], peer_subkeys[l], gate_scale)
        x = _peer_experts(x, mods[l], hb, gate, u8, su, v8, l,
                          final_g=final_g.reshape(1, D) if l == DEPTH - 1 else None)
    y = x
    y_prompt = y[:NP_TOK].reshape(N_PROMPT_SEQ, SEQ, D)
    y_sample = y[NP_TOK:].reshape(N_SAMPLE_SEQ, DEC_SEQ, D)
    return (y_prompt, y_sample, jnp.stack(new_k, axis=1), jnp.stack(new_v, axis=1))
```

```python
import functools

import numpy as np
import jax
import jax.numpy as jnp
from jax import lax
from jax.experimental import pallas as pl
from jax.experimental.pallas import tpu as pltpu

F32 = jnp.float32
BF16 = jnp.bfloat16

D = 1024
N_PROMPT_SEQ, SEQ = 32, 256
N_SAMPLE_SEQ, DEC_SEQ = 2, 2048
PAST = 256
NP_TOK = N_PROMPT_SEQ * SEQ
NS_TOK = N_SAMPLE_SEQ * DEC_SEQ
N_TOK = NP_TOK + NS_TOK
DEPTH = 4
GRID_W = 64
N_HEADS, N_KV, HEAD_DIM = 8, 2, 64
ATTN_W, KV_W, POOL_W = 512, 128, 512
POOL_WINDOWS = (2, 4, 8, 16)
A_IN_W = ATTN_W + 2 * KV_W + POOL_W
PEER_HEADS, PEER_NKEYS, PEER_TOPK, PEER_DK = 8, 128, 16, 128
PEER_EXPERTS = PEER_NKEYS * PEER_NKEYS
ROPE_THETA = 10000.0
EPS = 1e-6

TM = 256
HALO = 8
TR = 256
G_PITCH = TR + 8
TE = 1024
EC = 2048
EC_SUB = 512
V7X_VMEM_BYTES = 64 * 1024 * 1024
VMEM_LIMIT = V7X_VMEM_BYTES - 8 * 1024 * 1024


def _cparams(sem):
    return pltpu.CompilerParams(dimension_semantics=sem, vmem_limit_bytes=VMEM_LIMIT)


def _mod_row(tile, tile_tokens):
    start = tile * tile_tokens
    return (start >= NP_TOK).astype(jnp.int32) + (start >= NP_TOK + DEC_SEQ).astype(jnp.int32)


def _modnorm(x, g, shift, scale):
    ms = jnp.mean(x * x, axis=-1, keepdims=True)
    y = x * lax.rsqrt(ms + EPS) * g
    return y * (1.0 + scale) + shift


def _split_bf16(a):
    hi = a.astype(BF16)
    lo = (a - hi.astype(F32)).astype(BF16)
    return hi, lo


def _dot(a, b):
    return jnp.dot(a, b, preferred_element_type=F32)


def _dot_nt(a, b):
    return lax.dot_general(a, b, (((1,), (1,)), ((), ())), preferred_element_type=F32)


def _mod_kernel(ct_ref, w_ref, b_ref, o_ref):
    ct = ct_ref[...]
    s = ct * jax.nn.sigmoid(ct)
    w = w_ref[0]
    rows = [jnp.sum(s[:, r:r + 1] * w, axis=0, keepdims=True) + b_ref[0] for r in range(3)]
    rows.append(jnp.zeros((5, w.shape[1]), F32))
    o_ref[0] = jnp.concatenate(rows, axis=0)


def _modulation(ct, mod_w, mod_b):
    tn = 1536
    return pl.pallas_call(
        _mod_kernel,
        grid=(DEPTH, 6 * D // tn),
        in_specs=[
            pl.BlockSpec((D, 8), lambda l, j: (0, 0)),
            pl.BlockSpec((1, D, tn), lambda l, j: (l, 0, j)),
            pl.BlockSpec((1, 1, tn), lambda l, j: (l, 0, j)),
        ],
        out_specs=pl.BlockSpec((1, 8, tn), lambda l, j: (l, 0, j)),
        out_shape=jax.ShapeDtypeStruct((DEPTH, 8, 6 * D), F32),
        compiler_params=_cparams(("arbitrary", "arbitrary")),
        name="adaln_modulation",
    )(ct, mod_w, mod_b.reshape(DEPTH, 1, 6 * D))


def _head_norm(x, g, bd):
    outs = []
    for c in range(x.shape[1] // 128):
        xc = x[:, c * 128:(c + 1) * 128]
        hi, lo = _split_bf16(xc * xc)
        ms = (_dot(hi, bd) + _dot(lo, bd)) * (1.0 / HEAD_DIM)
        outs.append(xc * lax.rsqrt(ms + EPS))
    return jnp.concatenate(outs, axis=1) * g


def _rope(x, cos, sin, low16):
    outs = []
    for c in range(x.shape[1] // 128):
        sl = slice(c * 128, (c + 1) * 128)
        xc = x[:, sl]
        partner = jnp.where(low16, pltpu.roll(xc, 112, 1), pltpu.roll(xc, 16, 1))
        outs.append(xc * cos[:, sl] + partner * sin[:, sl])
    return jnp.concatenate(outs, axis=1)


def _a1_kernel(x_ref, m_ref, g1_ref, w_ref, qg_ref, kg_ref, bd_ref, rc_ref, rs_ref,
               q_ref, kk_ref, vv_ref, knew_ref, vnew_ref, p_ref):
    h = _modnorm(x_ref[...], g1_ref[...], m_ref[0, 0:1, :], m_ref[0, 1:2, :])
    z = _dot(h.astype(BF16), w_ref[...])
    bd = bd_ref[...]
    q = _head_norm(z[:, :ATTN_W], qg_ref[...], bd)
    k = _head_norm(z[:, ATTN_W:ATTN_W + KV_W], kg_ref[...], bd)
    v = z[:, ATTN_W + KV_W:ATTN_W + 2 * KV_W]
    knew_ref[...] = k
    vnew_ref[...] = v
    p_ref[...] = z[:, ATTN_W + 2 * KV_W:]
    cos, sin = rc_ref[...], rs_ref[...]
    lane = lax.broadcasted_iota(jnp.int32, (TM, 128), 1)
    low16 = (lane & 31) < 16
    q = _rope(q, cos, sin, low16) * (HEAD_DIM ** -0.5)
    k = _rope(k, cos, sin, low16)
    q_ref[...] = q.astype(BF16)
    kk_ref[...] = jnp.concatenate([k, pltpu.roll(k, 64, 1)], axis=1).astype(BF16)
    vv_ref[...] = jnp.concatenate([v, pltpu.roll(v, 64, 1)], axis=1).astype(BF16)


def _rope_block(t):
    return jnp.where(t < N_PROMPT_SEQ, 0, 1 + (t - N_PROMPT_SEQ) % (DEC_SEQ // TM))


def _attn_in(x, mods, g1, w_in, qg, kg, bd, rope_c, rope_s):
    n_tiles = N_TOK // TM
    tile = lambda w: pl.BlockSpec((TM, w), lambda t: (t, 0))
    const = lambda shape: pl.BlockSpec(shape, lambda t: (0,) * len(shape))
    return pl.pallas_call(
        _a1_kernel,
        grid=(n_tiles,),
        in_specs=[
            tile(D),
            pl.BlockSpec((1, 6, D), lambda t: (_mod_row(t, TM), 0, 0)),
            const((1, D)),
            const((D, A_IN_W)),
            const((1, ATTN_W)),
            const((1, KV_W)),
            const((128, 128)),
            pl.BlockSpec((TM, ATTN_W), lambda t: (_rope_block(t), 0)),
            pl.BlockSpec((TM, ATTN_W), lambda t: (_rope_block(t), 0)),
        ],
        out_specs=[tile(ATTN_W), tile(2 * KV_W), tile(2 * KV_W), tile(KV_W), tile(KV_W), tile(POOL_W)],
        out_shape=[
            jax.ShapeDtypeStruct((N_TOK, ATTN_W), BF16),
            jax.ShapeDtypeStruct((N_TOK, 2 * KV_W), BF16),
            jax.ShapeDtypeStruct((N_TOK, 2 * KV_W), BF16),
            jax.ShapeDtypeStruct((N_TOK, KV_W), F32),
            jax.ShapeDtypeStruct((N_TOK, KV_W), F32),
            jax.ShapeDtypeStruct((N_TOK, POOL_W), F32),
        ],
        compiler_params=_cparams(("arbitrary",)),
        name="attn_in_proj",
    )(x, mods, g1, w_in, qg, kg, bd, rope_c, rope_s)


def _a2_kernel(x_ref, m_ref, q_ref, kk_ref, vv_ref, p_ref, pp_ref, pn_ref, pw_ref, ps_ref,
               wo_ref, o_ref, *, tiles_per_seq):
    t = pl.program_id(0)
    q = q_ref[...]
    kk = kk_ref[0]
    vv = vv_ref[0]
    lane = lax.broadcasted_iota(jnp.int32, (TM, 128), 1)
    half_mask = (lane < HEAD_DIM, lane >= HEAD_DIM)
    zero_q = jnp.zeros((TM, 128), BF16)
    chunks = []
    for c in range(ATTN_W // 128):
        qc = q[:, c * 128:(c + 1) * 128]
        acc = None
        for half in range(2):
            kv = (2 * c + half) // (N_HEADS // N_KV)
            slab = slice(0, 128) if kv == half else slice(128, 256)
            s = _dot_nt(jnp.where(half_mask[half], qc, zero_q), kk[:, slab])
            e = jnp.exp(s - jnp.max(s, axis=-1, keepdims=True))
            l = jnp.sum(e, axis=-1, keepdims=True)
            o = _dot(e.astype(BF16), vv[:, slab]) / l
            o = jnp.where(half_mask[half], o, 0.0)
            acc = o if acc is None else acc + o
        chunks.append(acc)

    pos = t % tiles_per_seq
    has_prev = jnp.where(pos == 0, 0.0, 1.0)
    has_next = jnp.where(pos == tiles_per_seq - 1, 0.0, 1.0)
    p = p_ref[...]
    pext = jnp.concatenate([pp_ref[...] * has_prev, p, pn_ref[...] * has_next], axis=0)
    rows_ext = TM + 2 * HALO
    trow = lax.broadcasted_iota(jnp.int32, (TM, 128), 0).astype(F32)
    for g, w in enumerate(POOL_WINDOWS):
        sl = slice(g * 128, (g + 1) * 128)
        a = pext[:, sl]
        s = a + pltpu.roll(a, 1, 0)
        half = 1
        while 2 * half < w:
            s = pltpu.roll(s, half, 0) + pltpu.roll(s, rows_ext - half, 0)
            half *= 2
        cnt = (w - jnp.maximum(w // 2 - trow, 0.0) * (1.0 - has_prev)
               - jnp.maximum(trow + (w // 2 - TM), 0.0) * (1.0 - has_next))
        dev = s[HALO:HALO + TM] / cnt - p[:, sl]
        chunks.append(_dot(dev.astype(BF16), pw_ref[g]) * ps_ref[:, sl])
    mix = jnp.concatenate(chunks, axis=1).astype(BF16)
    o_ref[...] = x_ref[...] + m_ref[0, 2:3, :] * _dot(mix, wo_ref[...])


def _attn_out(x, mods, q, kk_seq, vv_seq, p, pool_w, pool_s, w_out, *, first_tile, n_tiles, tiles_per_seq):
    s_len = kk_seq.shape[1]
    off = first_tile
    n_halo_blocks = N_TOK // HALO
    per_tile = TM // HALO
    tile = lambda w: pl.BlockSpec((TM, w), lambda t: (t + off, 0))
    const = lambda shape: pl.BlockSpec(shape, lambda t: (0,) * len(shape))
    seq = pl.BlockSpec((1, s_len, 2 * KV_W), lambda t: (t // tiles_per_seq, 0, 0))
    return pl.pallas_call(
        functools.partial(_a2_kernel, tiles_per_seq=tiles_per_seq),
        grid=(n_tiles,),
        in_specs=[
            tile(D),
            pl.BlockSpec((1, 6, D), lambda t: (_mod_row(t + off, TM), 0, 0)),
            tile(ATTN_W),
            seq,
            seq,
            tile(POOL_W),
            pl.BlockSpec((HALO, POOL_W), lambda t: (jnp.maximum((t + off) * per_tile - 1, 0), 0)),
            pl.BlockSpec((HALO, POOL_W), lambda t: (jnp.minimum((t + off + 1) * per_tile, n_halo_blocks - 1), 0)),
            const((len(POOL_WINDOWS), 128, 128)),
            const((1, POOL_W)),
            const((D, D)),
        ],
        out_specs=tile(D),
        out_shape=jax.ShapeDtypeStruct((N_TOK, D), F32),
        input_output_aliases={0: 0},
        compiler_params=_cparams(("arbitrary",)),
        name="attn_pool_out_%d" % s_len,
    )(x, mods, q, kk_seq, vv_seq, p, p, p, pool_w, pool_s, w_out)


def _conv_kernel(x_ref, xp_ref, xn_ref, m_ref, g1_ref, wi_ref, cw_ref, wo_ref, o_ref):
    t = pl.program_id(0)
    is_prompt = t < N_PROMPT_SEQ
    pos = t & (DEC_SEQ // TM - 1)
    has_prev = jnp.where(is_prompt | (pos == 0), 0.0, 1.0)
    has_next = jnp.where(is_prompt | (pos == DEC_SEQ // TM - 1), 0.0, 1.0)
    x = x_ref[...]
    xext = jnp.concatenate([xp_ref[...], x, xn_ref[...]], axis=0)
    h = _modnorm(xext, g1_ref[...], m_ref[0, 0:1, :], m_ref[0, 1:2, :])
    z = _dot(h.astype(BF16), wi_ref[...])
    gb = z[HALO:HALO + TM, :D]
    u = z[:, D:2 * D] * z[:, 2 * D:]
    row = lax.broadcasted_iota(jnp.int32, (TM + 2 * HALO, 1), 0)
    u = u * jnp.where(row < HALO, has_prev, jnp.where(row >= TM + HALO, has_next, 1.0))
    rows = TM + 2 * HALO
    cw = cw_ref[...]
    y = (pltpu.roll(u, 1, 0)[HALO:HALO + TM] * cw[0:1, :]
         + u[HALO:HALO + TM] * cw[1:2, :]
         + pltpu.roll(u, rows - 1, 0)[HALO:HALO + TM] * cw[2:3, :])
    out = _dot((gb * y).astype(BF16), wo_ref[...])
    o_ref[...] = x + m_ref[0, 2:3, :] * out


def _conv_layer(x, mods, g1, w_in, conv_w, w_out):
    n_tiles = N_TOK // TM
    n_halo_blocks = N_TOK // HALO
    per_tile = TM // HALO
    const = lambda shape: pl.BlockSpec(shape, lambda t: (0,) * len(shape))
    return pl.pallas_call(
        _conv_kernel,
        grid=(n_tiles,),
        in_specs=[
            pl.BlockSpec((TM, D), lambda t: (t, 0)),
            pl.BlockSpec((HALO, D), lambda t: (jnp.maximum(t * per_tile - 1, 0), 0)),
            pl.BlockSpec((HALO, D), lambda t: (jnp.minimum((t + 1) * per_tile, n_halo_blocks - 1), 0)),
            pl.BlockSpec((1, 6, D), lambda t: (_mod_row(t, TM), 0, 0)),
            const((1, D)),
            const((D, 3 * D)),
            const((8, D)),
            const((D, D)),
        ],
        out_specs=pl.BlockSpec((TM, D), lambda t: (t, 0)),
        out_shape=jax.ShapeDtypeStruct((N_TOK, D), F32),
        compiler_params=_cparams(("arbitrary",)),
        name="conv_mixer",
    )(x, x, x, mods, g1, w_in, conv_w, w_out)


NEG_INF = float("-inf")
SUB = 8
NO_ROW = float(1 << 20)


def _sort_network(n):
    def merge(lo, hi, r):
        step = r * 2
        if step < hi - lo:
            yield from merge(lo, hi, step)
            yield from merge(lo + r, hi, step)
            yield from [(i, i + r) for i in range(lo + r, hi - r, step)]
        else:
            yield (lo, lo + r)

    def sort(lo, hi):
        if hi - lo >= 1:
            mid = lo + (hi - lo) // 2
            yield from sort(lo, mid)
            yield from sort(mid + 1, hi)
            yield from merge(lo, hi, 1)

    full = 1 << (n - 1).bit_length()
    return tuple((a, b) for a, b in sort(0, full - 1) if b < n)


def _top16_of_blocks(vs, ts, row_of, n_out=PEER_TOPK):
    vs, ts = list(vs), list(ts)
    for a, b in _sort_network(len(vs)):
        va, vb, ta, tb = vs[a], vs[b], ts[a], ts[b]
        if isinstance(ta, float) and isinstance(tb, float):
            first = (va >= vb) if ta < tb else (va > vb)
        else:
            first = (va > vb) | ((va == vb) & (ta < tb))
        vs[a], vs[b] = jnp.maximum(va, vb), jnp.minimum(va, vb)
        ts[a], ts[b] = jnp.where(first, ta, tb), jnp.where(first, tb, ta)
    vals, rows_out = [], []
    for r in range(n_out):
        rows = row_of(ts[0])
        m = jnp.max(vs[0], axis=0, keepdims=True)
        ix = jnp.min(jnp.where(vs[0] == m, rows, NO_ROW), axis=0, keepdims=True)
        vals.append(m)
        rows_out.append(ix)
        taken = rows == ix
        for k in range(min(len(vs), n_out - r - 1)):
            if k + 1 < len(vs):
                vs[k] = jnp.where(taken, vs[k + 1], vs[k])
                ts[k] = jnp.where(taken, ts[k + 1], ts[k])
            else:
                vs[k] = jnp.where(taken, NEG_INF, vs[k])
    return jnp.concatenate(vals, axis=0), jnp.concatenate(rows_out, axis=0)


def _top16_keys(s, sub):
    n = PEER_NKEYS // SUB
    return _top16_of_blocks([s[SUB * v:SUB * (v + 1), :] for v in range(n)], [float(v) for v in range(n)],
                            lambda blk: blk * float(SUB) + sub)


def _pair_candidates(sv1, sv2, sub):
    a_lo, a_hi, b_lo, b_hi = sv1[0:SUB], sv1[SUB:], sv2[0:SUB], sv2[SUB:]
    row = lambda x, r: x[r:r + 1, :]
    keep = lambda cond, v: jnp.where(cond, v, NEG_INF)
    k = float(PEER_TOPK)
    tail = keep((sub >= 2) & (sub <= 4), row(sv1, 2) + b_lo)
    tail_flat = sub + 2 * k
    for r, (a, b) in ((5, (3, 2)), (6, (4, 2)), (7, (3, 3))):
        tail = jnp.where(sub == r, row(sv1, a) + row(sv2, b), tail)
        tail_flat = jnp.where(sub == r, a * k + b, tail_flat)
    return [
        (row(sv1, 0) + b_lo, sub),
        (row(sv1, 0) + b_hi, sub + SUB),
        (keep(sub >= 1, a_lo + row(sv2, 0)), sub * k),
        (a_hi + row(sv2, 0), (sub + SUB) * k),
        (keep(sub >= 1, row(sv1, 1) + b_lo), sub + k),
        (keep(sub >= 2, a_lo + row(sv2, 1)), sub * k + 1),
        (tail, tail_flat),
    ]


def _route_kernel(x_ref, m_ref, g2_ref, wq_ref, sk_ref, gs_ref, hb_ref, gate_ref,
                  q_scr, i1t_scr, i2t_scr, gtt_scr, i1_scr, i2_scr, gt_scr, gscr):
    @pl.when(pl.program_id(0) == 0)
    def _():
        i1_scr[...] = jnp.zeros_like(i1_scr)
        i2_scr[...] = jnp.zeros_like(i2_scr)
        gt_scr[...] = jnp.zeros_like(gt_scr)

    key = lax.broadcasted_iota(jnp.int32, (PEER_NKEYS, 128), 0).astype(F32)

    def expand_token(n):
        row = pl.ds(n, 1)
        left = jnp.where(key == i1_scr[row, :], gt_scr[row, :], 0.0).astype(BF16)
        right = jnp.where(key == i2_scr[row, :], 1.0, 0.0).astype(BF16)
        gscr[pl.ds(n, PEER_NKEYS, stride=G_PITCH), :] = _dot_nt(left, right)

    def copy_out(i, carry):
        rows = gscr[pl.ds(pl.multiple_of(i * G_PITCH, 8), TR), :]
        gate_ref[i] = (rows * gs_ref[pl.ds(i, 1), :]).astype(BF16)
        return carry

    last = pl.num_programs(0) - 1

    @pl.when(pl.program_id(0) == last)
    def _():
        hb_ref[...] = _modnorm(x_ref[...], g2_ref[...], m_ref[0, 3:4, :], m_ref[0, 4:5, :]).astype(BF16)

        def expand_group(j, carry):
            for k in range(16):
                expand_token(j * 16 + k)
            return carry

        lax.fori_loop(0, TR // 16, expand_group, 0)
        lax.fori_loop(0, PEER_NKEYS, copy_out, 0, unroll=4)

    @pl.when(pl.program_id(0) < last)
    def _():
        _route_select(x_ref, m_ref, g2_ref, wq_ref, sk_ref, hb_ref, q_scr, i1t_scr, i2t_scr, gtt_scr, expand_token)
        lax.fori_loop(0, PEER_NKEYS, copy_out, 0, unroll=4)
        i1_scr[...] = i1t_scr[...].T
        i2_scr[...] = i2t_scr[...].T
        gt_scr[...] = gtt_scr[...].T


def _route_select(x_ref, m_ref, g2_ref, wq_ref, sk_ref, hb_ref, q_scr, i1t_scr, i2t_scr, gtt_scr, expand_token):
    h = _modnorm(x_ref[...], g2_ref[...], m_ref[0, 3:4, :], m_ref[0, 4:5, :])
    hb = h.astype(BF16)
    hb_ref[...] = hb
    q = _dot(hb, wq_ref[...])
    for hp in range(2 * PEER_HEADS):
        q_scr[hp] = q[:, hp * PEER_DK:(hp + 1) * PEER_DK]
    sub = lax.broadcasted_iota(jnp.int32, (SUB, TR), 0).astype(F32)
    sk_bf = [sk_ref[p].astype(BF16) for p in range(2)]

    def key_stage(hd):
        top = ()
        for p in range(2):
            s = _dot_nt(sk_bf[p], q_scr[2 * hd + p].astype(BF16))
            top += _top16_keys(s, sub)
        return top

    def pair_stage(hd, top):
        sv1, si1, sv2, si2 = top
        cands = _pair_candidates(sv1, sv2, sub)
        fv, ff = _top16_of_blocks([c[0] for c in cands], [c[1] for c in cands], lambda flat: flat)
        ff = ff.astype(jnp.int32)
        fa, fb = ff >> 4, ff & (PEER_TOPK - 1)
        i1 = jnp.zeros((PEER_TOPK, TR), F32)
        i2 = jnp.zeros((PEER_TOPK, TR), F32)
        for a in range(PEER_TOPK):
            i1 = jnp.where(fa == a, si1[a:a + 1, :], i1)
            i2 = jnp.where(fb == a, si2[a:a + 1, :], i2)
        e = jnp.exp(fv - fv[0:1, :])
        g = e / jnp.sum(e, axis=0, keepdims=True)
        start = hd * PEER_TOPK
        rows = pl.ds(start if isinstance(start, int) else pl.multiple_of(start, PEER_TOPK), PEER_TOPK)
        i1t_scr[rows, :] = i1
        i2t_scr[rows, :] = i2
        gtt_scr[rows, :] = g

    n_first, n_mid = TR // 8, (TR - TR // 8) // 8
    top = key_stage(0)
    for k in range(n_first):
        expand_token(k)

    def per_head(hd, top_prev):
        top_cur = key_stage(hd)
        pair_stage(hd - 1, top_prev)
        for k in range(n_mid):
            expand_token(n_first + (hd - 1) * n_mid + k)
        return top_cur

    top = lax.fori_loop(1, PEER_HEADS, per_head, top)
    pair_stage(PEER_HEADS - 1, top)
    for n in range(n_first + (PEER_HEADS - 1) * n_mid, TR):
        expand_token(n)


def _peer_route(x, mods, g2, wq, subkeys, gate_scale):
    n_tiles = N_TOK // TR
    const = lambda shape: pl.BlockSpec(shape, lambda t: (0,) * len(shape))
    cur = lambda t: jnp.minimum(t, n_tiles - 1)
    return pl.pallas_call(
        _route_kernel,
        grid=(n_tiles + 1,),
        in_specs=[
            pl.BlockSpec((TR, D), lambda t: (cur(t), 0)),
            pl.BlockSpec((1, 6, D), lambda t: (_mod_row(cur(t), TR), 0, 0)),
            const((1, D)),
            const((D, PEER_HEADS * 2 * PEER_DK)),
            const((2, PEER_NKEYS, PEER_DK)),
            const((PEER_NKEYS, PEER_NKEYS)),
        ],
        out_specs=[
            pl.BlockSpec((TR, D), lambda t: (cur(t), 0)),
            pl.BlockSpec((PEER_NKEYS, TR, PEER_NKEYS), lambda t: (0, jnp.maximum(t - 1, 0), 0)),
        ],
        out_shape=[
            jax.ShapeDtypeStruct((N_TOK, D), BF16),
            jax.ShapeDtypeStruct((PEER_NKEYS, N_TOK, PEER_NKEYS), BF16),
        ],
        scratch_shapes=[
            pltpu.VMEM((2 * PEER_HEADS, TR, PEER_DK), F32),
            pltpu.VMEM((PEER_HEADS * PEER_TOPK, TR), F32),
            pltpu.VMEM((PEER_HEADS * PEER_TOPK, TR), F32),
            pltpu.VMEM((PEER_HEADS * PEER_TOPK, TR), F32),
            pltpu.VMEM((TR, PEER_HEADS * PEER_TOPK), F32),
            pltpu.VMEM((TR, PEER_HEADS * PEER_TOPK), F32),
            pltpu.VMEM((TR, PEER_HEADS * PEER_TOPK), F32),
            pltpu.VMEM((PEER_NKEYS * G_PITCH, PEER_NKEYS), F32),
        ],
        compiler_params=_cparams(("arbitrary",)),
        name="peer_route",
    )(x, mods, g2, wq, subkeys, gate_scale)


F8 = jnp.float8_e4m3fn
F8_TOP = 256.0
TINY = 1e-30


def _row_scale(a):
    return jnp.maximum(jnp.max(jnp.abs(a), axis=1, keepdims=True), TINY) * (1.0 / F8_TOP)


def _quant_rows_kernel(w_ref, q_ref, s_ref):
    w = w_ref[0]
    scale = _row_scale(w)
    q_ref[0] = (w * (1.0 / scale)).astype(F8)
    s_ref[0, 0] = jnp.transpose(jnp.broadcast_to(scale, (EC, 128)))[0:1, :]


def _quantize_rows(w_all):
    n_chunks = PEER_EXPERTS // EC
    return pl.pallas_call(
        _quant_rows_kernel,
        grid=(DEPTH, n_chunks),
        in_specs=[pl.BlockSpec((1, EC, D), lambda l, j: (l, j, 0))],
        out_specs=[pl.BlockSpec((1, EC, D), lambda l, j: (l, j, 0)),
                   pl.BlockSpec((1, 1, 1, EC), lambda l, j: (l, j, 0, 0))],
        out_shape=[jax.ShapeDtypeStruct((DEPTH, PEER_EXPERTS, D), F8),
                   jax.ShapeDtypeStruct((DEPTH, n_chunks, 1, EC), F32)],
        compiler_params=_cparams(("arbitrary", "arbitrary")),
        name="quantize_rows",
    )(w_all)


def _expert_kernel(x_ref, m_ref, hb_ref, gate_ref, u8_ref, su_ref, v8_ref, *rest):
    o_ref, h8_scr, t_scr = rest[-3:]
    c = pl.program_id(1)

    @pl.when(c == 0)
    def _():
        o_ref[...] = jnp.zeros_like(o_ref)
        hf = hb_ref[...].astype(F32)
        t = jnp.maximum(jnp.max(jnp.abs(hf), keepdims=True), TINY) * (1.0 / F8_TOP)
        h8_scr[...] = (hf * (1.0 / t)).astype(F8)
        t_scr[...] = t

    h8 = h8_scr[...]
    scale = su_ref[0, 0] * (t_scr[...] * (2.0 ** -0.5))
    slabs = []
    for s in range(EC // EC_SUB):
        cols = slice(s * EC_SUB, (s + 1) * EC_SUB)
        z = _dot_nt(h8, u8_ref[0, cols, :]) * scale[:, cols]
        act = z * (1.0 + lax.erf(z))
        for a in range(EC_SUB // PEER_NKEYS):
            gate = gate_ref[s * (EC_SUB // PEER_NKEYS) + a].astype(F32)
            slabs.append(act[:, a * PEER_NKEYS:(a + 1) * PEER_NKEYS] * gate)
    w = jnp.concatenate(slabs, axis=1)
    r = _row_scale(w)
    o_ref[...] += _dot((w * (1.0 / r)).astype(F8), v8_ref[0]) * r

    @pl.when(c == pl.num_programs(1) - 1)
    def _():
        y = x_ref[...] + m_ref[0, 5:6, :] * o_ref[...]
        if len(rest) == 4:
            y = y * lax.rsqrt(jnp.mean(y * y, axis=-1, keepdims=True) + EPS) * rest[0][...]
        o_ref[...] = y


def _peer_experts(x, mods, hb, gate, u8_all, su_all, v8_all, layer, final_g=None):
    n_tiles = N_TOK // TE
    n_chunks = PEER_EXPERTS // EC
    once = pl.Buffered(1)
    in_specs = [
        pl.BlockSpec((TE, D), lambda t, c: (t, 0), pipeline_mode=once),
        pl.BlockSpec((1, 6, D), lambda t, c: (_mod_row(t, TE), 0, 0)),
        pl.BlockSpec((TE, D), lambda t, c: (t, 0), pipeline_mode=once),
        pl.BlockSpec((EC // PEER_NKEYS, TE, PEER_NKEYS), lambda t, c: (c, t, 0)),
        pl.BlockSpec((1, EC, D), lambda t, c: (layer, c, 0)),
        pl.BlockSpec((1, 1, 1, EC), lambda t, c: (layer, c, 0, 0)),
        pl.BlockSpec((1, EC, D), lambda t, c: (layer, c, 0)),
    ]
    operands = [x, mods, hb, gate, u8_all, su_all, v8_all]
    if final_g is not None:
        in_specs.append(pl.BlockSpec((1, D), lambda t, c: (0, 0)))
        operands.append(final_g)
    return pl.pallas_call(
        _expert_kernel,
        grid=(n_tiles, n_chunks),
        in_specs=in_specs,
        out_specs=pl.BlockSpec((TE, D), lambda t, c: (t, 0)),
        out_shape=jax.ShapeDtypeStruct((N_TOK, D), F32),
        scratch_shapes=[pltpu.VMEM((TE, D), F8), pltpu.VMEM((1, 1), F32)],
        compiler_params=_cparams(("arbitrary", "arbitrary")),
        name="peer_experts",
    )(*operands)


def _rope_tables():
    t = jnp.arange(DEC_SEQ)
    row, col = t // GRID_W, t % GRID_W
    nf = HEAD_DIM // 4
    inv = ROPE_THETA ** (-jnp.arange(nf, dtype=F32) / nf)
    ang_r = row.astype(F32)[:, None] * inv[None, :]
    ang_c = col.astype(F32)[:, None] * inv[None, :]
    cos = jnp.concatenate([jnp.cos(ang_r)] * 2 + [jnp.cos(ang_c)] * 2, axis=1)
    sin = jnp.concatenate([-jnp.sin(ang_r), jnp.sin(ang_r), -jnp.sin(ang_c), jnp.sin(ang_c)], axis=1)
    cos = jnp.concatenate([jnp.ones((TM, HEAD_DIM), F32), cos], axis=0)
    sin = jnp.concatenate([jnp.zeros((TM, HEAD_DIM), F32), sin], axis=0)
    return jnp.tile(cos, (1, N_HEADS)), jnp.tile(sin, (1, N_HEADS))


def _head_block_diag():
    i = np.arange(128)
    return jnp.asarray((i[:, None] // HEAD_DIM == i[None, :] // HEAD_DIM).astype(np.float32), dtype=BF16)


def kernel(x_prompt, x_sample, cache_k, cache_v, c, c_ctx, mod_w, mod_b, norm1_g, norm2_g, a_in_w, a_out_w,
           q_norm_g, k_norm_g, pool_w, pool_scale, c_in_w, c_conv_w, c_out_w, peer_wq, peer_subkeys, peer_u,
           peer_v, final_g):
    x = jnp.concatenate([x_prompt.reshape(NP_TOK, D), x_sample.reshape(NS_TOK, D)], axis=0)
    ct = jnp.zeros((D, 8), F32).at[:, 0].set(c_ctx).at[:, 1:3].set(c.T)
    mods = _modulation(ct, mod_w, mod_b).reshape(DEPTH, 8, 6, D)
    rope_c, rope_s = _rope_tables()
    bd = _head_block_diag()
    wq = peer_wq.astype(BF16)
    u8, su = _quantize_rows(peer_u)
    v8, sv = _quantize_rows(peer_v)
    new_k, new_v = [], []
    for l in range(DEPTH):
        g1 = norm1_g[l].reshape(1, D)
        if l % 2 == 0:
            i = l // 2
            q, kk, vv, knew, vnew, p = _attn_in(
                x, mods[l], g1, a_in_w[i].astype(BF16), jnp.tile(q_norm_g[i], N_HEADS).reshape(1, ATTN_W),
                jnp.tile(k_norm_g[i], N_KV).reshape(1, KV_W), bd, rope_c, rope_s)
            new_k.append(knew[:NP_TOK].reshape(N_PROMPT_SEQ, SEQ, N_KV, HEAD_DIM))
            new_v.append(vnew[:NP_TOK].reshape(N_PROMPT_SEQ, SEQ, N_KV, HEAD_DIM))
            ck = cache_k[:, i]
            cv = cache_v[:, i]
            ctx_kk = jnp.concatenate([ck, ck[:, :, ::-1]], axis=2).reshape(N_SAMPLE_SEQ, PAST, 2 * KV_W).astype(BF16)
            ctx_vv = jnp.concatenate([cv, cv[:, :, ::-1]], axis=2).reshape(N_SAMPLE_SEQ, PAST, 2 * KV_W).astype(BF16)
            kk_p = kk[:NP_TOK].reshape(N_PROMPT_SEQ, SEQ, 2 * KV_W)
            vv_p = vv[:NP_TOK].reshape(N_PROMPT_SEQ, SEQ, 2 * KV_W)
            kk_s = jnp.concatenate([kk[NP_TOK:].reshape(N_SAMPLE_SEQ, DEC_SEQ, 2 * KV_W), ctx_kk], axis=1)
            vv_s = jnp.concatenate([vv[NP_TOK:].reshape(N_SAMPLE_SEQ, DEC_SEQ, 2 * KV_W), ctx_vv], axis=1)
            common = (pool_w[i].astype(BF16), pool_scale[i].reshape(1, POOL_W), a_out_w[i].astype(BF16))
            x = _attn_out(x, mods[l], q, kk_p, vv_p, p, *common,
                          first_tile=0, n_tiles=N_PROMPT_SEQ, tiles_per_seq=SEQ // TM)
            x = _attn_out(x, mods[l], q, kk_s, vv_s, p, *common,
                          first_tile=N_PROMPT_SEQ, n_tiles=NS_TOK // TM, tiles_per_seq=DEC_SEQ // TM)
        else:
            j = l // 2
            cw = jnp.zeros((8, D), F32).at[:3].set(c_conv_w[j])
            x = _conv_layer(x, mods[l], g1, c_in_w[j].astype(BF16), cw, c_out_w[j].astype(BF16))
        gate_scale = (2.0 ** -0.5) * sv[l].reshape(PEER_NKEYS, PEER_NKEYS)
        hb, gate = _peer_route(x, mods[l], norm2_g[l].reshape(1, D), wq[l], peer_subkeys[l], gate_scale)
        x = _peer_experts(x, mods[l], hb, gate, u8, su, v8, l,
                          final_g=final_g.reshape(1, D) if l == DEPTH - 1 else None)
    y = x
    y_prompt = y[:NP_TOK].reshape(N_PROMPT_SEQ, SEQ, D)
    y_sample = y[NP_TOK:].reshape(N_SAMPLE_SEQ, DEC_SEQ, D)
    return (y_prompt, y_sample, jnp.stack(new_k, axis=1), jnp.stack(new_v, axis=1))
```

```python
import functools

import numpy as np
import jax
import jax.numpy as jnp
from jax import lax
from jax.experimental import pallas as pl
from jax.experimental.pallas import tpu as pltpu

F32 = jnp.float32
BF16 = jnp.bfloat16

D = 1024
N_PROMPT_SEQ, SEQ = 32, 256
N_SAMPLE_SEQ, DEC_SEQ = 2, 2048
PAST = 256
NP_TOK = N_PROMPT_SEQ * SEQ
NS_TOK = N_SAMPLE_SEQ * DEC_SEQ
N_TOK = NP_TOK + NS_TOK
DEPTH = 4
GRID_W = 64
N_HEADS, N_KV, HEAD_DIM = 8, 2, 64
ATTN_W, KV_W, POOL_W = 512, 128, 512
POOL_WINDOWS = (2, 4, 8, 16)
A_IN_W = ATTN_W + 2 * KV_W + POOL_W
PEER_HEADS, PEER_NKEYS, PEER_TOPK, PEER_DK = 8, 128, 16, 128
PEER_EXPERTS = PEER_NKEYS * PEER_NKEYS
ROPE_THETA = 10000.0
EPS = 1e-6

TM = 256
HALO = 8
TR = 256
G_PITCH = TR + 8
TE = 1024
EC = 2048
EC_SUB = 512
V7X_VMEM_BYTES = 64 * 1024 * 1024
VMEM_LIMIT = V7X_VMEM_BYTES - 8 * 1024 * 1024


def _cparams(sem):
    return pltpu.CompilerParams(dimension_semantics=sem, vmem_limit_bytes=VMEM_LIMIT)


def _mod_row(tile, tile_tokens):
    start = tile * tile_tokens
    return (start >= NP_TOK).astype(jnp.int32) + (start >= NP_TOK + DEC_SEQ).astype(jnp.int32)


def _modnorm(x, g, shift, scale):
    ms = jnp.mean(x * x, axis=-1, keepdims=True)
    y = x * lax.rsqrt(ms + EPS) * g
    return y * (1.0 + scale) + shift


def _split_bf16(a):
    hi = a.astype(BF16)
    lo = (a - hi.astype(F32)).astype(BF16)
    return hi, lo


def _dot(a, b):
    return jnp.dot(a, b, preferred_element_type=F32)


def _dot_nt(a, b):
    return lax.dot_general(a, b, (((1,), (1,)), ((), ())), preferred_element_type=F32)


def _mod_kernel(ct_ref, w_ref, b_ref, o_ref):
    ct = ct_ref[...]
    s = ct * jax.nn.sigmoid(ct)
    w = w_ref[0]
    rows = [jnp.sum(s[:, r:r + 1] * w, axis=0, keepdims=True) + b_ref[0] for r in range(3)]
    rows.append(jnp.zeros((5, w.shape[1]), F32))
    o_ref[0] = jnp.concatenate(rows, axis=0)


def _modulation(ct, mod_w, mod_b):
    tn = 1536
    return pl.pallas_call(
        _mod_kernel,
        grid=(DEPTH, 6 * D // tn),
        in_specs=[
            pl.BlockSpec((D, 8), lambda l, j: (0, 0)),
            pl.BlockSpec((1, D, tn), lambda l, j: (l, 0, j)),
            pl.BlockSpec((1, 1, tn), lambda l, j: (l, 0, j)),
        ],
        out_specs=pl.BlockSpec((1, 8, tn), lambda l, j: (l, 0, j)),
        out_shape=jax.ShapeDtypeStruct((DEPTH, 8, 6 * D), F32),
        compiler_params=_cparams(("arbitrary", "arbitrary")),
        name="adaln_modulation",
    )(ct, mod_w, mod_b.reshape(DEPTH, 1, 6 * D))


def _head_norm(x, g, bd):
    outs = []
    for c in range(x.shape[1] // 128):
        xc = x[:, c * 128:(c + 1) * 128]
        hi, lo = _split_bf16(xc * xc)
        ms = (_dot(hi, bd) + _dot(lo, bd)) * (1.0 / HEAD_DIM)
        outs.append(xc * lax.rsqrt(ms + EPS))
    return jnp.concatenate(outs, axis=1) * g


def _rope(x, cos, sin, low16):
    outs = []
    for c in range(x.shape[1] // 128):
        sl = slice(c * 128, (c + 1) * 128)
        xc = x[:, sl]
        partner = jnp.where(low16, pltpu.roll(xc, 112, 1), pltpu.roll(xc, 16, 1))
        outs.append(xc * cos[:, sl] + partner * sin[:, sl])
    return jnp.concatenate(outs, axis=1)


def _a1_kernel(x_ref, m_ref, g1_ref, w_ref, qg_ref, kg_ref, bd_ref, rc_ref, rs_ref,
               q_ref, kk_ref, vv_ref, knew_ref, vnew_ref, p_ref):
    h = _modnorm(x_ref[...], g1_ref[...], m_ref[0, 0:1, :], m_ref[0, 1:2, :])
    z = _dot(h.astype(BF16), w_ref[...])
    bd = bd_ref[...]
    q = _head_norm(z[:, :ATTN_W], qg_ref[...], bd)
    k = _head_norm(z[:, ATTN_W:ATTN_W + KV_W], kg_ref[...], bd)
    v = z[:, ATTN_W + KV_W:ATTN_W + 2 * KV_W]
    knew_ref[...] = k
    vnew_ref[...] = v
    p_ref[...] = z[:, ATTN_W + 2 * KV_W:]
    cos, sin = rc_ref[...], rs_ref[...]
    lane = lax.broadcasted_iota(jnp.int32, (TM, 128), 1)
    low16 = (lane & 31) < 16
    q = _rope(q, cos, sin, low16) * (HEAD_DIM ** -0.5)
    k = _rope(k, cos, sin, low16)
    q_ref[...] = q.astype(BF16)
    kk_ref[...] = jnp.concatenate([k, pltpu.roll(k, 64, 1)], axis=1).astype(BF16)
    vv_ref[...] = jnp.concatenate([v, pltpu.roll(v, 64, 1)], axis=1).astype(BF16)


def _rope_block(t):
    return jnp.where(t < N_PROMPT_SEQ, 0, 1 + (t - N_PROMPT_SEQ) % (DEC_SEQ // TM))


def _attn_in(x, mods, g1, w_in, qg, kg, bd, rope_c, rope_s):
    n_tiles = N_TOK // TM
    tile = lambda w: pl.BlockSpec((TM, w), lambda t: (t, 0))
    const = lambda shape: pl.BlockSpec(shape, lambda t: (0,) * len(shape))
    return pl.pallas_call(
        _a1_kernel,
        grid=(n_tiles,),
        in_specs=[
            tile(D),
            pl.BlockSpec((1, 6, D), lambda t: (_mod_row(t, TM), 0, 0)),
            const((1, D)),
            const((D, A_IN_W)),
            const((1, ATTN_W)),
            const((1, KV_W)),
            const((128, 128)),
            pl.BlockSpec((TM, ATTN_W), lambda t: (_rope_block(t), 0)),
            pl.BlockSpec((TM, ATTN_W), lambda t: (_rope_block(t), 0)),
        ],
        out_specs=[tile(ATTN_W), tile(2 * KV_W), tile(2 * KV_W), tile(KV_W), tile(KV_W), tile(POOL_W)],
        out_shape=[
            jax.ShapeDtypeStruct((N_TOK, ATTN_W), BF16),
            jax.ShapeDtypeStruct((N_TOK, 2 * KV_W), BF16),
            jax.ShapeDtypeStruct((N_TOK, 2 * KV_W), BF16),
            jax.ShapeDtypeStruct((N_TOK, KV_W), F32),
            jax.ShapeDtypeStruct((N_TOK, KV_W), F32),
            jax.ShapeDtypeStruct((N_TOK, POOL_W), F32),
        ],
        compiler_params=_cparams(("arbitrary",)),
        name="attn_in_proj",
    )(x, mods, g1, w_in, qg, kg, bd, rope_c, rope_s)


def _a2_kernel(x_ref, m_ref, q_ref, kk_ref, vv_ref, p_ref, pp_ref, pn_ref, pw_ref, ps_ref,
               wo_ref, o_ref, *, tiles_per_seq):
    t = pl.program_id(0)
    q = q_ref[...]
    kk = kk_ref[0]
    vv = vv_ref[0]
    lane = lax.broadcasted_iota(jnp.int32, (TM, 128), 1)
    half_mask = (lane < HEAD_DIM, lane >= HEAD_DIM)
    zero_q = jnp.zeros((TM, 128), BF16)
    chunks = []
    for c in range(ATTN_W // 128):
        qc = q[:, c * 128:(c + 1) * 128]
        acc = None
        for half in range(2):
            kv = (2 * c + half) // (N_HEADS // N_KV)
            slab = slice(0, 128) if kv == half else slice(128, 256)
            s = _dot_nt(jnp.where(half_mask[half], qc, zero_q), kk[:, slab])
            e = jnp.exp(s - jnp.max(s, axis=-1, keepdims=True))
            l = jnp.sum(e, axis=-1, keepdims=True)
            o = _dot(e.astype(BF16), vv[:, slab]) / l
            o = jnp.where(half_mask[half], o, 0.0)
            acc = o if acc is None else acc + o
        chunks.append(acc)

    pos = t % tiles_per_seq
    has_prev = jnp.where(pos == 0, 0.0, 1.0)
    has_next = jnp.where(pos == tiles_per_seq - 1, 0.0, 1.0)
    p = p_ref[...]
    pext = jnp.concatenate([pp_ref[...] * has_prev, p, pn_ref[...] * has_next], axis=0)
    rows_ext = TM + 2 * HALO
    trow = lax.broadcasted_iota(jnp.int32, (TM, 128), 0).astype(F32)
    for g, w in enumerate(POOL_WINDOWS):
        sl = slice(g * 128, (g + 1) * 128)
        a = pext[:, sl]
        s = a + pltpu.roll(a, 1, 0)
        half = 1
        while 2 * half < w:
            s = pltpu.roll(s, half, 0) + pltpu.roll(s, rows_ext - half, 0)
            half *= 2
        cnt = (w - jnp.maximum(w // 2 - trow, 0.0) * (1.0 - has_prev)
               - jnp.maximum(trow + (w // 2 - TM), 0.0) * (1.0 - has_next))
        dev = s[HALO:HALO + TM] / cnt - p[:, sl]
        chunks.append(_dot(dev.astype(BF16), pw_ref[g]) * ps_ref[:, sl])
    mix = jnp.concatenate(chunks, axis=1).astype(BF16)
    o_ref[...] = x_ref[...] + m_ref[0, 2:3, :] * _dot(mix, wo_ref[...])


def _attn_out(x, mods, q, kk_seq, vv_seq, p, pool_w, pool_s, w_out, *, first_tile, n_tiles, tiles_per_seq):
    s_len = kk_seq.shape[1]
    off = first_tile
    n_halo_blocks = N_TOK // HALO
    per_tile = TM // HALO
    tile = lambda w: pl.BlockSpec((TM, w), lambda t: (t + off, 0))
    const = lambda shape: pl.BlockSpec(shape, lambda t: (0,) * len(shape))
    seq = pl.BlockSpec((1, s_len, 2 * KV_W), lambda t: (t // tiles_per_seq, 0, 0))
    return pl.pallas_call(
        functools.partial(_a2_kernel, tiles_per_seq=tiles_per_seq),
        grid=(n_tiles,),
        in_specs=[
            tile(D),
            pl.BlockSpec((1, 6, D), lambda t: (_mod_row(t + off, TM), 0, 0)),
            tile(ATTN_W),
            seq,
            seq,
            tile(POOL_W),
            pl.BlockSpec((HALO, POOL_W), lambda t: (jnp.maximum((t + off) * per_tile - 1, 0), 0)),
            pl.BlockSpec((HALO, POOL_W), lambda t: (jnp.minimum((t + off + 1) * per_tile, n_halo_blocks - 1), 0)),
            const((len(POOL_WINDOWS), 128, 128)),
            const((1, POOL_W)),
            const((D, D)),
        ],
        out_specs=tile(D),
        out_shape=jax.ShapeDtypeStruct((N_TOK, D), F32),
        input_output_aliases={0: 0},
        compiler_params=_cparams(("arbitrary",)),
        name="attn_pool_out_%d" % s_len,
    )(x, mods, q, kk_seq, vv_seq, p, p, p, pool_w, pool_s, w_out)


def _conv_kernel(x_ref, xp_ref, xn_ref, m_ref, g1_ref, wi_ref, cw_ref, wo_ref, o_ref):
    t = pl.program_id(0)
    is_prompt = t < N_PROMPT_SEQ
    pos = t & (DEC_SEQ // TM - 1)
    has_prev = jnp.where(is_prompt | (pos == 0), 0.0, 1.0)
    has_next = jnp.where(is_prompt | (pos == DEC_SEQ // TM - 1), 0.0, 1.0)
    x = x_ref[...]
    xext = jnp.concatenate([xp_ref[...], x, xn_ref[...]], axis=0)
    h = _modnorm(xext, g1_ref[...], m_ref[0, 0:1, :], m_ref[0, 1:2, :])
    z = _dot(h.astype(BF16), wi_ref[...])
    gb = z[HALO:HALO + TM, :D]
    u = z[:, D:2 * D] * z[:, 2 * D:]
    row = lax.broadcasted_iota(jnp.int32, (TM + 2 * HALO, 1), 0)
    u = u * jnp.where(row < HALO, has_prev, jnp.where(row >= TM + HALO, has_next, 1.0))
    rows = TM + 2 * HALO
    cw = cw_ref[...]
    y = (pltpu.roll(u, 1, 0)[HALO:HALO + TM] * cw[0:1, :]
         + u[HALO:HALO + TM] * cw[1:2, :]
         + pltpu.roll(u, rows - 1, 0)[HALO:HALO + TM] * cw[2:3, :])
    out = _dot((gb * y).astype(BF16), wo_ref[...])
    o_ref[...] = x + m_ref[0, 2:3, :] * out


def _conv_layer(x, mods, g1, w_in, conv_w, w_out):
    n_tiles = N_TOK // TM
    n_halo_blocks = N_TOK // HALO
    per_tile = TM // HALO
    const = lambda shape: pl.BlockSpec(shape, lambda t: (0,) * len(shape))
    return pl.pallas_call(
        _conv_kernel,
        grid=(n_tiles,),
        in_specs=[
            pl.BlockSpec((TM, D), lambda t: (t, 0)),
            pl.BlockSpec((HALO, D), lambda t: (jnp.maximum(t * per_tile - 1, 0), 0)),
            pl.BlockSpec((HALO, D), lambda t: (jnp.minimum((t + 1) * per_tile, n_halo_blocks - 1), 0)),
            pl.BlockSpec((1, 6, D), lambda t: (_mod_row(t, TM), 0, 0)),
            const((1, D)),
            const((D, 3 * D)),
            const((8, D)),
            const((D, D)),
        ],
        out_specs=pl.BlockSpec((TM, D), lambda t: (t, 0)),
        out_shape=jax.ShapeDtypeStruct((N_TOK, D), F32),
        compiler_params=_cparams(("arbitrary",)),
        name="conv_mixer",
    )(x, x, x, mods, g1, w_in, conv_w, w_out)


NEG_INF = float("-inf")
SUB = 8
NO_ROW = float(1 << 20)


def _sort_network(n):
    def merge(lo, hi, r):
        step = r * 2
        if step < hi - lo:
            yield from merge(lo, hi, step)
            yield from merge(lo + r, hi, step)
            yield from [(i, i + r) for i in range(lo + r, hi - r, step)]
        else:
            yield (lo, lo + r)

    def sort(lo, hi):
        if hi - lo >= 1:
            mid = lo + (hi - lo) // 2
            yield from sort(lo, mid)
            yield from sort(mid + 1, hi)
            yield from merge(lo, hi, 1)

    full = 1 << (n - 1).bit_length()
    return tuple((a, b) for a, b in sort(0, full - 1) if b < n)


def _top16_of_blocks(vs, ts, row_of, n_out=PEER_TOPK):
    vs, ts = list(vs), list(ts)
    for a, b in _sort_network(len(vs)):
        va, vb, ta, tb = vs[a], vs[b], ts[a], ts[b]
        if isinstance(ta, float) and isinstance(tb, float):
            first = (va >= vb) if ta < tb else (va > vb)
        else:
            first = (va > vb) | ((va == vb) & (ta < tb))
        vs[a], vs[b] = jnp.maximum(va, vb), jnp.minimum(va, vb)
        ts[a], ts[b] = jnp.where(first, ta, tb), jnp.where(first, tb, ta)
    vals, rows_out = [], []
    for r in range(n_out):
        rows = row_of(ts[0])
        m = jnp.max(vs[0], axis=0, keepdims=True)
        ix = jnp.min(jnp.where(vs[0] == m, rows, NO_ROW), axis=0, keepdims=True)
        vals.append(m)
        rows_out.append(ix)
        taken = rows == ix
        for k in range(min(len(vs), n_out - r - 1)):
            if k + 1 < len(vs):
                vs[k] = jnp.where(taken, vs[k + 1], vs[k])
                ts[k] = jnp.where(taken, ts[k + 1], ts[k])
            else:
                vs[k] = jnp.where(taken, NEG_INF, vs[k])
    return jnp.concatenate(vals, axis=0), jnp.concatenate(rows_out, axis=0)


def _top16_keys(s, sub):
    n = PEER_NKEYS // SUB
    return _top16_of_blocks([s[SUB * v:SUB * (v + 1), :] for v in range(n)], [float(v) for v in range(n)],
                            lambda blk: blk * float(SUB) + sub)


def _pair_candidates(sv1, sv2, sub):
    a_lo, a_hi, b_lo, b_hi = sv1[0:SUB], sv1[SUB:], sv2[0:SUB], sv2[SUB:]
    row = lambda x, r: x[r:r + 1, :]
    keep = lambda cond, v: jnp.where(cond, v, NEG_INF)
    k = float(PEER_TOPK)
    tail = keep((sub >= 2) & (sub <= 4), row(sv1, 2) + b_lo)
    tail_flat = sub + 2 * k
    for r, (a, b) in ((5, (3, 2)), (6, (4, 2)), (7, (3, 3))):
        tail = jnp.where(sub == r, row(sv1, a) + row(sv2, b), tail)
        tail_flat = jnp.where(sub == r, a * k + b, tail_flat)
    return [
        (row(sv1, 0) + b_lo, sub),
        (row(sv1, 0) + b_hi, sub + SUB),
        (keep(sub >= 1, a_lo + row(sv2, 0)), sub * k),
        (a_hi + row(sv2, 0), (sub + SUB) * k),
        (keep(sub >= 1, row(sv1, 1) + b_lo), sub + k),
        (keep(sub >= 2, a_lo + row(sv2, 1)), sub * k + 1),
        (tail, tail_flat),
    ]


def _route_kernel(x_ref, m_ref, g2_ref, wq_ref, sk_ref, gs_ref, hb_ref, gate_ref,
                  q_scr, i1t_scr, i2t_scr, gtt_scr, i1_scr, i2_scr, gt_scr, gscr):
    @pl.when(pl.program_id(0) == 0)
    def _():
        i1_scr[...] = jnp.zeros_like(i1_scr)
        i2_scr[...] = jnp.zeros_like(i2_scr)
        gt_scr[...] = jnp.zeros_like(gt_scr)

    key = lax.broadcasted_iota(jnp.int32, (PEER_NKEYS, 128), 0).astype(F32)

    def expand_token(n):
        row = pl.ds(n, 1)
        left = jnp.where(key == i1_scr[row, :], gt_scr[row, :], 0.0).astype(BF16)
        right = jnp.where(key == i2_scr[row, :], 1.0, 0.0).astype(BF16)
        gscr[pl.ds(n, PEER_NKEYS, stride=G_PITCH), :] = _dot_nt(left, right)

    def copy_out(i, carry):
        rows = gscr[pl.ds(pl.multiple_of(i * G_PITCH, 8), TR), :]
        gate_ref[i] = (rows * gs_ref[pl.ds(i, 1), :]).astype(BF16)
        return carry

    last = pl.num_programs(0) - 1

    @pl.when(pl.program_id(0) == last)
    def _():
        hb_ref[...] = _modnorm(x_ref[...], g2_ref[...], m_ref[0, 3:4, :], m_ref[0, 4:5, :]).astype(BF16)

        def expand_group(j, carry):
            for k in range(16):
                expand_token(j * 16 + k)
            return carry

        lax.fori_loop(0, TR // 16, expand_group, 0)
        lax.fori_loop(0, PEER_NKEYS, copy_out, 0, unroll=4)

    @pl.when(pl.program_id(0) < last)
    def _():
        _route_select(x_ref, m_ref, g2_ref, wq_ref, sk_ref, hb_ref, q_scr, i1t_scr, i2t_scr, gtt_scr, expand_token)
        lax.fori_loop(0, PEER_NKEYS, copy_out, 0, unroll=4)
        i1_scr[...] = i1t_scr[...].T
        i2_scr[...] = i2t_scr[...].T
        gt_scr[...] = gtt_scr[...].T


def _route_select(x_ref, m_ref, g2_ref, wq_ref, sk_ref, hb_ref, q_scr, i1t_scr, i2t_scr, gtt_scr, expand_token):
    h = _modnorm(x_ref[...], g2_ref[...], m_ref[0, 3:4, :], m_ref[0, 4:5, :])
    hb = h.astype(BF16)
    hb_ref[...] = hb
    q = _dot(hb, wq_ref[...])
    for hp in range(2 * PEER_HEADS):
        q_scr[hp] = q[:, hp * PEER_DK:(hp + 1) * PEER_DK]
    sub = lax.broadcasted_iota(jnp.int32, (SUB, TR), 0).astype(F32)
    sk_bf = [sk_ref[p].astype(BF16) for p in range(2)]

    def key_stage(hd):
        top = ()
        for p in range(2):
            s = _dot_nt(sk_bf[p], q_scr[2 * hd + p].astype(BF16))
            top += _top16_keys(s, sub)
        return top

    def pair_stage(hd, top):
        sv1, si1, sv2, si2 = top
        cands = _pair_candidates(sv1, sv2, sub)
        fv, ff = _top16_of_blocks([c[0] for c in cands], [c[1] for c in cands], lambda flat: flat)
        ff = ff.astype(jnp.int32)
        fa, fb = ff >> 4, ff & (PEER_TOPK - 1)
        i1 = jnp.zeros((PEER_TOPK, TR), F32)
        i2 = jnp.zeros((PEER_TOPK, TR), F32)
        for a in range(PEER_TOPK):
            i1 = jnp.where(fa == a, si1[a:a + 1, :], i1)
            i2 = jnp.where(fb == a, si2[a:a + 1, :], i2)
        e = jnp.exp(fv - fv[0:1, :])
        g = e / jnp.sum(e, axis=0, keepdims=True)
        start = hd * PEER_TOPK
        rows = pl.ds(start if isinstance(start, int) else pl.multiple_of(start, PEER_TOPK), PEER_TOPK)
        i1t_scr[rows, :] = i1
        i2t_scr[rows, :] = i2
        gtt_scr[rows, :] = g

    n_first, n_mid = TR // 8, (TR - TR // 8) // 8
    top = key_stage(0)
    for k in range(n_first):
        expand_token(k)

    def per_head(hd, top_prev):
        top_cur = key_stage(hd)
        pair_stage(hd - 1, top_prev)
        for k in range(n_mid):
            expand_token(n_first + (hd - 1) * n_mid + k)
        return top_cur

    top = lax.fori_loop(1, PEER_HEADS, per_head, top)
    pair_stage(PEER_HEADS - 1, top)
    for n in range(n_first + (PEER_HEADS - 1) * n_mid, TR):
        expand_token(n)


def _peer_route(x, mods, g2, wq, subkeys, gate_scale):
    n_tiles = N_TOK // TR
    const = lambda shape: pl.BlockSpec(shape, lambda t: (0,) * len(shape))
    cur = lambda t: jnp.minimum(t, n_tiles - 1)
    return pl.pallas_call(
        _route_kernel,
        grid=(n_tiles + 1,),
        in_specs=[
            pl.BlockSpec((TR, D), lambda t: (cur(t), 0)),
            pl.BlockSpec((1, 6, D), lambda t: (_mod_row(cur(t), TR), 0, 0)),
            const((1, D)),
            const((D, PEER_HEADS * 2 * PEER_DK)),
            const((2, PEER_NKEYS, PEER_DK)),
            const((PEER_NKEYS, PEER_NKEYS)),
        ],
        out_specs=[
            pl.BlockSpec((TR, D), lambda t: (cur(t), 0)),
            pl.BlockSpec((PEER_NKEYS, TR, PEER_NKEYS), lambda t: (0, jnp.maximum(t - 1, 0), 0)),
        ],
        out_shape=[
            jax.ShapeDtypeStruct((N_TOK, D), BF16),
            jax.ShapeDtypeStruct((PEER_NKEYS, N_TOK, PEER_NKEYS), BF16),
        ],
        scratch_shapes=[
            pltpu.VMEM((2 * PEER_HEADS, TR, PEER_DK), F32),
            pltpu.VMEM((PEER_HEADS * PEER_TOPK, TR), F32),
            pltpu.VMEM((PEER_HEADS * PEER_TOPK, TR), F32),
            pltpu.VMEM((PEER_HEADS * PEER_TOPK, TR), F32),
            pltpu.VMEM((TR, PEER_HEADS * PEER_TOPK), F32),
            pltpu.VMEM((TR, PEER_HEADS * PEER_TOPK), F32),
            pltpu.VMEM((TR, PEER_HEADS * PEER_TOPK), F32),
            pltpu.VMEM((PEER_NKEYS * G_PITCH, PEER_NKEYS), F32),
        ],
        compiler_params=_cparams(("arbitrary",)),
        name="peer_route",
    )(x, mods, g2, wq, subkeys, gate_scale)


F8 = jnp.float8_e4m3fn
F8_TOP = 256.0
TINY = 1e-30


def _row_scale(a):
    return jnp.maximum(jnp.max(jnp.abs(a), axis=1, keepdims=True), TINY) * (1.0 / F8_TOP)


def _quant_rows_kernel(w_ref, q_ref, s_ref):
    w = w_ref[0]
    scale = _row_scale(w)
    q_ref[0] = (w * (1.0 / scale)).astype(F8)
    s_ref[0, 0] = jnp.transpose(jnp.broadcast_to(scale, (EC, 128)))[0:1, :]


def _quantize_rows(w_all):
    n_chunks = PEER_EXPERTS // EC
    return pl.pallas_call(
        _quant_rows_kernel,
        grid=(DEPTH, n_chunks),
        in_specs=[pl.BlockSpec((1, EC, D), lambda l, j: (l, j, 0))],
        out_specs=[pl.BlockSpec((1, EC, D), lambda l, j: (l, j, 0)),
                   pl.BlockSpec((1, 1, 1, EC), lambda l, j: (l, j, 0, 0))],
        out_shape=[jax.ShapeDtypeStruct((DEPTH, PEER_EXPERTS, D), F8),
                   jax.ShapeDtypeStruct((DEPTH, n_chunks, 1, EC), F32)],
        compiler_params=_cparams(("arbitrary", "arbitrary")),
        name="quantize_rows",
    )(w_all)


def _expert_kernel(x_ref, m_ref, hb_ref, gate_ref, u8_ref, su_ref, v8_ref, *rest):
    o_ref, h8_scr, t_scr, hn_scr = rest[-4:]
    c = pl.program_id(1)

    @pl.when(c == 0)
    def _():
        o_ref[...] = jnp.zeros_like(o_ref)
        hf = hb_ref[...].astype(F32)
        t = jnp.maximum(jnp.max(jnp.abs(hf), keepdims=True), TINY) * (1.0 / F8_TOP)
        t_scr[...] = t
        hq = (hf * (1.0 / t)).astype(F8)
        h8_scr[...] = hq
        hq = hq.astype(F32)
        hn_scr[...] = jnp.sqrt(jnp.sum(hq * hq, axis=1, keepdims=True))

    h8 = h8_scr[...]
    scale = su_ref[0, 0] * (t_scr[...] * (2.0 ** -0.5))
    n_slabs = EC // PEER_NKEYS
    gmax = gate_ref[0]
    for a in range(1, n_slabs):
        gmax = jnp.maximum(gmax, gate_ref[a])
    gmax = jnp.max(gmax.astype(F32), axis=1, keepdims=True)
    z_bound = hn_scr[...] * (jnp.max(scale, axis=1, keepdims=True) * (1.01 * (D ** 0.5) * F8_TOP))
    r = jnp.maximum(gmax * z_bound * (2.0 / F8_TOP), TINY)
    inv_r = 1.0 / r
    slabs = []
    for s in range(EC // EC_SUB):
        cols = slice(s * EC_SUB, (s + 1) * EC_SUB)
        z = _dot_nt(h8, u8_ref[0, cols, :]) * scale[:, cols]
        act = z * (1.0 + lax.erf(z))
        for a in range(EC_SUB // PEER_NKEYS):
            gate = gate_ref[s * (EC_SUB // PEER_NKEYS) + a].astype(F32) * inv_r
            slabs.append((act[:, a * PEER_NKEYS:(a + 1) * PEER_NKEYS] * gate).astype(F8))
    o_ref[...] += _dot(jnp.concatenate(slabs, axis=1), v8_ref[0]) * r

    @pl.when(c == pl.num_programs(1) - 1)
    def _():
        y = x_ref[...] + m_ref[0, 5:6, :] * o_ref[...]
        if len(rest) == 5:
            y = y * lax.rsqrt(jnp.mean(y * y, axis=-1, keepdims=True) + EPS) * rest[0][...]
        o_ref[...] = y


def _peer_experts(x, mods, hb, gate, u8_all, su_all, v8_all, layer, final_g=None):
    n_tiles = N_TOK // TE
    n_chunks = PEER_EXPERTS // EC
    once = pl.Buffered(1)
    in_specs = [
        pl.BlockSpec((TE, D), lambda t, c: (t, 0), pipeline_mode=once),
        pl.BlockSpec((1, 6, D), lambda t, c: (_mod_row(t, TE), 0, 0)),
        pl.BlockSpec((TE, D), lambda t, c: (t, 0), pipeline_mode=once),
        pl.BlockSpec((EC // PEER_NKEYS, TE, PEER_NKEYS), lambda t, c: (c, t, 0)),
        pl.BlockSpec((1, EC, D), lambda t, c: (layer, c, 0)),
        pl.BlockSpec((1, 1, 1, EC), lambda t, c: (layer, c, 0, 0)),
        pl.BlockSpec((1, EC, D), lambda t, c: (layer, c, 0)),
    ]
    operands = [x, mods, hb, gate, u8_all, su_all, v8_all]
    if final_g is not None:
        in_specs.append(pl.BlockSpec((1, D), lambda t, c: (0, 0)))
        operands.append(final_g)
    return pl.pallas_call(
        _expert_kernel,
        grid=(n_tiles, n_chunks),
        in_specs=in_specs,
        out_specs=pl.BlockSpec((TE, D), lambda t, c: (t, 0)),
        out_shape=jax.ShapeDtypeStruct((N_TOK, D), F32),
        scratch_shapes=[pltpu.VMEM((TE, D), F8), pltpu.VMEM((1, 1), F32), pltpu.VMEM((TE, 1), F32)],
        compiler_params=_cparams(("arbitrary", "arbitrary")),
        name="peer_experts",
    )(*operands)


def _rope_tables():
    t = jnp.arange(DEC_SEQ)
    row, col = t // GRID_W, t % GRID_W
    nf = HEAD_DIM // 4
    inv = ROPE_THETA ** (-jnp.arange(nf, dtype=F32) / nf)
    ang_r = row.astype(F32)[:, None] * inv[None, :]
    ang_c = col.astype(F32)[:, None] * inv[None, :]
    cos = jnp.concatenate([jnp.cos(ang_r)] * 2 + [jnp.cos(ang_c)] * 2, axis=1)
    sin = jnp.concatenate([-jnp.sin(ang_r), jnp.sin(ang_r), -jnp.sin(ang_c), jnp.sin(ang_c)], axis=1)
    cos = jnp.concatenate([jnp.ones((TM, HEAD_DIM), F32), cos], axis=0)
    sin = jnp.concatenate([jnp.zeros((TM, HEAD_DIM), F32), sin], axis=0)
    return jnp.tile(cos, (1, N_HEADS)), jnp.tile(sin, (1, N_HEADS))


def _head_block_diag():
    i = np.arange(128)
    return jnp.asarray((i[:, None] // HEAD_DIM == i[None, :] // HEAD_DIM).astype(np.float32), dtype=BF16)


def kernel(x_prompt, x_sample, cache_k, cache_v, c, c_ctx, mod_w, mod_b, norm1_g, norm2_g, a_in_w, a_out_w,
           q_norm_g, k_norm_g, pool_w, pool_scale, c_in_w, c_conv_w, c_out_w, peer_wq, peer_subkeys, peer_u,
           peer_v, final_g):
    x = jnp.concatenate([x_prompt.reshape(NP_TOK, D), x_sample.reshape(NS_TOK, D)], axis=0)
    ct = jnp.zeros((D, 8), F32).at[:, 0].set(c_ctx).at[:, 1:3].set(c.T)
    mods = _modulation(ct, mod_w, mod_b).reshape(DEPTH, 8, 6, D)
    rope_c, rope_s = _rope_tables()
    bd = _head_block_diag()
    wq = peer_wq.astype(BF16)
    u8, su = _quantize_rows(peer_u)
    v8, sv = _quantize_rows(peer_v)
    new_k, new_v = [], []
    for l in range(DEPTH):
        g1 = norm1_g[l].reshape(1, D)
        if l % 2 == 0:
            i = l // 2
            q, kk, vv, knew, vnew, p = _attn_in(
                x, mods[l], g1, a_in_w[i].astype(BF16), jnp.tile(q_norm_g[i], N_HEADS).reshape(1, ATTN_W),
                jnp.tile(k_norm_g[i], N_KV).reshape(1, KV_W), bd, rope_c, rope_s)
            new_k.append(knew[:NP_TOK].reshape(N_PROMPT_SEQ, SEQ, N_KV, HEAD_DIM))
            new_v.append(vnew[:NP_TOK].reshape(N_PROMPT_SEQ, SEQ, N_KV, HEAD_DIM))
            ck = cache_k[:, i]
            cv = cache_v[:, i]
            ctx_kk = jnp.concatenate([ck, ck[:, :, ::-1]], axis=2).reshape(N_SAMPLE_SEQ, PAST, 2 * KV_W).astype(BF16)
            ctx_vv = jnp.concatenate([cv, cv[:, :, ::-1]], axis=2).reshape(N_SAMPLE_SEQ, PAST, 2 * KV_W).astype(BF16)
            kk_p = kk[:NP_TOK].reshape(N_PROMPT_SEQ, SEQ, 2 * KV_W)
            vv_p = vv[:NP_TOK].reshape(N_PROMPT_SEQ, SEQ, 2 * KV_W)
            kk_s = jnp.concatenate([kk[NP_TOK:].reshape(N_SAMPLE_SEQ, DEC_SEQ, 2 * KV_W), ctx_kk], axis=1)
            vv_s = jnp.concatenate([vv[NP_TOK:].reshape(N_SAMPLE_SEQ, DEC_SEQ, 2 * KV_W), ctx_vv], axis=1)
            common = (pool_w[i].astype(BF16), pool_scale[i].reshape(1, POOL_W), a_out_w[i].astype(BF16))
            x = _attn_out(x, mods[l], q, kk_p, vv_p, p, *common,
                          first_tile=0, n_tiles=N_PROMPT_SEQ, tiles_per_seq=SEQ // TM)
            x = _attn_out(x, mods[l], q, kk_s, vv_s, p, *common,
                          first_tile=N_PROMPT_SEQ, n_tiles=NS_TOK // TM, tiles_per_seq=DEC_SEQ // TM)
        else:
            j = l // 2
            cw = jnp.zeros((8, D), F32).at[:3].set(c_conv_w[j])
            x = _conv_layer(x, mods[l], g1, c_in_w[j].astype(BF16), cw, c_out_w[j].astype(BF16))
        gate_scale = (2.0 ** -0.5) * sv[l].reshape(PEER_NKEYS, PEER_NKEYS)
        hb, gate = _peer_route(x, mods[l], norm2_g[l].reshape(1, D), wq[l], peer_subkeys[l], gate_scale)
        x = _peer_experts(x, mods[l], hb, gate, u8, su, v8, l,
                          final_g=final_g.reshape(1, D) if l == DEPTH - 1 else None)
    y = x
    y_prompt = y[:NP_TOK].reshape(N_PROMPT_SEQ, SEQ, D)
    y_sample = y[NP_TOK:].reshape(N_SAMPLE_SEQ, DEC_SEQ, D)
    return (y_prompt, y_sample, jnp.stack(new_k, axis=1), jnp.stack(new_v, axis=1))
```

```python
import functools

import numpy as np
import jax
import jax.numpy as jnp
from jax import lax
from jax.experimental import pallas as pl
from jax.experimental.pallas import tpu as pltpu

F32 = jnp.float32
BF16 = jnp.bfloat16

D = 1024
N_PROMPT_SEQ, SEQ = 32, 256
N_SAMPLE_SEQ, DEC_SEQ = 2, 2048
PAST = 256
NP_TOK = N_PROMPT_SEQ * SEQ
NS_TOK = N_SAMPLE_SEQ * DEC_SEQ
N_TOK = NP_TOK + NS_TOK
DEPTH = 4
GRID_W = 64
N_HEADS, N_KV, HEAD_DIM = 8, 2, 64
ATTN_W, KV_W, POOL_W = 512, 128, 512
POOL_WINDOWS = (2, 4, 8, 16)
A_IN_W = ATTN_W + 2 * KV_W + POOL_W
PEER_HEADS, PEER_NKEYS, PEER_TOPK, PEER_DK = 8, 128, 16, 128
PEER_EXPERTS = PEER_NKEYS * PEER_NKEYS
ROPE_THETA = 10000.0
EPS = 1e-6

TM = 256
HALO = 8
TR = 256
G_PITCH = TR + 8
TE = 1024
EC = 2048
EC_SUB = 512
V7X_VMEM_BYTES = 64 * 1024 * 1024
VMEM_LIMIT = V7X_VMEM_BYTES - 8 * 1024 * 1024


def _cparams(sem):
    return pltpu.CompilerParams(dimension_semantics=sem, vmem_limit_bytes=VMEM_LIMIT)


def _mod_row(tile, tile_tokens):
    start = tile * tile_tokens
    return (start >= NP_TOK).astype(jnp.int32) + (start >= NP_TOK + DEC_SEQ).astype(jnp.int32)


def _modnorm(x, g, shift, scale):
    ms = jnp.mean(x * x, axis=-1, keepdims=True)
    y = x * lax.rsqrt(ms + EPS) * g
    return y * (1.0 + scale) + shift


def _split_bf16(a):
    hi = a.astype(BF16)
    lo = (a - hi.astype(F32)).astype(BF16)
    return hi, lo


def _dot(a, b):
    return jnp.dot(a, b, preferred_element_type=F32)


def _dot_nt(a, b):
    return lax.dot_general(a, b, (((1,), (1,)), ((), ())), preferred_element_type=F32)


def _mod_kernel(ct_ref, w_ref, b_ref, o_ref):
    ct = ct_ref[...]
    s = ct * jax.nn.sigmoid(ct)
    w = w_ref[0]
    rows = [jnp.sum(s[:, r:r + 1] * w, axis=0, keepdims=True) + b_ref[0] for r in range(3)]
    rows.append(jnp.zeros((5, w.shape[1]), F32))
    o_ref[0] = jnp.concatenate(rows, axis=0)


def _modulation(ct, mod_w, mod_b):
    tn = 1536
    return pl.pallas_call(
        _mod_kernel,
        grid=(DEPTH, 6 * D // tn),
        in_specs=[
            pl.BlockSpec((D, 8), lambda l, j: (0, 0)),
            pl.BlockSpec((1, D, tn), lambda l, j: (l, 0, j)),
            pl.BlockSpec((1, 1, tn), lambda l, j: (l, 0, j)),
        ],
        out_specs=pl.BlockSpec((1, 8, tn), lambda l, j: (l, 0, j)),
        out_shape=jax.ShapeDtypeStruct((DEPTH, 8, 6 * D), F32),
        compiler_params=_cparams(("arbitrary", "arbitrary")),
        name="adaln_modulation",
    )(ct, mod_w, mod_b.reshape(DEPTH, 1, 6 * D))


def _head_norm(x, g, bd):
    outs = []
    for c in range(x.shape[1] // 128):
        xc = x[:, c * 128:(c + 1) * 128]
        hi, lo = _split_bf16(xc * xc)
        ms = (_dot(hi, bd) + _dot(lo, bd)) * (1.0 / HEAD_DIM)
        outs.append(xc * lax.rsqrt(ms + EPS))
    return jnp.concatenate(outs, axis=1) * g


def _rope(x, cos, sin, low16):
    outs = []
    for c in range(x.shape[1] // 128):
        sl = slice(c * 128, (c + 1) * 128)
        xc = x[:, sl]
        partner = jnp.where(low16, pltpu.roll(xc, 112, 1), pltpu.roll(xc, 16, 1))
        outs.append(xc * cos[:, sl] + partner * sin[:, sl])
    return jnp.concatenate(outs, axis=1)


def _a1_kernel(x_ref, m_ref, g1_ref, w_ref, qg_ref, kg_ref, bd_ref, rc_ref, rs_ref,
               q_ref, kk_ref, vv_ref, knew_ref, vnew_ref, p_ref):
    h = _modnorm(x_ref[...], g1_ref[...], m_ref[0, 0:1, :], m_ref[0, 1:2, :])
    z = _dot(h.astype(BF16), w_ref[...])
    bd = bd_ref[...]
    q = _head_norm(z[:, :ATTN_W], qg_ref[...], bd)
    k = _head_norm(z[:, ATTN_W:ATTN_W + KV_W], kg_ref[...], bd)
    v = z[:, ATTN_W + KV_W:ATTN_W + 2 * KV_W]
    knew_ref[...] = k
    vnew_ref[...] = v
    p_ref[...] = z[:, ATTN_W + 2 * KV_W:]
    cos, sin = rc_ref[...], rs_ref[...]
    lane = lax.broadcasted_iota(jnp.int32, (TM, 128), 1)
    low16 = (lane & 31) < 16
    q = _rope(q, cos, sin, low16) * (HEAD_DIM ** -0.5)
    k = _rope(k, cos, sin, low16)
    q_ref[...] = q.astype(BF16)
    kk_ref[...] = jnp.concatenate([k, pltpu.roll(k, 64, 1)], axis=1).astype(BF16)
    vv_ref[...] = jnp.concatenate([v, pltpu.roll(v, 64, 1)], axis=1).astype(BF16)


def _rope_block(t):
    return jnp.where(t < N_PROMPT_SEQ, 0, 1 + (t - N_PROMPT_SEQ) % (DEC_SEQ // TM))


def _attn_in(x, mods, g1, w_in, qg, kg, bd, rope_c, rope_s):
    n_tiles = N_TOK // TM
    tile = lambda w: pl.BlockSpec((TM, w), lambda t: (t, 0))
    const = lambda shape: pl.BlockSpec(shape, lambda t: (0,) * len(shape))
    return pl.pallas_call(
        _a1_kernel,
        grid=(n_tiles,),
        in_specs=[
            tile(D),
            pl.BlockSpec((1, 6, D), lambda t: (_mod_row(t, TM), 0, 0)),
            const((1, D)),
            const((D, A_IN_W)),
            const((1, ATTN_W)),
            const((1, KV_W)),
            const((128, 128)),
            pl.BlockSpec((TM, ATTN_W), lambda t: (_rope_block(t), 0)),
            pl.BlockSpec((TM, ATTN_W), lambda t: (_rope_block(t), 0)),
        ],
        out_specs=[tile(ATTN_W), tile(2 * KV_W), tile(2 * KV_W), tile(KV_W), tile(KV_W), tile(POOL_W)],
        out_shape=[
            jax.ShapeDtypeStruct((N_TOK, ATTN_W), BF16),
            jax.ShapeDtypeStruct((N_TOK, 2 * KV_W), BF16),
            jax.ShapeDtypeStruct((N_TOK, 2 * KV_W), BF16),
            jax.ShapeDtypeStruct((N_TOK, KV_W), F32),
            jax.ShapeDtypeStruct((N_TOK, KV_W), F32),
            jax.ShapeDtypeStruct((N_TOK, POOL_W), F32),
        ],
        compiler_params=_cparams(("arbitrary",)),
        name="attn_in_proj",
    )(x, mods, g1, w_in, qg, kg, bd, rope_c, rope_s)


def _a2_kernel(x_ref, m_ref, q_ref, kk_ref, vv_ref, p_ref, pp_ref, pn_ref, pw_ref, ps_ref,
               wo_ref, o_ref, *, tiles_per_seq):
    t = pl.program_id(0)
    q = q_ref[...]
    kk = kk_ref[0]
    vv = vv_ref[0]
    lane = lax.broadcasted_iota(jnp.int32, (TM, 128), 1)
    half_mask = (lane < HEAD_DIM, lane >= HEAD_DIM)
    zero_q = jnp.zeros((TM, 128), BF16)
    chunks = []
    for c in range(ATTN_W // 128):
        qc = q[:, c * 128:(c + 1) * 128]
        acc = None
        for half in range(2):
            kv = (2 * c + half) // (N_HEADS // N_KV)
            slab = slice(0, 128) if kv == half else slice(128, 256)
            s = _dot_nt(jnp.where(half_mask[half], qc, zero_q), kk[:, slab])
            e = jnp.exp(s - jnp.max(s, axis=-1, keepdims=True))
            l = jnp.sum(e, axis=-1, keepdims=True)
            o = _dot(e.astype(BF16), vv[:, slab]) / l
            o = jnp.where(half_mask[half], o, 0.0)
            acc = o if acc is None else acc + o
        chunks.append(acc)

    pos = t % tiles_per_seq
    has_prev = jnp.where(pos == 0, 0.0, 1.0)
    has_next = jnp.where(pos == tiles_per_seq - 1, 0.0, 1.0)
    p = p_ref[...]
    pext = jnp.concatenate([pp_ref[...] * has_prev, p, pn_ref[...] * has_next], axis=0)
    rows_ext = TM + 2 * HALO
    trow = lax.broadcasted_iota(jnp.int32, (TM, 128), 0).astype(F32)
    for g, w in enumerate(POOL_WINDOWS):
        sl = slice(g * 128, (g + 1) * 128)
        a = pext[:, sl]
        s = a + pltpu.roll(a, 1, 0)
        half = 1
        while 2 * half < w:
            s = pltpu.roll(s, half, 0) + pltpu.roll(s, rows_ext - half, 0)
            half *= 2
        cnt = (w - jnp.maximum(w // 2 - trow, 0.0) * (1.0 - has_prev)
               - jnp.maximum(trow + (w // 2 - TM), 0.0) * (1.0 - has_next))
        dev = s[HALO:HALO + TM] / cnt - p[:, sl]
        chunks.append(_dot(dev.astype(BF16), pw_ref[g]) * ps_ref[:, sl])
    mix = jnp.concatenate(chunks, axis=1).astype(BF16)
    o_ref[...] = x_ref[...] + m_ref[0, 2:3, :] * _dot(mix, wo_ref[...])


def _attn_out(x, mods, q, kk_seq, vv_seq, p, pool_w, pool_s, w_out, *, first_tile, n_tiles, tiles_per_seq):
    s_len = kk_seq.shape[1]
    off = first_tile
    n_halo_blocks = N_TOK // HALO
    per_tile = TM // HALO
    tile = lambda w: pl.BlockSpec((TM, w), lambda t: (t + off, 0))
    const = lambda shape: pl.BlockSpec(shape, lambda t: (0,) * len(shape))
    seq = pl.BlockSpec((1, s_len, 2 * KV_W), lambda t: (t // tiles_per_seq, 0, 0))
    return pl.pallas_call(
        functools.partial(_a2_kernel, tiles_per_seq=tiles_per_seq),
        grid=(n_tiles,),
        in_specs=[
            tile(D),
            pl.BlockSpec((1, 6, D), lambda t: (_mod_row(t + off, TM), 0, 0)),
            tile(ATTN_W),
            seq,
            seq,
            tile(POOL_W),
            pl.BlockSpec((HALO, POOL_W), lambda t: (jnp.maximum((t + off) * per_tile - 1, 0), 0)),
            pl.BlockSpec((HALO, POOL_W), lambda t: (jnp.minimum((t + off + 1) * per_tile, n_halo_blocks - 1), 0)),
            const((len(POOL_WINDOWS), 128, 128)),
            const((1, POOL_W)),
            const((D, D)),
        ],
        out_specs=tile(D),
        out_shape=jax.ShapeDtypeStruct((N_TOK, D), F32),
        input_output_aliases={0: 0},
        compiler_params=_cparams(("arbitrary",)),
        name="attn_pool_out_%d" % s_len,
    )(x, mods, q, kk_seq, vv_seq, p, p, p, pool_w, pool_s, w_out)


def _conv_kernel(x_ref, xp_ref, xn_ref, m_ref, g1_ref, wi_ref, cw_ref, wo_ref, o_ref):
    t = pl.program_id(0)
    is_prompt = t < N_PROMPT_SEQ
    pos = t & (DEC_SEQ // TM - 1)
    has_prev = jnp.where(is_prompt | (pos == 0), 0.0, 1.0)
    has_next = jnp.where(is_prompt | (pos == DEC_SEQ // TM - 1), 0.0, 1.0)
    x = x_ref[...]
    xext = jnp.concatenate([xp_ref[...], x, xn_ref[...]], axis=0)
    h = _modnorm(xext, g1_ref[...], m_ref[0, 0:1, :], m_ref[0, 1:2, :])
    z = _dot(h.astype(BF16), wi_ref[...])
    gb = z[HALO:HALO + TM, :D]
    u = z[:, D:2 * D] * z[:, 2 * D:]
    row = lax.broadcasted_iota(jnp.int32, (TM + 2 * HALO, 1), 0)
    u = u * jnp.where(row < HALO, has_prev, jnp.where(row >= TM + HALO, has_next, 1.0))
    rows = TM + 2 * HALO
    cw = cw_ref[...]
    y = (pltpu.roll(u, 1, 0)[HALO:HALO + TM] * cw[0:1, :]
         + u[HALO:HALO + TM] * cw[1:2, :]
         + pltpu.roll(u, rows - 1, 0)[HALO:HALO + TM] * cw[2:3, :])
    out = _dot((gb * y).astype(BF16), wo_ref[...])
    o_ref[...] = x + m_ref[0, 2:3, :] * out


def _conv_layer(x, mods, g1, w_in, conv_w, w_out):
    n_tiles = N_TOK // TM
    n_halo_blocks = N_TOK // HALO
    per_tile = TM // HALO
    const = lambda shape: pl.BlockSpec(shape, lambda t: (0,) * len(shape))
    return pl.pallas_call(
        _conv_kernel,
        grid=(n_tiles,),
        in_specs=[
            pl.BlockSpec((TM, D), lambda t: (t, 0)),
            pl.BlockSpec((HALO, D), lambda t: (jnp.maximum(t * per_tile - 1, 0), 0)),
            pl.BlockSpec((HALO, D), lambda t: (jnp.minimum((t + 1) * per_tile, n_halo_blocks - 1), 0)),
            pl.BlockSpec((1, 6, D), lambda t: (_mod_row(t, TM), 0, 0)),
            const((1, D)),
            const((D, 3 * D)),
            const((8, D)),
            const((D, D)),
        ],
        out_specs=pl.BlockSpec((TM, D), lambda t: (t, 0)),
        out_shape=jax.ShapeDtypeStruct((N_TOK, D), F32),
        compiler_params=_cparams(("arbitrary",)),
        name="conv_mixer",
    )(x, x, x, mods, g1, w_in, conv_w, w_out)


NEG_INF = float("-inf")
SUB = 8
NO_ROW = float(1 << 20)


def _sort_network(n):
    def merge(lo, hi, r):
        step = r * 2
        if step < hi - lo:
            yield from merge(lo, hi, step)
            yield from merge(lo + r, hi, step)
            yield from [(i, i + r) for i in range(lo + r, hi - r, step)]
        else:
            yield (lo, lo + r)

    def sort(lo, hi):
        if hi - lo >= 1:
            mid = lo + (hi - lo) // 2
            yield from sort(lo, mid)
            yield from sort(mid + 1, hi)
            yield from merge(lo, hi, 1)

    full = 1 << (n - 1).bit_length()
    return tuple((a, b) for a, b in sort(0, full - 1) if b < n)


def _top16_of_blocks(vs, ts, row_of, n_out=PEER_TOPK):
    vs, ts = list(vs), list(ts)
    for a, b in _sort_network(len(vs)):
        va, vb, ta, tb = vs[a], vs[b], ts[a], ts[b]
        if isinstance(ta, float) and isinstance(tb, float):
            first = (va >= vb) if ta < tb else (va > vb)
        else:
            first = (va > vb) | ((va == vb) & (ta < tb))
        vs[a], vs[b] = jnp.maximum(va, vb), jnp.minimum(va, vb)
        ts[a], ts[b] = jnp.where(first, ta, tb), jnp.where(first, tb, ta)
    vals, rows_out = [], []
    for r in range(n_out):
        rows = row_of(ts[0])
        m = jnp.max(vs[0], axis=0, keepdims=True)
        ix = jnp.min(jnp.where(vs[0] == m, rows, NO_ROW), axis=0, keepdims=True)
        vals.append(m)
        rows_out.append(ix)
        taken = rows == ix
        for k in range(min(len(vs), n_out - r - 1)):
            if k + 1 < len(vs):
                vs[k] = jnp.where(taken, vs[k + 1], vs[k])
                ts[k] = jnp.where(taken, ts[k + 1], ts[k])
            else:
                vs[k] = jnp.where(taken, NEG_INF, vs[k])
    return jnp.concatenate(vals, axis=0), jnp.concatenate(rows_out, axis=0)


def _top16_keys(s, sub):
    n = PEER_NKEYS // SUB
    return _top16_of_blocks([s[SUB * v:SUB * (v + 1), :] for v in range(n)], [float(v) for v in range(n)],
                            lambda blk: blk * float(SUB) + sub)


def _pair_candidates(sv1, sv2, sub):
    a_lo, a_hi, b_lo, b_hi = sv1[0:SUB], sv1[SUB:], sv2[0:SUB], sv2[SUB:]
    row = lambda x, r: x[r:r + 1, :]
    keep = lambda cond, v: jnp.where(cond, v, NEG_INF)
    k = float(PEER_TOPK)
    tail = keep((sub >= 2) & (sub <= 4), row(sv1, 2) + b_lo)
    tail_flat = sub + 2 * k
    for r, (a, b) in ((5, (3, 2)), (6, (4, 2)), (7, (3, 3))):
        tail = jnp.where(sub == r, row(sv1, a) + row(sv2, b), tail)
        tail_flat = jnp.where(sub == r, a * k + b, tail_flat)
    return [
        (row(sv1, 0) + b_lo, sub),
        (row(sv1, 0) + b_hi, sub + SUB),
        (keep(sub >= 1, a_lo + row(sv2, 0)), sub * k),
        (a_hi + row(sv2, 0), (sub + SUB) * k),
        (keep(sub >= 1, row(sv1, 1) + b_lo), sub + k),
        (keep(sub >= 2, a_lo + row(sv2, 1)), sub * k + 1),
        (tail, tail_flat),
    ]


def _route_kernel(x_ref, m_ref, g2_ref, wq_ref, sk_ref, gs_ref, hb_ref, gate_ref,
                  q_scr, i1t_scr, i2t_scr, gtt_scr, i1_scr, i2_scr, gt_scr, gscr):
    @pl.when(pl.program_id(0) == 0)
    def _():
        i1_scr[...] = jnp.zeros_like(i1_scr)
        i2_scr[...] = jnp.zeros_like(i2_scr)
        gt_scr[...] = jnp.zeros_like(gt_scr)

    key = lax.broadcasted_iota(jnp.int32, (PEER_NKEYS, 128), 0).astype(F32)

    def expand_token(n):
        row = pl.ds(n, 1)
        left = jnp.where(key == i1_scr[row, :], gt_scr[row, :], 0.0).astype(BF16)
        right = jnp.where(key == i2_scr[row, :], 1.0, 0.0).astype(BF16)
        gscr[pl.ds(n, PEER_NKEYS, stride=G_PITCH), :] = _dot_nt(left, right)

    def copy_out(i, carry):
        rows = gscr[pl.ds(pl.multiple_of(i * G_PITCH, 8), TR), :]
        gate_ref[i] = (rows * gs_ref[pl.ds(i, 1), :]).astype(BF16)
        return carry

    last = pl.num_programs(0) - 1

    @pl.when(pl.program_id(0) == last)
    def _():
        hb_ref[...] = _modnorm(x_ref[...], g2_ref[...], m_ref[0, 3:4, :], m_ref[0, 4:5, :]).astype(BF16)

        def expand_group(j, carry):
            for k in range(16):
                expand_token(j * 16 + k)
            return carry

        lax.fori_loop(0, TR // 16, expand_group, 0)
        lax.fori_loop(0, PEER_NKEYS, copy_out, 0, unroll=4)

    @pl.when(pl.program_id(0) < last)
    def _():
        _route_select(x_ref, m_ref, g2_ref, wq_ref, sk_ref, hb_ref, q_scr, i1t_scr, i2t_scr, gtt_scr, expand_token)
        lax.fori_loop(0, PEER_NKEYS, copy_out, 0, unroll=4)
        i1_scr[...] = i1t_scr[...].T
        i2_scr[...] = i2t_scr[...].T
        gt_scr[...] = gtt_scr[...].T


def _route_select(x_ref, m_ref, g2_ref, wq_ref, sk_ref, hb_ref, q_scr, i1t_scr, i2t_scr, gtt_scr, expand_token):
    h = _modnorm(x_ref[...], g2_ref[...], m_ref[0, 3:4, :], m_ref[0, 4:5, :])
    hb = h.astype(BF16)
    hb_ref[...] = hb
    q = _dot(hb, wq_ref[...])
    for hp in range(2 * PEER_HEADS):
        q_scr[hp] = q[:, hp * PEER_DK:(hp + 1) * PEER_DK]
    sub = lax.broadcasted_iota(jnp.int32, (SUB, TR), 0).astype(F32)
    sk_bf = [sk_ref[p].astype(BF16) for p in range(2)]

    def key_stage(hd):
        top = ()
        for p in range(2):
            s = _dot_nt(sk_bf[p], q_scr[2 * hd + p].astype(BF16))
            top += _top16_keys(s, sub)
        return top

    def pair_stage(hd, top):
        sv1, si1, sv2, si2 = top
        cands = _pair_candidates(sv1, sv2, sub)
        fv, ff = _top16_of_blocks([c[0] for c in cands], [c[1] for c in cands], lambda flat: flat)
        ff = ff.astype(jnp.int32)
        fa, fb = ff >> 4, ff & (PEER_TOPK - 1)
        i1 = jnp.zeros((PEER_TOPK, TR), F32)
        i2 = jnp.zeros((PEER_TOPK, TR), F32)
        for a in range(PEER_TOPK):
            i1 = jnp.where(fa == a, si1[a:a + 1, :], i1)
            i2 = jnp.where(fb == a, si2[a:a + 1, :], i2)
        e = jnp.exp(fv - fv[0:1, :])
        g = e / jnp.sum(e, axis=0, keepdims=True)
        start = hd * PEER_TOPK
        rows = pl.ds(start if isinstance(start, int) else pl.multiple_of(start, PEER_TOPK), PEER_TOPK)
        i1t_scr[rows, :] = i1
        i2t_scr[rows, :] = i2
        gtt_scr[rows, :] = g

    n_first, n_mid = TR // 8, (TR - TR // 8) // 8
    top = key_stage(0)
    for k in range(n_first):
        expand_token(k)

    def per_head(hd, top_prev):
        top_cur = key_stage(hd)
        pair_stage(hd - 1, top_prev)
        for k in range(n_mid):
            expand_token(n_first + (hd - 1) * n_mid + k)
        return top_cur

    top = lax.fori_loop(1, PEER_HEADS, per_head, top, unroll=True)
    pair_stage(PEER_HEADS - 1, top)
    for n in range(n_first + (PEER_HEADS - 1) * n_mid, TR):
        expand_token(n)


def _peer_route(x, mods, g2, wq, subkeys, gate_scale):
    n_tiles = N_TOK // TR
    const = lambda shape: pl.BlockSpec(shape, lambda t: (0,) * len(shape))
    cur = lambda t: jnp.minimum(t, n_tiles - 1)
    return pl.pallas_call(
        _route_kernel,
        grid=(n_tiles + 1,),
        in_specs=[
            pl.BlockSpec((TR, D), lambda t: (cur(t), 0)),
            pl.BlockSpec((1, 6, D), lambda t: (_mod_row(cur(t), TR), 0, 0)),
            const((1, D)),
            const((D, PEER_HEADS * 2 * PEER_DK)),
            const((2, PEER_NKEYS, PEER_DK)),
            const((PEER_NKEYS, PEER_NKEYS)),
        ],
        out_specs=[
            pl.BlockSpec((TR, D), lambda t: (cur(t), 0)),
            pl.BlockSpec((PEER_NKEYS, TR, PEER_NKEYS), lambda t: (0, jnp.maximum(t - 1, 0), 0)),
        ],
        out_shape=[
            jax.ShapeDtypeStruct((N_TOK, D), BF16),
            jax.ShapeDtypeStruct((PEER_NKEYS, N_TOK, PEER_NKEYS), BF16),
        ],
        scratch_shapes=[
            pltpu.VMEM((2 * PEER_HEADS, TR, PEER_DK), F32),
            pltpu.VMEM((PEER_HEADS * PEER_TOPK, TR), F32),
            pltpu.VMEM((PEER_HEADS * PEER_TOPK, TR), F32),
            pltpu.VMEM((PEER_HEADS * PEER_TOPK, TR), F32),
            pltpu.VMEM((TR, PEER_HEADS * PEER_TOPK), F32),
            pltpu.VMEM((TR, PEER_HEADS * PEER_TOPK), F32),
            pltpu.VMEM((TR, PEER_HEADS * PEER_TOPK), F32),
            pltpu.VMEM((PEER_NKEYS * G_PITCH, PEER_NKEYS), F32),
        ],
        compiler_params=_cparams(("arbitrary",)),
        name="peer_route",
    )(x, mods, g2, wq, subkeys, gate_scale)


F8 = jnp.float8_e4m3fn
F8_TOP = 256.0
TINY = 1e-30


def _row_scale(a):
    return jnp.maximum(jnp.max(jnp.abs(a), axis=1, keepdims=True), TINY) * (1.0 / F8_TOP)


def _quant_rows_kernel(w_ref, q_ref, s_ref):
    w = w_ref[0]
    scale = _row_scale(w)
    q_ref[0] = (w * (1.0 / scale)).astype(F8)
    s_ref[0, 0] = jnp.transpose(jnp.broadcast_to(scale, (EC, 128)))[0:1, :]


def _quantize_rows(w_all):
    n_chunks = PEER_EXPERTS // EC
    return pl.pallas_call(
        _quant_rows_kernel,
        grid=(DEPTH, n_chunks),
        in_specs=[pl.BlockSpec((1, EC, D), lambda l, j: (l, j, 0))],
        out_specs=[pl.BlockSpec((1, EC, D), lambda l, j: (l, j, 0)),
                   pl.BlockSpec((1, 1, 1, EC), lambda l, j: (l, j, 0, 0))],
        out_shape=[jax.ShapeDtypeStruct((DEPTH, PEER_EXPERTS, D), F8),
                   jax.ShapeDtypeStruct((DEPTH, n_chunks, 1, EC), F32)],
        compiler_params=_cparams(("arbitrary", "arbitrary")),
        name="quantize_rows",
    )(w_all)


def _expert_kernel(x_ref, m_ref, hb_ref, gate_ref, u8_ref, su_ref, v8_ref, *rest):
    o_ref, h8_scr, t_scr, hn_scr = rest[-4:]
    c = pl.program_id(1)

    @pl.when(c == 0)
    def _():
        o_ref[...] = jnp.zeros_like(o_ref)
        hf = hb_ref[...].astype(F32)
        t = jnp.maximum(jnp.max(jnp.abs(hf), keepdims=True), TINY) * (1.0 / F8_TOP)
        t_scr[...] = t
        hq = (hf * (1.0 / t)).astype(F8)
        h8_scr[...] = hq
        hq = hq.astype(F32)
        hn_scr[...] = jnp.sqrt(jnp.sum(hq * hq, axis=1, keepdims=True))

    h8 = h8_scr[...]
    scale = su_ref[0, 0] * (t_scr[...] * (2.0 ** -0.5))
    n_slabs = EC // PEER_NKEYS
    gmax = gate_ref[0]
    for a in range(1, n_slabs):
        gmax = jnp.maximum(gmax, gate_ref[a])
    gmax = jnp.max(gmax.astype(F32), axis=1, keepdims=True)
    z_bound = hn_scr[...] * (jnp.max(scale, axis=1, keepdims=True) * (1.01 * (D ** 0.5) * F8_TOP))
    r = jnp.maximum(gmax * z_bound * (2.0 / F8_TOP), TINY)
    inv_r = 1.0 / r
    slabs = []
    for s in range(EC // EC_SUB):
        cols = slice(s * EC_SUB, (s + 1) * EC_SUB)
        z = _dot_nt(h8, u8_ref[0, cols, :]) * scale[:, cols]
        act = z * (1.0 + lax.erf(z))
        for a in range(EC_SUB // PEER_NKEYS):
            gate = gate_ref[s * (EC_SUB // PEER_NKEYS) + a].astype(F32) * inv_r
            slabs.append((act[:, a * PEER_NKEYS:(a + 1) * PEER_NKEYS] * gate).astype(F8))
    o_ref[...] += _dot(jnp.concatenate(slabs, axis=1), v8_ref[0]) * r

    @pl.when(c == pl.num_programs(1) - 1)
    def _():
        y = x_ref[...] + m_ref[0, 5:6, :] * o_ref[...]
        if len(rest) == 5:
            y = y * lax.rsqrt(jnp.mean(y * y, axis=-1, keepdims=True) + EPS) * rest[0][...]
        o_ref[...] = y


def _peer_experts(x, mods, hb, gate, u8_all, su_all, v8_all, layer, final_g=None):
    n_tiles = N_TOK // TE
    n_chunks = PEER_EXPERTS // EC
    once = pl.Buffered(1)
    in_specs = [
        pl.BlockSpec((TE, D), lambda t, c: (t, 0), pipeline_mode=once),
        pl.BlockSpec((1, 6, D), lambda t, c: (_mod_row(t, TE), 0, 0)),
        pl.BlockSpec((TE, D), lambda t, c: (t, 0), pipeline_mode=once),
        pl.BlockSpec((EC // PEER_NKEYS, TE, PEER_NKEYS), lambda t, c: (c, t, 0)),
        pl.BlockSpec((1, EC, D), lambda t, c: (layer, c, 0)),
        pl.BlockSpec((1, 1, 1, EC), lambda t, c: (layer, c, 0, 0)),
        pl.BlockSpec((1, EC, D), lambda t, c: (layer, c, 0)),
    ]
    operands = [x, mods, hb, gate, u8_all, su_all, v8_all]
    if final_g is not None:
        in_specs.append(pl.BlockSpec((1, D), lambda t, c: (0, 0)))
        operands.append(final_g)
    return pl.pallas_call(
        _expert_kernel,
        grid=(n_tiles, n_chunks),
        in_specs=in_specs,
        out_specs=pl.BlockSpec((TE, D), lambda t, c: (t, 0)),
        out_shape=jax.ShapeDtypeStruct((N_TOK, D), F32),
        scratch_shapes=[pltpu.VMEM((TE, D), F8), pltpu.VMEM((1, 1), F32), pltpu.VMEM((TE, 1), F32)],
        compiler_params=_cparams(("arbitrary", "arbitrary")),
        name="peer_experts",
    )(*operands)


def _rope_tables():
    t = jnp.arange(DEC_SEQ)
    row, col = t // GRID_W, t % GRID_W
    nf = HEAD_DIM // 4
    inv = ROPE_THETA ** (-jnp.arange(nf, dtype=F32) / nf)
    ang_r = row.astype(F32)[:, None] * inv[None, :]
    ang_c = col.astype(F32)[:, None] * inv[None, :]
    cos = jnp.concatenate([jnp.cos(ang_r)] * 2 + [jnp.cos(ang_c)] * 2, axis=1)
    sin = jnp.concatenate([-jnp.sin(ang_r), jnp.sin(ang_r), -jnp.sin(ang_c), jnp.sin(ang_c)], axis=1)
    cos = jnp.concatenate([jnp.ones((TM, HEAD_DIM), F32), cos], axis=0)
    sin = jnp.concatenate([jnp.zeros((TM, HEAD_DIM), F32), sin], axis=0)
    return jnp.tile(cos, (1, N_HEADS)), jnp.tile(sin, (1, N_HEADS))


def _head_block_diag():
    i = np.arange(128)
    return jnp.asarray((i[:, None] // HEAD_DIM == i[None, :] // HEAD_DIM).astype(np.float32), dtype=BF16)


def kernel(x_prompt, x_sample, cache_k, cache_v, c, c_ctx, mod_w, mod_b, norm1_g, norm2_g, a_in_w, a_out_w,
           q_norm_g, k_norm_g, pool_w, pool_scale, c_in_w, c_conv_w, c_out_w, peer_wq, peer_subkeys, peer_u,
           peer_v, final_g):
    x = jnp.concatenate([x_prompt.reshape(NP_TOK, D), x_sample.reshape(NS_TOK, D)], axis=0)
    ct = jnp.zeros((D, 8), F32).at[:, 0].set(c_ctx).at[:, 1:3].set(c.T)
    mods = _modulation(ct, mod_w, mod_b).reshape(DEPTH, 8, 6, D)
    rope_c, rope_s = _rope_tables()
    bd = _head_block_diag()
    wq = peer_wq.astype(BF16)
    u8, su = _quantize_rows(peer_u)
    v8, sv = _quantize_rows(peer_v)
    new_k, new_v = [], []
    for l in range(DEPTH):
        g1 = norm1_g[l].reshape(1, D)
        if l % 2 == 0:
            i = l // 2
            q, kk, vv, knew, vnew, p = _attn_in(
                x, mods[l], g1, a_in_w[i].astype(BF16), jnp.tile(q_norm_g[i], N_HEADS).reshape(1, ATTN_W),
                jnp.tile(k_norm_g[i], N_KV).reshape(1, KV_W), bd, rope_c, rope_s)
            new_k.append(knew[:NP_TOK].reshape(N_PROMPT_SEQ, SEQ, N_KV, HEAD_DIM))
            new_v.append(vnew[:NP_TOK].reshape(N_PROMPT_SEQ, SEQ, N_KV, HEAD_DIM))
            ck = cache_k[:, i]
            cv = cache_v[:, i]
            ctx_kk = jnp.concatenate([ck, ck[:, :, ::-1]], axis=2).reshape(N_SAMPLE_SEQ, PAST, 2 * KV_W).astype(BF16)
            ctx_vv = jnp.concatenate([cv, cv[:, :, ::-1]], axis=2).reshape(N_SAMPLE_SEQ, PAST, 2 * KV_W).astype(BF16)
            kk_p = kk[:NP_TOK].reshape(N_PROMPT_SEQ, SEQ, 2 * KV_W)
            vv_p = vv[:NP_TOK].reshape(N_PROMPT_SEQ, SEQ, 2 * KV_W)
            kk_s = jnp.concatenate([kk[NP_TOK:].reshape(N_SAMPLE_SEQ, DEC_SEQ, 2 * KV_W), ctx_kk], axis=1)
            vv_s = jnp.concatenate([vv[NP_TOK:].reshape(N_SAMPLE_SEQ, DEC_SEQ, 2 * KV_W), ctx_vv], axis=1)
            common = (pool_w[i].astype(BF16), pool_scale[i].reshape(1, POOL_W), a_out_w[i].astype(BF16))
            x = _attn_out(x, mods[l], q, kk_p, vv_p, p, *common,
                          first_tile=0, n_tiles=N_PROMPT_SEQ, tiles_per_seq=SEQ // TM)
            x = _attn_out(x, mods[l], q, kk_s, vv_s, p, *common,
                          first_tile=N_PROMPT_SEQ, n_tiles=NS_TOK // TM, tiles_per_seq=DEC_SEQ // TM)
        else:
            j = l // 2
            cw = jnp.zeros((8, D), F32).at[:3].set(c_conv_w[j])
            x = _conv_layer(x, mods[l], g1, c_in_w[j].astype(BF16), cw, c_out_w[j].astype(BF16))
        gate_scale = (2.0 ** -0.5) * sv[l].reshape(PEER_NKEYS, PEER_NKEYS)
        hb, gate = _peer_route(x, mods[l], norm2_g[l].reshape(1, D), wq[l], peer_subkeys[l], gate_scale)
        x = _peer_experts(x, mods[l], hb, gate, u8, su, v8, l,
                          final_g=final_g.reshape(1, D) if l == DEPTH - 1 else None)
    y = x
    y_prompt = y[:NP_TOK].reshape(N_PROMPT_SEQ, SEQ, D)
    y_sample = y[NP_TOK:].reshape(N_SAMPLE_SEQ, DEC_SEQ, D)
    return (y_prompt, y_sample, jnp.stack(new_k, axis=1), jnp.stack(new_v, axis=1))
```

```python
import functools

import numpy as np
import jax
import jax.numpy as jnp
from jax import lax
from jax.experimental import pallas as pl
from jax.experimental.pallas import tpu as pltpu

F32 = jnp.float32
BF16 = jnp.bfloat16

D = 1024
N_PROMPT_SEQ, SEQ = 32, 256
N_SAMPLE_SEQ, DEC_SEQ = 2, 2048
PAST = 256
NP_TOK = N_PROMPT_SEQ * SEQ
NS_TOK = N_SAMPLE_SEQ * DEC_SEQ
N_TOK = NP_TOK + NS_TOK
DEPTH = 4
GRID_W = 64
N_HEADS, N_KV, HEAD_DIM = 8, 2, 64
ATTN_W, KV_W, POOL_W = 512, 128, 512
POOL_WINDOWS = (2, 4, 8, 16)
A_IN_W = ATTN_W + 2 * KV_W + POOL_W
PEER_HEADS, PEER_NKEYS, PEER_TOPK, PEER_DK = 8, 128, 16, 128
PEER_EXPERTS = PEER_NKEYS * PEER_NKEYS
ROPE_THETA = 10000.0
EPS = 1e-6

TM = 256
HALO = 8
TR = 256
G_PITCH = TR + 8
TE = 1024
EC = 2048
EC_SUB = 512
TE_SUB = 512
V7X_VMEM_BYTES = 64 * 1024 * 1024
VMEM_LIMIT = V7X_VMEM_BYTES - 8 * 1024 * 1024


def _cparams(sem):
    return pltpu.CompilerParams(dimension_semantics=sem, vmem_limit_bytes=VMEM_LIMIT)


def _mod_row(tile, tile_tokens):
    start = tile * tile_tokens
    return (start >= NP_TOK).astype(jnp.int32) + (start >= NP_TOK + DEC_SEQ).astype(jnp.int32)


def _modnorm(x, g, shift, scale):
    ms = jnp.mean(x * x, axis=-1, keepdims=True)
    y = x * lax.rsqrt(ms + EPS) * g
    return y * (1.0 + scale) + shift


def _split_bf16(a):
    hi = a.astype(BF16)
    lo = (a - hi.astype(F32)).astype(BF16)
    return hi, lo


def _dot(a, b):
    return jnp.dot(a, b, preferred_element_type=F32)


def _dot_nt(a, b):
    return lax.dot_general(a, b, (((1,), (1,)), ((), ())), preferred_element_type=F32)


def _mod_kernel(ct_ref, w_ref, b_ref, o_ref):
    ct = ct_ref[...]
    s = ct * jax.nn.sigmoid(ct)
    w = w_ref[0]
    rows = [jnp.sum(s[:, r:r + 1] * w, axis=0, keepdims=True) + b_ref[0] for r in range(3)]
    rows.append(jnp.zeros((5, w.shape[1]), F32))
    o_ref[0] = jnp.concatenate(rows, axis=0)


def _modulation(ct, mod_w, mod_b):
    tn = 1536
    return pl.pallas_call(
        _mod_kernel,
        grid=(DEPTH, 6 * D // tn),
        in_specs=[
            pl.BlockSpec((D, 8), lambda l, j: (0, 0)),
            pl.BlockSpec((1, D, tn), lambda l, j: (l, 0, j)),
            pl.BlockSpec((1, 1, tn), lambda l, j: (l, 0, j)),
        ],
        out_specs=pl.BlockSpec((1, 8, tn), lambda l, j: (l, 0, j)),
        out_shape=jax.ShapeDtypeStruct((DEPTH, 8, 6 * D), F32),
        compiler_params=_cparams(("arbitrary", "arbitrary")),
        name="adaln_modulation",
    )(ct, mod_w, mod_b.reshape(DEPTH, 1, 6 * D))


def _head_norm(x, g, bd):
    outs = []
    for c in range(x.shape[1] // 128):
        xc = x[:, c * 128:(c + 1) * 128]
        hi, lo = _split_bf16(xc * xc)
        ms = (_dot(hi, bd) + _dot(lo, bd)) * (1.0 / HEAD_DIM)
        outs.append(xc * lax.rsqrt(ms + EPS))
    return jnp.concatenate(outs, axis=1) * g


def _rope(x, cos, sin, low16):
    outs = []
    for c in range(x.shape[1] // 128):
        sl = slice(c * 128, (c + 1) * 128)
        xc = x[:, sl]
        partner = jnp.where(low16, pltpu.roll(xc, 112, 1), pltpu.roll(xc, 16, 1))
        outs.append(xc * cos[:, sl] + partner * sin[:, sl])
    return jnp.concatenate(outs, axis=1)


def _a1_kernel(x_ref, m_ref, g1_ref, w_ref, qg_ref, kg_ref, bd_ref, rc_ref, rs_ref,
               q_ref, kk_ref, vv_ref, knew_ref, vnew_ref, p_ref):
    h = _modnorm(x_ref[...], g1_ref[...], m_ref[0, 0:1, :], m_ref[0, 1:2, :])
    z = _dot(h.astype(BF16), w_ref[...])
    bd = bd_ref[...]
    q = _head_norm(z[:, :ATTN_W], qg_ref[...], bd)
    k = _head_norm(z[:, ATTN_W:ATTN_W + KV_W], kg_ref[...], bd)
    v = z[:, ATTN_W + KV_W:ATTN_W + 2 * KV_W]
    knew_ref[...] = k
    vnew_ref[...] = v
    p_ref[...] = z[:, ATTN_W + 2 * KV_W:]
    cos, sin = rc_ref[...], rs_ref[...]
    lane = lax.broadcasted_iota(jnp.int32, (TM, 128), 1)
    low16 = (lane & 31) < 16
    q = _rope(q, cos, sin, low16) * (HEAD_DIM ** -0.5)
    k = _rope(k, cos, sin, low16)
    q_ref[...] = q.astype(BF16)
    kk_ref[...] = jnp.concatenate([k, pltpu.roll(k, 64, 1)], axis=1).astype(BF16)
    vv_ref[...] = jnp.concatenate([v, pltpu.roll(v, 64, 1)], axis=1).astype(BF16)


def _rope_block(t):
    return jnp.where(t < N_PROMPT_SEQ, 0, 1 + (t - N_PROMPT_SEQ) % (DEC_SEQ // TM))


def _attn_in(x, mods, g1, w_in, qg, kg, bd, rope_c, rope_s):
    n_tiles = N_TOK // TM
    tile = lambda w: pl.BlockSpec((TM, w), lambda t: (t, 0))
    const = lambda shape: pl.BlockSpec(shape, lambda t: (0,) * len(shape))
    return pl.pallas_call(
        _a1_kernel,
        grid=(n_tiles,),
        in_specs=[
            tile(D),
            pl.BlockSpec((1, 6, D), lambda t: (_mod_row(t, TM), 0, 0)),
            const((1, D)),
            const((D, A_IN_W)),
            const((1, ATTN_W)),
            const((1, KV_W)),
            const((128, 128)),
            pl.BlockSpec((TM, ATTN_W), lambda t: (_rope_block(t), 0)),
            pl.BlockSpec((TM, ATTN_W), lambda t: (_rope_block(t), 0)),
        ],
        out_specs=[tile(ATTN_W), tile(2 * KV_W), tile(2 * KV_W), tile(KV_W), tile(KV_W), tile(POOL_W)],
        out_shape=[
            jax.ShapeDtypeStruct((N_TOK, ATTN_W), BF16),
            jax.ShapeDtypeStruct((N_TOK, 2 * KV_W), BF16),
            jax.ShapeDtypeStruct((N_TOK, 2 * KV_W), BF16),
            jax.ShapeDtypeStruct((N_TOK, KV_W), F32),
            jax.ShapeDtypeStruct((N_TOK, KV_W), F32),
            jax.ShapeDtypeStruct((N_TOK, POOL_W), F32),
        ],
        compiler_params=_cparams(("arbitrary",)),
        name="attn_in_proj",
    )(x, mods, g1, w_in, qg, kg, bd, rope_c, rope_s)


def _a2_kernel(x_ref, m_ref, q_ref, kk_ref, vv_ref, p_ref, pp_ref, pn_ref, pw_ref, ps_ref,
               wo_ref, o_ref, *, tiles_per_seq):
    t = pl.program_id(0)
    q = q_ref[...]
    kk = kk_ref[0]
    vv = vv_ref[0]
    lane = lax.broadcasted_iota(jnp.int32, (TM, 128), 1)
    half_mask = (lane < HEAD_DIM, lane >= HEAD_DIM)
    zero_q = jnp.zeros((TM, 128), BF16)
    chunks = []
    for c in range(ATTN_W // 128):
        qc = q[:, c * 128:(c + 1) * 128]
        acc = None
        for half in range(2):
            kv = (2 * c + half) // (N_HEADS // N_KV)
            slab = slice(0, 128) if kv == half else slice(128, 256)
            s = _dot_nt(jnp.where(half_mask[half], qc, zero_q), kk[:, slab])
            e = jnp.exp(s - jnp.max(s, axis=-1, keepdims=True))
            l = jnp.sum(e, axis=-1, keepdims=True)
            o = _dot(e.astype(BF16), vv[:, slab]) / l
            o = jnp.where(half_mask[half], o, 0.0)
            acc = o if acc is None else acc + o
        chunks.append(acc)

    pos = t % tiles_per_seq
    has_prev = jnp.where(pos == 0, 0.0, 1.0)
    has_next = jnp.where(pos == tiles_per_seq - 1, 0.0, 1.0)
    p = p_ref[...]
    pext = jnp.concatenate([pp_ref[...] * has_prev, p, pn_ref[...] * has_next], axis=0)
    rows_ext = TM + 2 * HALO
    trow = lax.broadcasted_iota(jnp.int32, (TM, 128), 0).astype(F32)
    for g, w in enumerate(POOL_WINDOWS):
        sl = slice(g * 128, (g + 1) * 128)
        a = pext[:, sl]
        s = a + pltpu.roll(a, 1, 0)
        half = 1
        while 2 * half < w:
            s = pltpu.roll(s, half, 0) + pltpu.roll(s, rows_ext - half, 0)
            half *= 2
        cnt = (w - jnp.maximum(w // 2 - trow, 0.0) * (1.0 - has_prev)
               - jnp.maximum(trow + (w // 2 - TM), 0.0) * (1.0 - has_next))
        dev = s[HALO:HALO + TM] / cnt - p[:, sl]
        chunks.append(_dot(dev.astype(BF16), pw_ref[g]) * ps_ref[:, sl])
    mix = jnp.concatenate(chunks, axis=1).astype(BF16)
    o_ref[...] = x_ref[...] + m_ref[0, 2:3, :] * _dot(mix, wo_ref[...])


def _attn_out(x, mods, q, kk_seq, vv_seq, p, pool_w, pool_s, w_out, *, first_tile, n_tiles, tiles_per_seq):
    s_len = kk_seq.shape[1]
    off = first_tile
    n_halo_blocks = N_TOK // HALO
    per_tile = TM // HALO
    tile = lambda w: pl.BlockSpec((TM, w), lambda t: (t + off, 0))
    const = lambda shape: pl.BlockSpec(shape, lambda t: (0,) * len(shape))
    seq = pl.BlockSpec((1, s_len, 2 * KV_W), lambda t: (t // tiles_per_seq, 0, 0))
    return pl.pallas_call(
        functools.partial(_a2_kernel, tiles_per_seq=tiles_per_seq),
        grid=(n_tiles,),
        in_specs=[
            tile(D),
            pl.BlockSpec((1, 6, D), lambda t: (_mod_row(t + off, TM), 0, 0)),
            tile(ATTN_W),
            seq,
            seq,
            tile(POOL_W),
            pl.BlockSpec((HALO, POOL_W), lambda t: (jnp.maximum((t + off) * per_tile - 1, 0), 0)),
            pl.BlockSpec((HALO, POOL_W), lambda t: (jnp.minimum((t + off + 1) * per_tile, n_halo_blocks - 1), 0)),
            const((len(POOL_WINDOWS), 128, 128)),
            const((1, POOL_W)),
            const((D, D)),
        ],
        out_specs=tile(D),
        out_shape=jax.ShapeDtypeStruct((N_TOK, D), F32),
        input_output_aliases={0: 0},
        compiler_params=_cparams(("arbitrary",)),
        name="attn_pool_out_%d" % s_len,
    )(x, mods, q, kk_seq, vv_seq, p, p, p, pool_w, pool_s, w_out)


def _conv_kernel(x_ref, xp_ref, xn_ref, m_ref, g1_ref, wi_ref, cw_ref, wo_ref, o_ref):
    t = pl.program_id(0)
    is_prompt = t < N_PROMPT_SEQ
    pos = t & (DEC_SEQ // TM - 1)
    has_prev = jnp.where(is_prompt | (pos == 0), 0.0, 1.0)
    has_next = jnp.where(is_prompt | (pos == DEC_SEQ // TM - 1), 0.0, 1.0)
    x = x_ref[...]
    xext = jnp.concatenate([xp_ref[...], x, xn_ref[...]], axis=0)
    h = _modnorm(xext, g1_ref[...], m_ref[0, 0:1, :], m_ref[0, 1:2, :])
    z = _dot(h.astype(BF16), wi_ref[...])
    gb = z[HALO:HALO + TM, :D]
    u = z[:, D:2 * D] * z[:, 2 * D:]
    row = lax.broadcasted_iota(jnp.int32, (TM + 2 * HALO, 1), 0)
    u = u * jnp.where(row < HALO, has_prev, jnp.where(row >= TM + HALO, has_next, 1.0))
    rows = TM + 2 * HALO
    cw = cw_ref[...]
    y = (pltpu.roll(u, 1, 0)[HALO:HALO + TM] * cw[0:1, :]
         + u[HALO:HALO + TM] * cw[1:2, :]
         + pltpu.roll(u, rows - 1, 0)[HALO:HALO + TM] * cw[2:3, :])
    out = _dot((gb * y).astype(BF16), wo_ref[...])
    o_ref[...] = x + m_ref[0, 2:3, :] * out


def _conv_layer(x, mods, g1, w_in, conv_w, w_out):
    n_tiles = N_TOK // TM
    n_halo_blocks = N_TOK // HALO
    per_tile = TM // HALO
    const = lambda shape: pl.BlockSpec(shape, lambda t: (0,) * len(shape))
    return pl.pallas_call(
        _conv_kernel,
        grid=(n_tiles,),
        in_specs=[
            pl.BlockSpec((TM, D), lambda t: (t, 0)),
            pl.BlockSpec((HALO, D), lambda t: (jnp.maximum(t * per_tile - 1, 0), 0)),
            pl.BlockSpec((HALO, D), lambda t: (jnp.minimum((t + 1) * per_tile, n_halo_blocks - 1), 0)),
            pl.BlockSpec((1, 6, D), lambda t: (_mod_row(t, TM), 0, 0)),
            const((1, D)),
            const((D, 3 * D)),
            const((8, D)),
            const((D, D)),
        ],
        out_specs=pl.BlockSpec((TM, D), lambda t: (t, 0)),
        out_shape=jax.ShapeDtypeStruct((N_TOK, D), F32),
        compiler_params=_cparams(("arbitrary",)),
        name="conv_mixer",
    )(x, x, x, mods, g1, w_in, conv_w, w_out)


NEG_INF = float("-inf")
SUB = 8
NO_ROW = float(1 << 20)


def _sort_network(n):
    def merge(lo, hi, r):
        step = r * 2
        if step < hi - lo:
            yield from merge(lo, hi, step)
            yield from merge(lo + r, hi, step)
            yield from [(i, i + r) for i in range(lo + r, hi - r, step)]
        else:
            yield (lo, lo + r)

    def sort(lo, hi):
        if hi - lo >= 1:
            mid = lo + (hi - lo) // 2
            yield from sort(lo, mid)
            yield from sort(mid + 1, hi)
            yield from merge(lo, hi, 1)

    full = 1 << (n - 1).bit_length()
    return tuple((a, b) for a, b in sort(0, full - 1) if b < n)


def _top16_of_blocks(vs, ts, row_of, n_out=PEER_TOPK):
    vs, ts = list(vs), list(ts)
    for a, b in _sort_network(len(vs)):
        va, vb, ta, tb = vs[a], vs[b], ts[a], ts[b]
        if isinstance(ta, float) and isinstance(tb, float):
            first = (va >= vb) if ta < tb else (va > vb)
        else:
            first = (va > vb) | ((va == vb) & (ta < tb))
        vs[a], vs[b] = jnp.maximum(va, vb), jnp.minimum(va, vb)
        ts[a], ts[b] = jnp.where(first, ta, tb), jnp.where(first, tb, ta)
    vals, rows_out = [], []
    for r in range(n_out):
        rows = row_of(ts[0])
        m = jnp.max(vs[0], axis=0, keepdims=True)
        ix = jnp.min(jnp.where(vs[0] == m, rows, NO_ROW), axis=0, keepdims=True)
        vals.append(m)
        rows_out.append(ix)
        taken = rows == ix
        for k in range(min(len(vs), n_out - r - 1)):
            if k + 1 < len(vs):
                vs[k] = jnp.where(taken, vs[k + 1], vs[k])
                ts[k] = jnp.where(taken, ts[k + 1], ts[k])
            else:
                vs[k] = jnp.where(taken, NEG_INF, vs[k])
    return jnp.concatenate(vals, axis=0), jnp.concatenate(rows_out, axis=0)


def _top16_keys(s, sub):
    n = PEER_NKEYS // SUB
    return _top16_of_blocks([s[SUB * v:SUB * (v + 1), :] for v in range(n)], [float(v) for v in range(n)],
                            lambda blk: blk * float(SUB) + sub)


def _pair_candidates(sv1, sv2, sub):
    a_lo, a_hi, b_lo, b_hi = sv1[0:SUB], sv1[SUB:], sv2[0:SUB], sv2[SUB:]
    row = lambda x, r: x[r:r + 1, :]
    keep = lambda cond, v: jnp.where(cond, v, NEG_INF)
    k = float(PEER_TOPK)
    tail = keep((sub >= 2) & (sub <= 4), row(sv1, 2) + b_lo)
    tail_flat = sub + 2 * k
    for r, (a, b) in ((5, (3, 2)), (6, (4, 2)), (7, (3, 3))):
        tail = jnp.where(sub == r, row(sv1, a) + row(sv2, b), tail)
        tail_flat = jnp.where(sub == r, a * k + b, tail_flat)
    return [
        (row(sv1, 0) + b_lo, sub),
        (row(sv1, 0) + b_hi, sub + SUB),
        (keep(sub >= 1, a_lo + row(sv2, 0)), sub * k),
        (a_hi + row(sv2, 0), (sub + SUB) * k),
        (keep(sub >= 1, row(sv1, 1) + b_lo), sub + k),
        (keep(sub >= 2, a_lo + row(sv2, 1)), sub * k + 1),
        (tail, tail_flat),
    ]


def _route_kernel(x_ref, m_ref, g2_ref, wq_ref, sk_ref, gs_ref, hb_ref, gate_ref,
                  q_scr, i1t_scr, i2t_scr, gtt_scr, i1_scr, i2_scr, gt_scr, gscr):
    @pl.when(pl.program_id(0) == 0)
    def _():
        i1_scr[...] = jnp.zeros_like(i1_scr)
        i2_scr[...] = jnp.zeros_like(i2_scr)
        gt_scr[...] = jnp.zeros_like(gt_scr)

    key = lax.broadcasted_iota(jnp.int32, (PEER_NKEYS, 128), 0).astype(F32)

    def expand_token(n):
        row = pl.ds(n, 1)
        left = jnp.where(key == i1_scr[row, :], gt_scr[row, :], 0.0).astype(BF16)
        right = jnp.where(key == i2_scr[row, :], 1.0, 0.0).astype(BF16)
        gscr[pl.ds(n, PEER_NKEYS, stride=G_PITCH), :] = _dot_nt(left, right)

    def copy_out(i, carry):
        rows = gscr[pl.ds(pl.multiple_of(i * G_PITCH, 8), TR), :]
        gate_ref[i] = (rows * gs_ref[pl.ds(i, 1), :]).astype(BF16)
        return carry

    last = pl.num_programs(0) - 1

    @pl.when(pl.program_id(0) == last)
    def _():
        hb_ref[...] = _modnorm(x_ref[...], g2_ref[...], m_ref[0, 3:4, :], m_ref[0, 4:5, :]).astype(BF16)

        def expand_group(j, carry):
            for k in range(16):
                expand_token(j * 16 + k)
            return carry

        lax.fori_loop(0, TR // 16, expand_group, 0)
        lax.fori_loop(0, PEER_NKEYS, copy_out, 0, unroll=4)

    @pl.when(pl.program_id(0) < last)
    def _():
        _route_select(x_ref, m_ref, g2_ref, wq_ref, sk_ref, hb_ref, q_scr, i1t_scr, i2t_scr, gtt_scr, expand_token)
        lax.fori_loop(0, PEER_NKEYS, copy_out, 0, unroll=4)
        i1_scr[...] = i1t_scr[...].T
        i2_scr[...] = i2t_scr[...].T
        gt_scr[...] = gtt_scr[...].T


def _route_select(x_ref, m_ref, g2_ref, wq_ref, sk_ref, hb_ref, q_scr, i1t_scr, i2t_scr, gtt_scr, expand_token):
    h = _modnorm(x_ref[...], g2_ref[...], m_ref[0, 3:4, :], m_ref[0, 4:5, :])
    hb = h.astype(BF16)
    hb_ref[...] = hb
    q = _dot(hb, wq_ref[...])
    for hp in range(2 * PEER_HEADS):
        q_scr[hp] = q[:, hp * PEER_DK:(hp + 1) * PEER_DK]
    sub = lax.broadcasted_iota(jnp.int32, (SUB, TR), 0).astype(F32)
    sk_bf = [sk_ref[p].astype(BF16) for p in range(2)]

    def key_stage(hd):
        top = ()
        for p in range(2):
            s = _dot_nt(sk_bf[p], q_scr[2 * hd + p].astype(BF16))
            top += _top16_keys(s, sub)
        return top

    def pair_stage(hd, top):
        sv1, si1, sv2, si2 = top
        cands = _pair_candidates(sv1, sv2, sub)
        fv, ff = _top16_of_blocks([c[0] for c in cands], [c[1] for c in cands], lambda flat: flat)
        ff = ff.astype(jnp.int32)
        fa, fb = ff >> 4, ff & (PEER_TOPK - 1)
        i1 = jnp.zeros((PEER_TOPK, TR), F32)
        i2 = jnp.zeros((PEER_TOPK, TR), F32)
        for a in range(PEER_TOPK):
            i1 = jnp.where(fa == a, si1[a:a + 1, :], i1)
            i2 = jnp.where(fb == a, si2[a:a + 1, :], i2)
        e = jnp.exp(fv - fv[0:1, :])
        g = e / jnp.sum(e, axis=0, keepdims=True)
        start = hd * PEER_TOPK
        rows = pl.ds(start if isinstance(start, int) else pl.multiple_of(start, PEER_TOPK), PEER_TOPK)
        i1t_scr[rows, :] = i1
        i2t_scr[rows, :] = i2
        gtt_scr[rows, :] = g

    n_first, n_mid = TR // 8, (TR - TR // 8) // 8
    top = key_stage(0)
    for k in range(n_first):
        expand_token(k)

    def per_head(hd, top_prev):
        top_cur = key_stage(hd)
        pair_stage(hd - 1, top_prev)
        for k in range(n_mid):
            expand_token(n_first + (hd - 1) * n_mid + k)
        return top_cur

    top = lax.fori_loop(1, PEER_HEADS, per_head, top, unroll=True)
    pair_stage(PEER_HEADS - 1, top)
    for n in range(n_first + (PEER_HEADS - 1) * n_mid, TR):
        expand_token(n)


def _peer_route(x, mods, g2, wq, subkeys, gate_scale):
    n_tiles = N_TOK // TR
    const = lambda shape: pl.BlockSpec(shape, lambda t: (0,) * len(shape))
    cur = lambda t: jnp.minimum(t, n_tiles - 1)
    return pl.pallas_call(
        _route_kernel,
        grid=(n_tiles + 1,),
        in_specs=[
            pl.BlockSpec((TR, D), lambda t: (cur(t), 0)),
            pl.BlockSpec((1, 6, D), lambda t: (_mod_row(cur(t), TR), 0, 0)),
            const((1, D)),
            const((D, PEER_HEADS * 2 * PEER_DK)),
            const((2, PEER_NKEYS, PEER_DK)),
            const((PEER_NKEYS, PEER_NKEYS)),
        ],
        out_specs=[
            pl.BlockSpec((TR, D), lambda t: (cur(t), 0)),
            pl.BlockSpec((PEER_NKEYS, TR, PEER_NKEYS), lambda t: (0, jnp.maximum(t - 1, 0), 0)),
        ],
        out_shape=[
            jax.ShapeDtypeStruct((N_TOK, D), BF16),
            jax.ShapeDtypeStruct((PEER_NKEYS, N_TOK, PEER_NKEYS), BF16),
        ],
        scratch_shapes=[
            pltpu.VMEM((2 * PEER_HEADS, TR, PEER_DK), F32),
            pltpu.VMEM((PEER_HEADS * PEER_TOPK, TR), F32),
            pltpu.VMEM((PEER_HEADS * PEER_TOPK, TR), F32),
            pltpu.VMEM((PEER_HEADS * PEER_TOPK, TR), F32),
            pltpu.VMEM((TR, PEER_HEADS * PEER_TOPK), F32),
            pltpu.VMEM((TR, PEER_HEADS * PEER_TOPK), F32),
            pltpu.VMEM((TR, PEER_HEADS * PEER_TOPK), F32),
            pltpu.VMEM((PEER_NKEYS * G_PITCH, PEER_NKEYS), F32),
        ],
        compiler_params=_cparams(("arbitrary",)),
        name="peer_route",
    )(x, mods, g2, wq, subkeys, gate_scale)


F8 = jnp.float8_e4m3fn
F8_TOP = 256.0
TINY = 1e-30


def _row_scale(a):
    return jnp.maximum(jnp.max(jnp.abs(a), axis=1, keepdims=True), TINY) * (1.0 / F8_TOP)


def _quant_rows_kernel(w_ref, q_ref, s_ref):
    w = w_ref[0]
    scale = _row_scale(w)
    q_ref[0] = (w * (1.0 / scale)).astype(F8)
    s_ref[0, 0] = jnp.transpose(jnp.broadcast_to(scale, (EC, 128)))[0:1, :]


def _quantize_rows(w_all):
    n_chunks = PEER_EXPERTS // EC
    return pl.pallas_call(
        _quant_rows_kernel,
        grid=(DEPTH, n_chunks),
        in_specs=[pl.BlockSpec((1, EC, D), lambda l, j: (l, j, 0))],
        out_specs=[pl.BlockSpec((1, EC, D), lambda l, j: (l, j, 0)),
                   pl.BlockSpec((1, 1, 1, EC), lambda l, j: (l, j, 0, 0))],
        out_shape=[jax.ShapeDtypeStruct((DEPTH, PEER_EXPERTS, D), F8),
                   jax.ShapeDtypeStruct((DEPTH, n_chunks, 1, EC), F32)],
        compiler_params=_cparams(("arbitrary", "arbitrary")),
        name="quantize_rows",
    )(w_all)


def _expert_kernel(x_ref, m_ref, hb_ref, gate_ref, u8_ref, su_ref, v8_ref, *rest):
    o_ref, h8_scr, t_scr, hn_scr = rest[-4:]
    c = pl.program_id(1)

    @pl.when(c == 0)
    def _():
        o_ref[...] = jnp.zeros_like(o_ref)
        hf = hb_ref[...].astype(F32)
        t = jnp.maximum(jnp.max(jnp.abs(hf), keepdims=True), TINY) * (1.0 / F8_TOP)
        t_scr[...] = t
        hq = (hf * (1.0 / t)).astype(F8)
        h8_scr[...] = hq
        hq = hq.astype(F32)
        hn_scr[...] = jnp.sqrt(jnp.sum(hq * hq, axis=1, keepdims=True))

    scale = su_ref[0, 0] * (t_scr[...] * (2.0 ** -0.5))
    scale_max = jnp.max(scale, axis=1, keepdims=True) * (1.01 * (D ** 0.5) * F8_TOP)
    n_slabs = EC // PEER_NKEYS
    for half in range(TE // TE_SUB):
        rows = slice(half * TE_SUB, (half + 1) * TE_SUB)
        h8 = h8_scr[rows, :]
        gmax = gate_ref[0, rows, :]
        for a in range(1, n_slabs):
            gmax = jnp.maximum(gmax, gate_ref[a, rows, :])
        gmax = jnp.max(gmax.astype(F32), axis=1, keepdims=True)
        r = jnp.maximum(gmax * (hn_scr[rows, :] * scale_max) * (2.0 / F8_TOP), TINY)
        inv_r = 1.0 / r
        slabs = []
        for s in range(EC // EC_SUB):
            cols = slice(s * EC_SUB, (s + 1) * EC_SUB)
            z = _dot_nt(h8, u8_ref[0, cols, :]) * scale[:, cols]
            act = z * (1.0 + lax.erf(z))
            for a in range(EC_SUB // PEER_NKEYS):
                gate = gate_ref[s * (EC_SUB // PEER_NKEYS) + a, rows, :].astype(F32) * inv_r
                slabs.append((act[:, a * PEER_NKEYS:(a + 1) * PEER_NKEYS] * gate).astype(F8))
        o_ref[rows, :] += _dot(jnp.concatenate(slabs, axis=1), v8_ref[0]) * r

    @pl.when(c == pl.num_programs(1) - 1)
    def _():
        y = x_ref[...] + m_ref[0, 5:6, :] * o_ref[...]
        if len(rest) == 5:
            y = y * lax.rsqrt(jnp.mean(y * y, axis=-1, keepdims=True) + EPS) * rest[0][...]
        o_ref[...] = y


def _peer_experts(x, mods, hb, gate, u8_all, su_all, v8_all, layer, final_g=None):
    n_tiles = N_TOK // TE
    n_chunks = PEER_EXPERTS // EC
    once = pl.Buffered(1)
    in_specs = [
        pl.BlockSpec((TE, D), lambda t, c: (t, 0), pipeline_mode=once),
        pl.BlockSpec((1, 6, D), lambda t, c: (_mod_row(t, TE), 0, 0)),
        pl.BlockSpec((TE, D), lambda t, c: (t, 0), pipeline_mode=once),
        pl.BlockSpec((EC // PEER_NKEYS, TE, PEER_NKEYS), lambda t, c: (c, t, 0)),
        pl.BlockSpec((1, EC, D), lambda t, c: (layer, c, 0)),
        pl.BlockSpec((1, 1, 1, EC), lambda t, c: (layer, c, 0, 0)),
        pl.BlockSpec((1, EC, D), lambda t, c: (layer, c, 0)),
    ]
    operands = [x, mods, hb, gate, u8_all, su_all, v8_all]
    if final_g is not None:
        in_specs.append(pl.BlockSpec((1, D), lambda t, c: (0, 0)))
        operands.append(final_g)
    return pl.pallas_call(
        _expert_kernel,
        grid=(n_tiles, n_chunks),
        in_specs=in_specs,
        out_specs=pl.BlockSpec((TE, D), lambda t, c: (t, 0)),
        out_shape=jax.ShapeDtypeStruct((N_TOK, D), F32),
        scratch_shapes=[pltpu.VMEM((TE, D), F8), pltpu.VMEM((1, 1), F32), pltpu.VMEM((TE, 1), F32)],
        compiler_params=_cparams(("arbitrary", "arbitrary")),
        name="peer_experts",
    )(*operands)


def _rope_tables():
    t = jnp.arange(DEC_SEQ)
    row, col = t // GRID_W, t % GRID_W
    nf = HEAD_DIM // 4
    inv = ROPE_THETA ** (-jnp.arange(nf, dtype=F32) / nf)
    ang_r = row.astype(F32)[:, None] * inv[None, :]
    ang_c = col.astype(F32)[:, None] * inv[None, :]
    cos = jnp.concatenate([jnp.cos(ang_r)] * 2 + [jnp.cos(ang_c)] * 2, axis=1)
    sin = jnp.concatenate([-jnp.sin(ang_r), jnp.sin(ang_r), -jnp.sin(ang_c), jnp.sin(ang_c)], axis=1)
    cos = jnp.concatenate([jnp.ones((TM, HEAD_DIM), F32), cos], axis=0)
    sin = jnp.concatenate([jnp.zeros((TM, HEAD_DIM), F32), sin], axis=0)
    return jnp.tile(cos, (1, N_HEADS)), jnp.tile(sin, (1, N_HEADS))


def _head_block_diag():
    i = np.arange(128)
    return jnp.asarray((i[:, None] // HEAD_DIM == i[None, :] // HEAD_DIM).astype(np.float32), dtype=BF16)


def kernel(x_prompt, x_sample, cache_k, cache_v, c, c_ctx, mod_w, mod_b, norm1_g, norm2_g, a_in_w, a_out_w,
           q_norm_g, k_norm_g, pool_w, pool_scale, c_in_w, c_conv_w, c_out_w, peer_wq, peer_subkeys, peer_u,
           peer_v, final_g):
    x = jnp.concatenate([x_prompt.reshape(NP_TOK, D), x_sample.reshape(NS_TOK, D)], axis=0)
    ct = jnp.zeros((D, 8), F32).at[:, 0].set(c_ctx).at[:, 1:3].set(c.T)
    mods = _modulation(ct, mod_w, mod_b).reshape(DEPTH, 8, 6, D)
    rope_c, rope_s = _rope_tables()
    bd = _head_block_diag()
    wq = peer_wq.astype(BF16)
    u8, su = _quantize_rows(peer_u)
    v8, sv = _quantize_rows(peer_v)
    new_k, new_v = [], []
    for l in range(DEPTH):
        g1 = norm1_g[l].reshape(1, D)
        if l % 2 == 0:
            i = l // 2
            q, kk, vv, knew, vnew, p = _attn_in(
                x, mods[l], g1, a_in_w[i].astype(BF16), jnp.tile(q_norm_g[i], N_HEADS).reshape(1, ATTN_W),
                jnp.tile(k_norm_g[i], N_KV).reshape(1, KV_W), bd, rope_c, rope_s)
            new_k.append(knew[:NP_TOK].reshape(N_PROMPT_SEQ, SEQ, N_KV, HEAD_DIM))
            new_v.append(vnew[:NP_TOK].reshape(N_PROMPT_SEQ, SEQ, N_KV, HEAD_DIM))
            ck = cache_k[:, i]
            cv = cache_v[:, i]
            ctx_kk = jnp.concatenate([ck, ck[:, :, ::-1]], axis=2).reshape(N_SAMPLE_SEQ, PAST, 2 * KV_W).astype(BF16)
            ctx_vv = jnp.concatenate([cv, cv[:, :, ::-1]], axis=2).reshape(N_SAMPLE_SEQ, PAST, 2 * KV_W).astype(BF16)
            kk_p = kk[:NP_TOK].reshape(N_PROMPT_SEQ, SEQ, 2 * KV_W)
            vv_p = vv[:NP_TOK].reshape(N_PROMPT_SEQ, SEQ, 2 * KV_W)
            kk_s = jnp.concatenate([kk[NP_TOK:].reshape(N_SAMPLE_SEQ, DEC_SEQ, 2 * KV_W), ctx_kk], axis=1)
            vv_s = jnp.concatenate([vv[NP_TOK:].reshape(N_SAMPLE_SEQ, DEC_SEQ, 2 * KV_W), ctx_vv], axis=1)
            common = (pool_w[i].astype(BF16), pool_scale[i].reshape(1, POOL_W), a_out_w[i].astype(BF16))
            x = _attn_out(x, mods[l], q, kk_p, vv_p, p, *common,
                          first_tile=0, n_tiles=N_PROMPT_SEQ, tiles_per_seq=SEQ // TM)
            x = _attn_out(x, mods[l], q, kk_s, vv_s, p, *common,
                          first_tile=N_PROMPT_SEQ, n_tiles=NS_TOK // TM, tiles_per_seq=DEC_SEQ // TM)
        else:
            j = l // 2
            cw = jnp.zeros((8, D), F32).at[:3].set(c_conv_w[j])
            x = _conv_layer(x, mods[l], g1, c_in_w[j].astype(BF16), cw, c_out_w[j].astype(BF16))
        gate_scale = (2.0 ** -0.5) * sv[l].reshape(PEER_NKEYS, PEER_NKEYS)
        hb, gate = _peer_route(x, mods[l], norm2_g[l].reshape(1, D), wq[l], peer_subkeys[l], gate_scale)
        x = _peer_experts(x, mods[l], hb, gate, u8, su, v8, l,
                          final_g=final_g.reshape(1, D) if l == DEPTH - 1 else None)
    y = x
    y_prompt = y[:NP_TOK].reshape(N_PROMPT_SEQ, SEQ, D)
    y_sample = y[NP_TOK:].reshape(N_SAMPLE_SEQ, DEC_SEQ, D)
    return (y_prompt, y_sample, jnp.stack(new_k, axis=1), jnp.stack(new_v, axis=1))
```
